```python
import jax, jax.numpy as jnp
from jax import lax
import numpy as np

D_MODEL = 1024
BATCH = 32
SEQ = 2048
DEPTH = 1
DEC_BATCH = 8
DEC_SEQ = 4096
PAST_LEN = 128

D_REC = 512
N_REC_HEADS = 8
REC_HEAD = D_REC // N_REC_HEADS
CONV_W = 4
CONV_LEFT = 2
LRU_C = 8.0
D_SGU = 512
N_SGU_HEADS = 8
SGU_HEAD = D_SGU // N_SGU_HEADS
CHUNK = 128
D_MIX = D_REC + D_SGU
D_IN = 2 * D_REC + 2 * D_SGU
N_GROUPS = 4
N_EXP_PER_GROUP = 4
N_EXPERTS = N_GROUPS * N_EXP_PER_GROUP
TOP_K = 2
D_EXPERT = 512
D_PLE = 256
EPS = 1e-6

kernel_name = "hymba_style_rglru_sgu_hiermoe_encoder"


def rms_norm(x, g):
    xf = x.astype(jnp.float32)
    y = xf * lax.rsqrt(jnp.mean(xf * xf, axis=-1, keepdims=True) + EPS)
    return (y * g.astype(jnp.float32)).astype(x.dtype)


def layer_norm(x, g, b):
    xf = x.astype(jnp.float32)
    mu = jnp.mean(xf, axis=-1, keepdims=True)
    xc = xf - mu
    var = jnp.mean(xc * xc, axis=-1, keepdims=True)
    y = xc * lax.rsqrt(var + EPS) * g.astype(jnp.float32) + b.astype(jnp.float32)
    return y.astype(x.dtype)


def centred_depthwise_conv(x, w, b):
    s = x.shape[1]
    xp = jnp.pad(x, ((0, 0), (CONV_LEFT, CONV_W - 1 - CONV_LEFT), (0, 0)))
    return sum(xp[:, k:k + s] * w[k] for k in range(CONV_W)) + b


def _lru_combine(left, right):
    a_l, h_l = left
    a_r, h_r = right
    return a_l * a_r, a_r * h_l + h_r


def rg_lru_direction(x, w_a, b_a, w_x, b_x, lam, reverse):
    bsz, s, _ = x.shape
    xh = x.reshape(bsz, s, N_REC_HEADS, REC_HEAD)
    r = jax.nn.sigmoid(jnp.einsum('bshi,hij->bshj', xh, w_a).reshape(bsz, s, D_REC) + b_a)
    i = jax.nn.sigmoid(jnp.einsum('bshi,hij->bshj', xh, w_x).reshape(bsz, s, D_REC) + b_x)
    log_a = -LRU_C * r * jax.nn.softplus(-lam.astype(jnp.float32))
    a = jnp.exp(log_a)
    mult = jnp.sqrt(-jnp.expm1(2.0 * log_a))
    _, h = lax.associative_scan(_lru_combine, (a, mult * (i * x)), axis=1, reverse=reverse)
    return h


def recurrent_group(xr, gr, conv_w, conv_b, w_a, b_a, w_x, b_x, lam):
    xc = centred_depthwise_conv(xr, conv_w, conv_b).astype(jnp.float32)
    h = (rg_lru_direction(xc, w_a[0], b_a[0], w_x[0], b_x[0], lam[0], False)
         + rg_lru_direction(xc, w_a[1], b_a[1], w_x[1], b_x[1], lam[1], True))
    return h.astype(xr.dtype) * jax.nn.gelu(gr)


def spatial_gating_group(u, v, ln_g, ln_b, w_s, b_s):
    bsz, s, _ = v.shape
    vn = layer_norm(v, ln_g, ln_b).reshape(bsz, s // CHUNK, CHUNK, N_SGU_HEADS, SGU_HEAD)
    mixed = jnp.einsum('hts,bnshc->bnthc', w_s, vn) + b_s.T[:, :, None]
    return u * mixed.reshape(bsz, s, D_SGU)


def hierarchical_moe(n, w_rg, b_rg, w_re, b_re, w1, w3, w2):
    bsz, s, d = n.shape
    t = n.reshape(bsz * s, d)
    g_prob = jax.nn.softmax((t @ w_rg).astype(jnp.float32) + b_rg.astype(jnp.float32), axis=-1)
    g_top, g_idx = lax.top_k(g_prob, 1)
    e_logits = ((t @ w_re).astype(jnp.float32) + b_re.astype(jnp.float32)).reshape(-1, N_GROUPS, N_EXP_PER_GROUP)
    e_logits = jnp.take_along_axis(e_logits, g_idx[:, :, None], axis=1)[:, 0]
    e_top, e_idx = lax.top_k(jax.nn.softmax(e_logits, axis=-1), TOP_K)
    weights = g_top * (e_top / jnp.sum(e_top, axis=-1, keepdims=True))
    expert_id = g_idx * N_EXP_PER_GROUP + e_idx
    gate = jnp.einsum('mk,mke->me', weights,
                      jax.nn.one_hot(expert_id, N_EXPERTS, dtype=jnp.float32)).astype(n.dtype)
    out = jnp.zeros_like(t)
    for e in range(N_EXPERTS):
        hdn = jax.nn.silu(t @ w1[e]) * (t @ w3[e])
        out = out + gate[:, e:e + 1] * (hdn @ w2[e])
    return out.reshape(bsz, s, d)


def encoder_trunk(x, p, W):
    h = x
    for l in range(DEPTH):
        n = rms_norm(h, W['g_mix'][l])
        z = n @ W['w_in'][l]
        xr, gr, zu, zv = jnp.split(z, [D_REC, 2 * D_REC, 2 * D_REC + D_SGU], axis=-1)
        y_rec = recurrent_group(xr, gr, W['conv_w'][l], W['conv_b'][l], W['lru_w_a'][l], W['lru_b_a'][l],
                                W['lru_w_x'][l], W['lru_b_x'][l], W['lru_lambda'][l])
        y_sgu = spatial_gating_group(jax.nn.gelu(zu), jax.nn.gelu(zv), W['sgu_ln_g'][l], W['sgu_ln_b'][l],
                                     W['sgu_w_s'][l], W['sgu_b_s'][l])
        merged = jnp.concatenate([rms_norm(y_rec, W['g_rec_out'][l]), rms_norm(y_sgu, W['g_sgu_out'][l])], axis=-1)
        h = h + merged @ W['w_out'][l]
        h = h + hierarchical_moe(rms_norm(h, W['g_ffn'][l]), W['w_router_group'][l], W['b_router_group'][l],
                                 W['w_router_expert'][l], W['b_router_expert'][l],
                                 W['w_exp_gate'][l], W['w_exp_up'][l], W['w_exp_down'][l])
        gate = jax.nn.sigmoid(rms_norm(h, W['g_ple'][l]) @ W['w_ple_gate'][l])
        h = h + gate * (p[l] @ W['w_ple_proj'][l])
    return rms_norm(h, W['g_final'])


def setup_inputs(seed: int = 0) -> dict:
    key = jax.random.key(seed)
    ks = jax.random.split(key, 40)
    f32 = jnp.float32

    def nrm(i, shape, scale):
        return scale * jax.random.normal(ks[i], shape, f32)

    def gain(i, shape):
        return 1.0 + 0.1 * jax.random.normal(ks[i], shape, f32)

    u = jax.random.uniform(ks[10], (DEPTH, 2, D_REC), f32, minval=0.9, maxval=0.999)
    s = u ** (1.0 / LRU_C)
    lru_lambda = jnp.log(s) - jnp.log1p(-s)
    return {
        "x_prompt": nrm(0, (BATCH, SEQ, D_MODEL), 1.0),
        "x_sample": nrm(1, (DEC_BATCH, DEC_SEQ, D_MODEL), 1.0),
        "p_prompt": nrm(2, (DEPTH, BATCH, SEQ, D_PLE), 1.0),
        "p_sample": nrm(3, (DEPTH, DEC_BATCH, DEC_SEQ, D_PLE), 1.0),
        "g_mix": gain(4, (DEPTH, D_MODEL)),
        "w_in": nrm(5, (DEPTH, D_MODEL, D_IN), D_MODEL ** -0.5),
        "conv_w": nrm(6, (DEPTH, CONV_W, D_REC), CONV_W ** -0.5),
        "conv_b": nrm(7, (DEPTH, D_REC), 0.02),
        "lru_w_a": nrm(8, (DEPTH, 2, N_REC_HEADS, REC_HEAD, REC_HEAD), REC_HEAD ** -0.5),
        "lru_b_a": nrm(9, (DEPTH, 2, D_REC), 0.1),
        "lru_w_x": nrm(11, (DEPTH, 2, N_REC_HEADS, REC_HEAD, REC_HEAD), REC_HEAD ** -0.5),
        "lru_b_x": nrm(12, (DEPTH, 2, D_REC), 0.1),
        "lru_lambda": lru_lambda,
        "sgu_ln_g": gain(13, (DEPTH, D_SGU)),
        "sgu_ln_b": nrm(14, (DEPTH, D_SGU), 0.02),
        "sgu_w_s": nrm(15, (DEPTH, N_SGU_HEADS, CHUNK, CHUNK), CHUNK ** -0.5),
        "sgu_b_s": gain(16, (DEPTH, N_SGU_HEADS, CHUNK)),
        "g_rec_out": gain(17, (DEPTH, D_REC)),
        "g_sgu_out": gain(18, (DEPTH, D_SGU)),
        "w_out": nrm(19, (DEPTH, D_MIX, D_MODEL), D_MIX ** -0.5),
        "g_ffn": gain(20, (DEPTH, D_MODEL)),
        "w_router_group": nrm(21, (DEPTH, D_MODEL, N_GROUPS), D_MODEL ** -0.5),
        "b_router_group": nrm(22, (DEPTH, N_GROUPS), 0.01),
        "w_router_expert": nrm(23, (DEPTH, D_MODEL, N_EXPERTS), D_MODEL ** -0.5),
        "b_router_expert": nrm(24, (DEPTH, N_EXPERTS), 0.01),
        "w_exp_gate": nrm(25, (DEPTH, N_EXPERTS, D_MODEL, D_EXPERT), D_MODEL ** -0.5),
        "w_exp_up": nrm(26, (DEPTH, N_EXPERTS, D_MODEL, D_EXPERT), D_MODEL ** -0.5),
        "w_exp_down": nrm(27, (DEPTH, N_EXPERTS, D_EXPERT, D_MODEL), D_EXPERT ** -0.5),
        "g_ple": gain(28, (DEPTH, D_MODEL)),
        "w_ple_gate": nrm(29, (DEPTH, D_MODEL, D_MODEL), D_MODEL ** -0.5),
        "w_ple_proj": nrm(30, (DEPTH, D_PLE, D_MODEL), D_PLE ** -0.5),
        "g_final": gain(31, (D_MODEL,)),
    }


def reference(x_prompt, x_sample, p_prompt, p_sample, g_mix, w_in, conv_w, conv_b, lru_w_a, lru_b_a,
              lru_w_x, lru_b_x, lru_lambda, sgu_ln_g, sgu_ln_b, sgu_w_s, sgu_b_s, g_rec_out, g_sgu_out,
              w_out, g_ffn, w_router_group, b_router_group, w_router_expert, b_router_expert,
              w_exp_gate, w_exp_up, w_exp_down, g_ple, w_ple_gate, w_ple_proj, g_final):
    W = {
        'g_mix': g_mix, 'w_in': w_in, 'conv_w': conv_w, 'conv_b': conv_b,
        'lru_w_a': lru_w_a, 'lru_b_a': lru_b_a, 'lru_w_x': lru_w_x, 'lru_b_x': lru_b_x,
        'lru_lambda': lru_lambda, 'sgu_ln_g': sgu_ln_g, 'sgu_ln_b': sgu_ln_b,
        'sgu_w_s': sgu_w_s, 'sgu_b_s': sgu_b_s, 'g_rec_out': g_rec_out, 'g_sgu_out': g_sgu_out,
        'w_out': w_out, 'g_ffn': g_ffn, 'w_router_group': w_router_group,
        'b_router_group': b_router_group, 'w_router_expert': w_router_expert,
        'b_router_expert': b_router_expert, 'w_exp_gate': w_exp_gate, 'w_exp_up': w_exp_up,
        'w_exp_down': w_exp_down, 'g_ple': g_ple, 'w_ple_gate': w_ple_gate,
        'w_ple_proj': w_ple_proj, 'g_final': g_final,
    }
    y_prompt = encoder_trunk(x_prompt, p_prompt, W)
    y_sample = encoder_trunk(x_sample, p_sample, W)
    return (y_prompt, y_sample)
```

```python
import functools

import jax
import jax.numpy as jnp
from jax import lax
from jax.experimental import pallas as pl
from jax.experimental.pallas import tpu as pltpu

f32 = jnp.float32
bf16 = jnp.bfloat16

D_MODEL = 1024
D_REC = 512
N_REC_HEADS = 8
REC_HEAD = 64
CONV_W = 4
CONV_LEFT = 2
LRU_C = 8.0
D_SGU = 512
N_SGU_HEADS = 8
SGU_HEAD = 64
CHUNK = 128
N_GROUPS = 4
N_EXP_PER_GROUP = 4
N_EXPERTS = 16
D_EXPERT = 512
D_PLE = 256
EPS = 1e-6

LANES = 128
SUBLANES = 8
NB = SUBLANES
ST = CHUNK
TT = NB * ST
HALO = SUBLANES
PITCH = ST + SUBLANES
N_LANE_GROUPS = D_REC // LANES
MXU_DIM = 256
TM = 512
N_PAIRS = 6
N_CLASSES = N_GROUPS * N_PAIRS
ROUTE_ROWS = 32
VMEM_LIMIT = 60 * 1024 * 1024


def _rms(x, g):
    ms = jnp.mean(x * x, axis=-1, keepdims=True)
    return x * lax.rsqrt(ms + EPS) * g


def _softplus(x):
    return jnp.maximum(x, 0.0) + jnp.log1p(jnp.exp(-jnp.abs(x)))


def _gate_ab(xc, wg_ref, ba, bx, lam):
    xcb = xc.astype(bf16)
    sp = _softplus(-lam)
    out = []
    for g in range(N_LANE_GROUPS):
        sl = slice(LANES * g, LANES * (g + 1))
        pre = jnp.dot(xcb[:, sl], wg_ref[g], preferred_element_type=f32)
        r = jax.nn.sigmoid(pre[:, :LANES] + ba[:, sl])
        i = jax.nn.sigmoid(pre[:, LANES:] + bx[:, sl])
        log_a = -LRU_C * r * sp[:, sl]
        a = jnp.exp(log_a)
        mult = jnp.sqrt(jnp.tanh(-log_a) * (1.0 + a * a))
        out.append((a, mult * (i * xc[:, sl])))
    return out


def _store_ab(ab, a_ref, b_ref):
    for g, (a, b) in enumerate(ab):
        for s in range(NB):
            a_ref[g, s * PITCH:s * PITCH + ST, :] = a[s * ST:(s + 1) * ST, :]
            b_ref[g, s * PITCH:s * PITCH + ST, :] = b[s * ST:(s + 1) * ST, :]


def _scan(a_ref, b_ref, carry_ref, reverse):
    def step(k, hs):
        t = (ST - 1 - k) if reverse else k
        new = []
        for g in range(N_LANE_GROUPS):
            a = a_ref[g, pl.ds(t, NB, stride=PITCH), :]
            b = b_ref[g, pl.ds(t, NB, stride=PITCH), :]
            h = a * hs[g] + b
            b_ref[g, pl.ds(t, NB, stride=PITCH), :] = h
            new.append(h)
        return tuple(new)

    init = tuple(carry_ref[:, LANES * g:LANES * (g + 1)] for g in range(N_LANE_GROUPS))
    hs = lax.fori_loop(0, ST, step, init, unroll=8)
    for g in range(N_LANE_GROUPS):
        carry_ref[:, LANES * g:LANES * (g + 1)] = hs[g]


def _bwd_kernel(x_ref, xp_ref, xn_ref, gmix_ref, wxr_ref, cw_ref, cb_ref, wg_ref, ba_ref, bx_ref, lam_ref,
                xc_ref, hb_ref, ext_ref, a_ref, b_ref, carry_ref):
    j = pl.program_id(1)
    n_t = pl.num_programs(1)
    tb = n_t - 1 - j

    @pl.when(j == 0)
    def _():
        carry_ref[...] = jnp.zeros_like(carry_ref)

    gm = gmix_ref[...]

    def proj(xv):
        return jnp.dot(_rms(xv, gm).astype(bf16), wxr_ref[...], preferred_element_type=f32)

    xr = proj(x_ref[...].reshape(TT, D_MODEL))
    ext_ref[:, HALO:HALO + ST, :] = xr.reshape(NB, ST, D_REC)
    hp = proj(xp_ref[...].reshape(NB * HALO, D_MODEL)).reshape(NB, HALO, D_REC)
    ext_ref[:, 0:HALO, :] = jnp.where(tb > 0, hp, 0.0)
    hn = proj(xn_ref[...].reshape(NB * HALO, D_MODEL)).reshape(NB, HALO, D_REC)
    ext_ref[:, HALO + ST:2 * HALO + ST, :] = jnp.where(tb < n_t - 1, hn, 0.0)

    cw = cw_ref[...]
    xc = None
    for k in range(CONV_W):
        off = HALO - CONV_LEFT + k
        term = ext_ref[:, off:off + ST, :] * cw[k:k + 1, :]
        xc = term if xc is None else xc + term
    xc = xc + cb_ref[...]
    xc_ref[...] = xc

    ab = _gate_ab(xc.reshape(TT, D_REC), wg_ref, ba_ref[...], bx_ref[...], lam_ref[...])
    _store_ab(ab, a_ref, b_ref)
    _scan(a_ref, b_ref, carry_ref, reverse=True)
    for g in range(N_LANE_GROUPS):
        for s in range(NB):
            hb_ref[s, :, LANES * g:LANES * (g + 1)] = b_ref[g, s * PITCH:s * PITCH + ST, :]


def _route(logits):
    gl = [logits[i:i + 1, :] for i in range(N_GROUPS)]
    gm = jnp.maximum(jnp.maximum(gl[0], gl[1]), jnp.maximum(gl[2], gl[3]))
    sg = sum(jnp.exp(v - gm) for v in gl)
    g_top = 1.0 / sg
    gidx = jnp.where(gl[0] == gm, 0, jnp.where(gl[1] == gm, 1, jnp.where(gl[2] == gm, 2, 3)))
    el = []
    for k in range(N_EXP_PER_GROUP):
        rows = [logits[N_GROUPS + N_EXP_PER_GROUP * g + k:N_GROUPS + N_EXP_PER_GROUP * g + k + 1, :]
                for g in range(N_GROUPS)]
        el.append(jnp.where(gidx == 0, rows[0], jnp.where(gidx == 1, rows[1], jnp.where(gidx == 2, rows[2], rows[3]))))
    em = jnp.maximum(jnp.maximum(el[0], el[1]), jnp.maximum(el[2], el[3]))
    ee = [jnp.exp(v - em) for v in el]
    se = ee[0] + ee[1] + ee[2] + ee[3]
    p = [v / se for v in ee]
    p1 = jnp.maximum(jnp.maximum(p[0], p[1]), jnp.maximum(p[2], p[3]))
    i1 = jnp.where(p[0] == p1, 0, jnp.where(p[1] == p1, 1, jnp.where(p[2] == p1, 2, 3)))
    q = [jnp.where(i1 == k, -1.0, p[k]) for k in range(N_EXP_PER_GROUP)]
    p2 = jnp.maximum(jnp.maximum(q[0], q[1]), jnp.maximum(q[2], q[3]))
    i2 = jnp.where(q[0] == p2, 0, jnp.where(q[1] == p2, 1, jnp.where(q[2] == p2, 2, 3)))
    norm = p1 + p2
    w1 = g_top * (p1 / norm)
    w2 = g_top * (p2 / norm)
    first_is_lo = i1 < i2
    lo = jnp.where(first_is_lo, i1, i2)
    hi = jnp.where(first_is_lo, i2, i1)
    w_lo = jnp.where(first_is_lo, w1, w2)
    w_hi = jnp.where(first_is_lo, w2, w1)
    base = jnp.where(lo == 0, 0, jnp.where(lo == 1, 3, 5))
    cls = gidx * N_PAIRS + base + hi - lo - 1
    return cls.astype(f32), w_lo, w_hi


def _fwd_kernel(x_ref, xc_ref, hb_ref, gmix_ref, wgr_ref, wuvt_ref, wg_ref, ba_ref, bx_ref, lam_ref,
                lng_ref, lnb_ref, bd_ref, bst_ref, grec_ref, gsgu_ref, wor_ref, wos_ref, gffn_ref, wrt_ref, brt_ref,
                h1_ref, route_ref, a_ref, b_ref, carry_ref):
    j = pl.program_id(1)

    @pl.when(j == 0)
    def _():
        carry_ref[...] = jnp.zeros_like(carry_ref)

    x2 = x_ref[...].reshape(TT, D_MODEL)
    n = _rms(x2, gmix_ref[...]).astype(bf16)

    ab = _gate_ab(xc_ref[...].reshape(TT, D_REC), wg_ref, ba_ref[...], bx_ref[...], lam_ref[...])
    _store_ab(ab, a_ref, b_ref)
    _scan(a_ref, b_ref, carry_ref, reverse=False)
    hf = jnp.concatenate(
        [jnp.concatenate([b_ref[g, s * PITCH:s * PITCH + ST, :] for s in range(NB)], axis=0)
         for g in range(N_LANE_GROUPS)], axis=1)
    gr = jnp.dot(n, wgr_ref[...], preferred_element_type=f32)
    y_rec = (hf + hb_ref[...].reshape(TT, D_REC)) * jax.nn.gelu(gr)
    rec_n = _rms(y_rec, grec_ref[...]).astype(bf16)

    zt = lax.dot_general(wuvt_ref[...], n, (((1,), (1,)), ((), ())), preferred_element_type=f32)
    ut = jax.nn.gelu(zt[:D_SGU, :])
    vt = jax.nn.gelu(zt[D_SGU:, :])
    mu = jnp.mean(vt, axis=0, keepdims=True)
    vc = vt - mu
    var = jnp.mean(vc * vc, axis=0, keepdims=True)
    vn = (vc * lax.rsqrt(var + EPS) * lng_ref[...] + lnb_ref[...]).astype(bf16)
    n_slab = TT // MXU_DIM
    heads = []
    for h in range(N_SGU_HEADS):
        rows = vn[SGU_HEAD * h:SGU_HEAD * (h + 1), :]
        lhs = jnp.concatenate([rows[:, MXU_DIM * c:MXU_DIM * (c + 1)] for c in range(n_slab)], axis=0)
        res = jnp.dot(lhs, bd_ref[h], preferred_element_type=f32)
        heads.append(jnp.concatenate([res[SGU_HEAD * c:SGU_HEAD * (c + 1), :] for c in range(n_slab)], axis=1))
    mixed = jnp.concatenate(heads, axis=0) + jnp.concatenate([bst_ref[...]] * (TT // CHUNK), axis=1)
    yst = ut * mixed
    ms = jnp.mean(yst * yst, axis=0, keepdims=True)
    sgu_n = (yst * lax.rsqrt(ms + EPS) * gsgu_ref[...]).T.astype(bf16)

    delta = (jnp.dot(rec_n, wor_ref[...], preferred_element_type=f32)
             + jnp.dot(sgu_n, wos_ref[...], preferred_element_type=f32))
    h1 = x2 + delta
    h1_ref[...] = h1.reshape(NB, ST, D_MODEL)

    n2 = _rms(h1, gffn_ref[...]).astype(bf16)
    logits = lax.dot_general(wrt_ref[...], n2, (((1,), (1,)), ((), ())), preferred_element_type=f32) + brt_ref[...]
    cls, w_lo, w_hi = _route(logits)
    route_ref[0] = jnp.concatenate([cls, w_lo, w_hi, jnp.zeros((SUBLANES - 3, TT), f32)], axis=0)


def _expert_kernel(elo_ref, ehi_ref, act_ref, xs_ref, ps_ref, ws_ref, gffn_ref,
                   w1lo_ref, w3lo_ref, w2lo_ref, w1hi_ref, w3hi_ref, w2hi_ref,
                   gple_ref, wpg_ref, wpp_ref, gfin_ref, ys_ref):
    i = pl.program_id(0)

    @pl.when(act_ref[i] > 0)
    def _():
        xs = xs_ref[...]
        n2 = _rms(xs, gffn_ref[...]).astype(bf16)

        def expert(w1_ref, w3_ref, w2_ref):
            a = jnp.dot(n2, w1_ref[0], preferred_element_type=f32)
            b = jnp.dot(n2, w3_ref[0], preferred_element_type=f32)
            hdn = (jax.nn.silu(a) * b).astype(bf16)
            return jnp.dot(hdn, w2_ref[0], preferred_element_type=f32)

        ws = ws_ref[...]
        out = ws[:, 0:1] * expert(w1lo_ref, w3lo_ref, w2lo_ref)
        out = out + ws[:, 1:2] * expert(w1hi_ref, w3hi_ref, w2hi_ref)
        h2 = xs + out
        n3 = _rms(h2, gple_ref[...]).astype(bf16)
        gate = jax.nn.sigmoid(jnp.dot(n3, wpg_ref[...], preferred_element_type=f32))
        proj = jnp.dot(ps_ref[...].astype(bf16), wpp_ref[...], preferred_element_type=f32)
        h3 = h2 + gate * proj
        ys_ref[...] = _rms(h3, gfin_ref[...])


def _const_spec(shape):
    nd = len(shape)
    return pl.BlockSpec(shape, lambda *_: (0,) * nd)


def _gate_blocks(w_a, w_x):
    def pair(w):
        z = jnp.zeros((REC_HEAD, REC_HEAD), w.dtype)
        return jnp.stack([jnp.block([[w[2 * g], z], [z, w[2 * g + 1]]]) for g in range(N_LANE_GROUPS)])
    return jnp.concatenate([pair(w_a), pair(w_x)], axis=-1).astype(bf16)


def _mix_trunk(x, P):
    bsz, seq, _ = x.shape
    assert bsz % NB == 0 and seq % ST == 0
    n_g, n_t = bsz // NB, seq // ST
    halo_blocks = seq // HALO
    row = lambda v: v.reshape(1, -1)

    x_spec_b = pl.BlockSpec((NB, ST, D_MODEL), lambda g, j: (g, n_t - 1 - j, 0))
    xp_spec = pl.BlockSpec((NB, HALO, D_MODEL),
                           lambda g, j: (g, jnp.maximum((n_t - 1 - j) * (ST // HALO) - 1, 0), 0))
    xn_spec = pl.BlockSpec((NB, HALO, D_MODEL),
                           lambda g, j: (g, jnp.minimum((n_t - j) * (ST // HALO), halo_blocks - 1), 0))
    rec_spec_b = pl.BlockSpec((NB, ST, D_REC), lambda g, j: (g, n_t - 1 - j, 0))
    scan_scratch = [pltpu.VMEM((N_LANE_GROUPS, NB * PITCH, LANES), f32),
                    pltpu.VMEM((N_LANE_GROUPS, NB * PITCH, LANES), f32),
                    pltpu.VMEM((NB, D_REC), f32)]

    bwd_w = [row(P['g_mix']), P['w_xr'], P['conv_w'], row(P['conv_b']), P['wg_b'],
             row(P['ba_b']), row(P['bx_b']), row(P['lam_b'])]
    xc, hb = pl.pallas_call(
        _bwd_kernel,
        grid=(n_g, n_t),
        in_specs=[x_spec_b, xp_spec, xn_spec] + [_const_spec(w.shape) for w in bwd_w],
        out_specs=[rec_spec_b, rec_spec_b],
        out_shape=[jax.ShapeDtypeStruct((bsz, seq, D_REC), f32)] * 2,
        scratch_shapes=[pltpu.VMEM((NB, ST + 2 * HALO, D_REC), f32)] + scan_scratch,
        compiler_params=pltpu.CompilerParams(dimension_semantics=("arbitrary", "arbitrary"),
                                             vmem_limit_bytes=VMEM_LIMIT),
        name="bwd_scan",
    )(x, x, x, *bwd_w)

    x_spec_f = pl.BlockSpec((NB, ST, D_MODEL), lambda g, j: (g, j, 0))
    rec_spec_f = pl.BlockSpec((NB, ST, D_REC), lambda g, j: (g, j, 0))
    fwd_w = [row(P['g_mix']), P['w_gr'], P['w_uvt'], P['wg_f'], row(P['ba_f']), row(P['bx_f']), row(P['lam_f']),
             P['ln_g'], P['ln_b'], P['bd'], P['bst'], row(P['g_rec']), P['g_sgu'], P['w_or'], P['w_os'],
             row(P['g_ffn']), P['w_rt'], P['b_rt']]
    h1, route = pl.pallas_call(
        _fwd_kernel,
        grid=(n_g, n_t),
        in_specs=[x_spec_f, rec_spec_f, rec_spec_f] + [_const_spec(w.shape) for w in fwd_w],
        out_specs=[x_spec_f, pl.BlockSpec((1, SUBLANES, TT), lambda g, j: (g * n_t + j, 0, 0))],
        out_shape=[jax.ShapeDtypeStruct((bsz, seq, D_MODEL), f32),
                   jax.ShapeDtypeStruct((n_g * n_t, SUBLANES, TT), f32)],
        scratch_shapes=scan_scratch,
        compiler_params=pltpu.CompilerParams(dimension_semantics=("arbitrary", "arbitrary"),
                                             vmem_limit_bytes=VMEM_LIMIT),
        name="fwd_mix",
    )(x, xc, hb, *fwd_w)

    r = route.reshape(n_g, n_t, SUBLANES, NB, ST)[:, :, :3]
    r = jnp.transpose(r, (2, 0, 3, 1, 4)).reshape(3, bsz * seq)
    return h1.reshape(bsz * seq, D_MODEL), r


def _expert_stage(h1, p, route, P):
    n_tok = h1.shape[0]
    assert n_tok % TM == 0
    n_tiles = n_tok // TM + N_CLASSES
    n_slots = n_tiles * TM
    cls = route[0].astype(jnp.int32)

    onehot = (cls[:, None] == jnp.arange(N_CLASSES, dtype=jnp.int32)[None, :]).astype(jnp.int32)
    counts = jnp.sum(onehot, axis=0)
    rank = jnp.sum((jnp.cumsum(onehot, axis=0) - onehot) * onehot, axis=1)
    tiles_c = (counts + TM - 1) // TM
    tile_end = jnp.cumsum(tiles_c)
    tile_start = tile_end - tiles_c
    slot_of_tok = tile_start[cls] * TM + rank

    order = jnp.argsort(cls, stable=True).astype(jnp.int32)
    dense_start = jnp.cumsum(counts) - counts
    tile_ids = jnp.arange(n_tiles, dtype=jnp.int32)
    tile_cls = jnp.minimum(jnp.searchsorted(tile_end, tile_ids, side='right'), N_CLASSES - 1).astype(jnp.int32)
    active = (tile_ids < tile_end[-1]).astype(jnp.int32)
    slot_cls = jnp.repeat(tile_cls, TM)
    within = jnp.arange(n_slots, dtype=jnp.int32) - jnp.repeat(tile_start[tile_cls], TM) * TM
    valid = (within < counts[slot_cls]) & (jnp.repeat(active, TM) > 0)
    dense_idx = jnp.clip(dense_start[slot_cls] + within, 0, n_tok - 1)
    src = jnp.where(valid, order[dense_idx], 0)

    pair_lo = jnp.array([0, 0, 0, 1, 1, 2], jnp.int32)
    pair_hi = jnp.array([1, 2, 3, 2, 3, 3], jnp.int32)
    e_lo = (tile_cls // N_PAIRS) * N_EXP_PER_GROUP + pair_lo[tile_cls % N_PAIRS]
    e_hi = (tile_cls // N_PAIRS) * N_EXP_PER_GROUP + pair_hi[tile_cls % N_PAIRS]

    xs = jnp.take(h1, src, axis=0)
    ps = jnp.take(p, src, axis=0)
    ws = jnp.pad(jnp.take(route[1:3].T, src, axis=0), ((0, 0), (0, LANES - 2)))

    row = lambda v: v.reshape(1, -1)
    tok_spec = lambda d: pl.BlockSpec((TM, d), lambda i, *_: (i, 0))
    lo_spec = lambda s: pl.BlockSpec((1,) + s, lambda i, elo, ehi, act: (elo[i], 0, 0))
    hi_spec = lambda s: pl.BlockSpec((1,) + s, lambda i, elo, ehi, act: (ehi[i], 0, 0))
    up_shape, down_shape = (D_MODEL, D_EXPERT), (D_EXPERT, D_MODEL)
    consts = [row(P['g_ple']), P['w_pg'], P['w_pp'], row(P['g_final'])]
    ys = pl.pallas_call(
        _expert_kernel,
        grid_spec=pltpu.PrefetchScalarGridSpec(
            num_scalar_prefetch=3,
            grid=(n_tiles,),
            in_specs=[tok_spec(D_MODEL), tok_spec(D_PLE), tok_spec(LANES), _const_spec((1, D_MODEL)),
                      lo_spec(up_shape), lo_spec(up_shape), lo_spec(down_shape),
                      hi_spec(up_shape), hi_spec(up_shape), hi_spec(down_shape)]
                     + [_const_spec(w.shape) for w in consts],
            out_specs=tok_spec(D_MODEL),
        ),
        out_shape=jax.ShapeDtypeStruct((n_slots, D_MODEL), f32),
        compiler_params=pltpu.CompilerParams(dimension_semantics=("arbitrary",), vmem_limit_bytes=VMEM_LIMIT),
        name="experts",
    )(e_lo, e_hi, active, xs, ps, ws, row(P['g_ffn']),
      P['w1'], P['w3'], P['w2'], P['w1'], P['w3'], P['w2'], *consts)
    return jnp.take(ys, slot_of_tok, axis=0)


def _prepare(g_mix, w_in, conv_w, conv_b, lru_w_a, lru_b_a, lru_w_x, lru_b_x, lru_lambda, sgu_ln_g, sgu_ln_b,
             sgu_w_s, sgu_b_s, g_rec_out, g_sgu_out, w_out, g_ffn, w_router_group, b_router_group,
             w_router_expert, b_router_expert, w_exp_gate, w_exp_up, w_exp_down, g_ple, w_ple_gate, w_ple_proj,
             g_final):
    l = 0
    w = w_in[l]
    ws_t = jnp.transpose(sgu_w_s[l], (0, 2, 1))
    z = jnp.zeros((CHUNK, CHUNK), f32)
    bd = jnp.stack([jnp.block([[ws_t[h], z], [z, ws_t[h]]]) for h in range(N_SGU_HEADS)]).astype(bf16)
    w_rt = jnp.concatenate([w_router_group[l].T, w_router_expert[l].T,
                            jnp.zeros((ROUTE_ROWS - N_GROUPS - N_EXPERTS, D_MODEL), f32)], axis=0).astype(bf16)
    b_rt = jnp.concatenate([b_router_group[l], b_router_expert[l],
                            jnp.zeros((ROUTE_ROWS - N_GROUPS - N_EXPERTS,), f32)]).reshape(ROUTE_ROWS, 1)
    return dict(
        g_mix=g_mix[l], w_xr=w[:, :D_REC].astype(bf16), w_gr=w[:, D_REC:2 * D_REC].astype(bf16),
        w_uvt=w[:, 2 * D_REC:].T.astype(bf16),
        conv_w=conv_w[l], conv_b=conv_b[l],
        wg_f=_gate_blocks(lru_w_a[l, 0], lru_w_x[l, 0]), wg_b=_gate_blocks(lru_w_a[l, 1], lru_w_x[l, 1]),
        ba_f=lru_b_a[l, 0], ba_b=lru_b_a[l, 1], bx_f=lru_b_x[l, 0], bx_b=lru_b_x[l, 1],
        lam_f=lru_lambda[l, 0], lam_b=lru_lambda[l, 1],
        ln_g=sgu_ln_g[l].reshape(D_SGU, 1), ln_b=sgu_ln_b[l].reshape(D_SGU, 1),
        bd=bd, bst=jnp.repeat(sgu_b_s[l], SGU_HEAD, axis=0),
        g_rec=g_rec_out[l], g_sgu=g_sgu_out[l].reshape(D_SGU, 1),
        w_or=w_out[l, :D_REC].astype(bf16), w_os=w_out[l, D_REC:].astype(bf16),
        g_ffn=g_ffn[l], w_rt=w_rt, b_rt=b_rt,
        w1=w_exp_gate[l].astype(bf16), w3=w_exp_up[l].astype(bf16), w2=w_exp_down[l].astype(bf16),
        g_ple=g_ple[l], w_pg=w_ple_gate[l].astype(bf16), w_pp=w_ple_proj[l].astype(bf16), g_final=g_final,
    )


def kernel(x_prompt, x_sample, p_prompt, p_sample, g_mix, w_in, conv_w, conv_b, lru_w_a, lru_b_a, lru_w_x, lru_b_x, lru_lambda, sgu_ln_g, sgu_ln_b, sgu_w_s, sgu_b_s, g_rec_out, g_sgu_out, w_out, g_ffn, w_router_group, b_router_group, w_router_expert, b_router_expert, w_exp_gate, w_exp_up, w_exp_down, g_ple, w_ple_gate, w_ple_proj, g_final):
    assert w_in.shape[0] == 1, "single-layer trunk"
    P = _prepare(g_mix, w_in, conv_w, conv_b, lru_w_a, lru_b_a, lru_w_x, lru_b_x, lru_lambda, sgu_ln_g, sgu_ln_b,
                 sgu_w_s, sgu_b_s, g_rec_out, g_sgu_out, w_out, g_ffn, w_router_group, b_router_group,
                 w_router_expert, b_router_expert, w_exp_gate, w_exp_up, w_exp_down, g_ple, w_ple_gate,
                 w_ple_proj, g_final)
    h1_p, r_p = _mix_trunk(x_prompt, P)
    h1_s, r_s = _mix_trunk(x_sample, P)
    n_p = h1_p.shape[0]
    h1 = jnp.concatenate([h1_p, h1_s], axis=0)
    p = jnp.concatenate([p_prompt[0].reshape(-1, D_PLE), p_sample[0].reshape(-1, D_PLE)], axis=0)
    route = jnp.concatenate([r_p, r_s], axis=1)
    y = _expert_stage(h1, p, route, P)
    return (y[:n_p].reshape(x_prompt.shape), y[n_p:].reshape(x_sample.shape))
```

```python
import functools

import jax
import jax.numpy as jnp
from jax import lax
from jax.experimental import pallas as pl
from jax.experimental.pallas import tpu as pltpu

f32 = jnp.float32
bf16 = jnp.bfloat16

D_MODEL = 1024
D_REC = 512
N_REC_HEADS = 8
REC_HEAD = 64
CONV_W = 4
CONV_LEFT = 2
LRU_C = 8.0
D_SGU = 512
N_SGU_HEADS = 8
SGU_HEAD = 64
CHUNK = 128
N_GROUPS = 4
N_EXP_PER_GROUP = 4
N_EXPERTS = 16
D_EXPERT = 512
D_PLE = 256
EPS = 1e-6

LANES = 128
SUBLANES = 8
NB = SUBLANES
ST = CHUNK
TT = NB * ST
HALO = SUBLANES
PITCH = ST + SUBLANES
N_LANE_GROUPS = D_REC // LANES
MXU_DIM = 256
TM = 512
N_PAIRS = 6
N_CLASSES = N_GROUPS * N_PAIRS
ROUTE_ROWS = 32
VMEM_LIMIT = 60 * 1024 * 1024


def _rms(x, g):
    ms = jnp.mean(x * x, axis=-1, keepdims=True)
    return x * lax.rsqrt(ms + EPS) * g


def _softplus(x):
    return jnp.maximum(x, 0.0) + jnp.log1p(jnp.exp(-jnp.abs(x)))


def _gate_ab(xc, wg_ref, ba, bx, lam):
    xcb = xc.astype(bf16)
    sp = _softplus(-lam)
    out = []
    for g in range(N_LANE_GROUPS):
        sl = slice(LANES * g, LANES * (g + 1))
        pre = jnp.dot(xcb[:, sl], wg_ref[g], preferred_element_type=f32)
        r = jax.nn.sigmoid(pre[:, :LANES] + ba[:, sl])
        i = jax.nn.sigmoid(pre[:, LANES:] + bx[:, sl])
        log_a = -LRU_C * r * sp[:, sl]
        a = jnp.exp(log_a)
        mult = jnp.sqrt(jnp.tanh(-log_a) * (1.0 + a * a))
        out.append((a, mult * (i * xc[:, sl])))
    return out


def _store_ab(ab, a_ref, b_ref):
    for g, (a, b) in enumerate(ab):
        for s in range(NB):
            a_ref[g, s * PITCH:s * PITCH + ST, :] = a[s * ST:(s + 1) * ST, :]
            b_ref[g, s * PITCH:s * PITCH + ST, :] = b[s * ST:(s + 1) * ST, :]


def _scan(a_ref, b_ref, carry_ref, reverse):
    def step(k, hs):
        t = (ST - 1 - k) if reverse else k
        new = []
        for g in range(N_LANE_GROUPS):
            a = a_ref[g, pl.ds(t, NB, stride=PITCH), :]
            b = b_ref[g, pl.ds(t, NB, stride=PITCH), :]
            h = a * hs[g] + b
            b_ref[g, pl.ds(t, NB, stride=PITCH), :] = h
            new.append(h)
        return tuple(new)

    init = tuple(carry_ref[:, LANES * g:LANES * (g + 1)] for g in range(N_LANE_GROUPS))
    hs = lax.fori_loop(0, ST, step, init, unroll=8)
    for g in range(N_LANE_GROUPS):
        carry_ref[:, LANES * g:LANES * (g + 1)] = hs[g]


def _bwd_kernel(x_ref, xp_ref, xn_ref, gmix_ref, wxr_ref, cw_ref, cb_ref, wg_ref, ba_ref, bx_ref, lam_ref,
                xc_ref, hb_ref, ext_ref, a_ref, b_ref, carry_ref):
    j = pl.program_id(1)
    n_t = pl.num_programs(1)
    tb = n_t - 1 - j

    @pl.when(j == 0)
    def _():
        carry_ref[...] = jnp.zeros_like(carry_ref)

    gm = gmix_ref[...]

    def proj(xv):
        return jnp.dot(_rms(xv, gm).astype(bf16), wxr_ref[...], preferred_element_type=f32)

    xr = proj(x_ref[...].reshape(TT, D_MODEL))
    ext_ref[:, HALO:HALO + ST, :] = xr.reshape(NB, ST, D_REC)
    hp = proj(xp_ref[...].reshape(NB * HALO, D_MODEL)).reshape(NB, HALO, D_REC)
    ext_ref[:, 0:HALO, :] = jnp.where(tb > 0, hp, 0.0)
    hn = proj(xn_ref[...].reshape(NB * HALO, D_MODEL)).reshape(NB, HALO, D_REC)
    ext_ref[:, HALO + ST:2 * HALO + ST, :] = jnp.where(tb < n_t - 1, hn, 0.0)

    cw = cw_ref[...]
    xc = None
    for k in range(CONV_W):
        off = HALO - CONV_LEFT + k
        term = ext_ref[:, off:off + ST, :] * cw[k:k + 1, :]
        xc = term if xc is None else xc + term
    xc = xc + cb_ref[...]
    xc_ref[...] = xc

    ab = _gate_ab(xc.reshape(TT, D_REC), wg_ref, ba_ref[...], bx_ref[...], lam_ref[...])
    _store_ab(ab, a_ref, b_ref)
    _scan(a_ref, b_ref, carry_ref, reverse=True)
    for g in range(N_LANE_GROUPS):
        for s in range(NB):
            hb_ref[s, :, LANES * g:LANES * (g + 1)] = b_ref[g, s * PITCH:s * PITCH + ST, :]


def _route(logits):
    gl = [logits[i:i + 1, :] for i in range(N_GROUPS)]
    gm = jnp.maximum(jnp.maximum(gl[0], gl[1]), jnp.maximum(gl[2], gl[3]))
    sg = sum(jnp.exp(v - gm) for v in gl)
    g_top = 1.0 / sg
    gidx = jnp.where(gl[0] == gm, 0, jnp.where(gl[1] == gm, 1, jnp.where(gl[2] == gm, 2, 3)))
    el = []
    for k in range(N_EXP_PER_GROUP):
        rows = [logits[N_GROUPS + N_EXP_PER_GROUP * g + k:N_GROUPS + N_EXP_PER_GROUP * g + k + 1, :]
                for g in range(N_GROUPS)]
        el.append(jnp.where(gidx == 0, rows[0], jnp.where(gidx == 1, rows[1], jnp.where(gidx == 2, rows[2], rows[3]))))
    em = jnp.maximum(jnp.maximum(el[0], el[1]), jnp.maximum(el[2], el[3]))
    ee = [jnp.exp(v - em) for v in el]
    se = ee[0] + ee[1] + ee[2] + ee[3]
    p = [v / se for v in ee]
    p1 = jnp.maximum(jnp.maximum(p[0], p[1]), jnp.maximum(p[2], p[3]))
    i1 = jnp.where(p[0] == p1, 0, jnp.where(p[1] == p1, 1, jnp.where(p[2] == p1, 2, 3)))
    q = [jnp.where(i1 == k, -1.0, p[k]) for k in range(N_EXP_PER_GROUP)]
    p2 = jnp.maximum(jnp.maximum(q[0], q[1]), jnp.maximum(q[2], q[3]))
    i2 = jnp.where(q[0] == p2, 0, jnp.where(q[1] == p2, 1, jnp.where(q[2] == p2, 2, 3)))
    norm = p1 + p2
    w1 = g_top * (p1 / norm)
    w2 = g_top * (p2 / norm)
    first_is_lo = i1 < i2
    lo = jnp.where(first_is_lo, i1, i2)
    hi = jnp.where(first_is_lo, i2, i1)
    w_lo = jnp.where(first_is_lo, w1, w2)
    w_hi = jnp.where(first_is_lo, w2, w1)
    base = jnp.where(lo == 0, 0, jnp.where(lo == 1, 3, 5))
    cls = gidx * N_PAIRS + base + hi - lo - 1
    return cls.astype(f32), w_lo, w_hi


def _fwd_kernel(x_ref, xc_ref, hb_ref, gmix_ref, wgr_ref, wuvt_ref, wg_ref, ba_ref, bx_ref, lam_ref,
                lng_ref, lnb_ref, bd_ref, bst_ref, grec_ref, gsgu_ref, wor_ref, wos_ref, gffn_ref, wrt_ref, brt_ref,
                h1_ref, route_ref, a_ref, b_ref, carry_ref):
    j = pl.program_id(1)

    @pl.when(j == 0)
    def _():
        carry_ref[...] = jnp.zeros_like(carry_ref)

    x2 = x_ref[...].reshape(TT, D_MODEL)
    n = _rms(x2, gmix_ref[...]).astype(bf16)

    ab = _gate_ab(xc_ref[...].reshape(TT, D_REC), wg_ref, ba_ref[...], bx_ref[...], lam_ref[...])
    _store_ab(ab, a_ref, b_ref)
    _scan(a_ref, b_ref, carry_ref, reverse=False)
    hf = jnp.concatenate(
        [jnp.concatenate([b_ref[g, s * PITCH:s * PITCH + ST, :] for s in range(NB)], axis=0)
         for g in range(N_LANE_GROUPS)], axis=1)
    gr = jnp.dot(n, wgr_ref[...], preferred_element_type=f32)
    y_rec = (hf + hb_ref[...].reshape(TT, D_REC)) * jax.nn.gelu(gr)
    rec_n = _rms(y_rec, grec_ref[...]).astype(bf16)

    zt = lax.dot_general(wuvt_ref[...], n, (((1,), (1,)), ((), ())), preferred_element_type=f32)
    ut = jax.nn.gelu(zt[:D_SGU, :])
    vt = jax.nn.gelu(zt[D_SGU:, :])
    mu = jnp.mean(vt, axis=0, keepdims=True)
    vc = vt - mu
    var = jnp.mean(vc * vc, axis=0, keepdims=True)
    vn = (vc * lax.rsqrt(var + EPS) * lng_ref[...] + lnb_ref[...]).astype(bf16)
    n_slab = TT // MXU_DIM
    heads = []
    for h in range(N_SGU_HEADS):
        rows = vn[SGU_HEAD * h:SGU_HEAD * (h + 1), :]
        lhs = jnp.concatenate([rows[:, MXU_DIM * c:MXU_DIM * (c + 1)] for c in range(n_slab)], axis=0)
        res = jnp.dot(lhs, bd_ref[h], preferred_element_type=f32)
        heads.append(jnp.concatenate([res[SGU_HEAD * c:SGU_HEAD * (c + 1), :] for c in range(n_slab)], axis=1))
    mixed = jnp.concatenate(heads, axis=0) + jnp.concatenate([bst_ref[...]] * (TT // CHUNK), axis=1)
    yst = ut * mixed
    ms = jnp.mean(yst * yst, axis=0, keepdims=True)
    sgu_n = (yst * lax.rsqrt(ms + EPS) * gsgu_ref[...]).T.astype(bf16)

    delta = (jnp.dot(rec_n, wor_ref[...], preferred_element_type=f32)
             + jnp.dot(sgu_n, wos_ref[...], preferred_element_type=f32))
    h1 = x2 + delta
    h1_ref[...] = h1.reshape(NB, ST, D_MODEL)

    n2 = _rms(h1, gffn_ref[...]).astype(bf16)
    logits = lax.dot_general(wrt_ref[...], n2, (((1,), (1,)), ((), ())), preferred_element_type=f32) + brt_ref[...]
    cls, w_lo, w_hi = _route(logits)
    rt = jnp.concatenate([cls, w_lo, w_hi, jnp.zeros((LANES - 3, TT), f32)], axis=0).T
    route_ref[...] = rt.reshape(NB, ST, LANES)


def _expert_kernel(elo_ref, ehi_ref, act_ref, xs_ref, ps_ref, ws_ref, gffn_ref,
                   w1lo_ref, w3lo_ref, w2lo_ref, w1hi_ref, w3hi_ref, w2hi_ref,
                   gple_ref, wpg_ref, wpp_ref, gfin_ref, ys_ref):
    i = pl.program_id(0)

    @pl.when(act_ref[i] > 0)
    def _():
        xs = xs_ref[...]
        n2 = _rms(xs, gffn_ref[...]).astype(bf16)

        def expert(w1_ref, w3_ref, w2_ref):
            a = jnp.dot(n2, w1_ref[0], preferred_element_type=f32)
            b = jnp.dot(n2, w3_ref[0], preferred_element_type=f32)
            hdn = (jax.nn.silu(a) * b).astype(bf16)
            return jnp.dot(hdn, w2_ref[0], preferred_element_type=f32)

        ws = ws_ref[...]
        out = ws[:, 1:2] * expert(w1lo_ref, w3lo_ref, w2lo_ref)
        out = out + ws[:, 2:3] * expert(w1hi_ref, w3hi_ref, w2hi_ref)
        h2 = xs + out
        n3 = _rms(h2, gple_ref[...]).astype(bf16)
        gate = jax.nn.sigmoid(jnp.dot(n3, wpg_ref[...], preferred_element_type=f32))
        proj = jnp.dot(ps_ref[...].astype(bf16), wpp_ref[...], preferred_element_type=f32)
        h3 = h2 + gate * proj
        ys_ref[...] = _rms(h3, gfin_ref[...])


def _const_spec(shape):
    nd = len(shape)
    return pl.BlockSpec(shape, lambda *_: (0,) * nd)


def _gate_blocks(w_a, w_x):
    def pair(w):
        z = jnp.zeros((REC_HEAD, REC_HEAD), w.dtype)
        return jnp.stack([jnp.block([[w[2 * g], z], [z, w[2 * g + 1]]]) for g in range(N_LANE_GROUPS)])
    return jnp.concatenate([pair(w_a), pair(w_x)], axis=-1).astype(bf16)


def _mix_trunk(x, P):
    bsz, seq, _ = x.shape
    assert bsz % NB == 0 and seq % ST == 0
    n_g, n_t = bsz // NB, seq // ST
    halo_blocks = seq // HALO
    row = lambda v: v.reshape(1, -1)

    x_spec_b = pl.BlockSpec((NB, ST, D_MODEL), lambda g, j: (g, n_t - 1 - j, 0))
    xp_spec = pl.BlockSpec((NB, HALO, D_MODEL),
                           lambda g, j: (g, jnp.maximum((n_t - 1 - j) * (ST // HALO) - 1, 0), 0))
    xn_spec = pl.BlockSpec((NB, HALO, D_MODEL),
                           lambda g, j: (g, jnp.minimum((n_t - j) * (ST // HALO), halo_blocks - 1), 0))
    rec_spec_b = pl.BlockSpec((NB, ST, D_REC), lambda g, j: (g, n_t - 1 - j, 0))
    scan_scratch = [pltpu.VMEM((N_LANE_GROUPS, NB * PITCH, LANES), f32),
                    pltpu.VMEM((N_LANE_GROUPS, NB * PITCH, LANES), f32),
                    pltpu.VMEM((NB, D_REC), f32)]

    bwd_w = [row(P['g_mix']), P['w_xr'], P['conv_w'], row(P['conv_b']), P['wg_b'],
             row(P['ba_b']), row(P['bx_b']), row(P['lam_b'])]
    xc, hb = pl.pallas_call(
        _bwd_kernel,
        grid=(n_g, n_t),
        in_specs=[x_spec_b, xp_spec, xn_spec] + [_const_spec(w.shape) for w in bwd_w],
        out_specs=[rec_spec_b, rec_spec_b],
        out_shape=[jax.ShapeDtypeStruct((bsz, seq, D_REC), f32)] * 2,
        scratch_shapes=[pltpu.VMEM((NB, ST + 2 * HALO, D_REC), f32)] + scan_scratch,
        compiler_params=pltpu.CompilerParams(dimension_semantics=("arbitrary", "arbitrary"),
                                             vmem_limit_bytes=VMEM_LIMIT),
        name="bwd_scan",
    )(x, x, x, *bwd_w)

    x_spec_f = pl.BlockSpec((NB, ST, D_MODEL), lambda g, j: (g, j, 0))
    rec_spec_f = pl.BlockSpec((NB, ST, D_REC), lambda g, j: (g, j, 0))
    fwd_w = [row(P['g_mix']), P['w_gr'], P['w_uvt'], P['wg_f'], row(P['ba_f']), row(P['bx_f']), row(P['lam_f']),
             P['ln_g'], P['ln_b'], P['bd'], P['bst'], row(P['g_rec']), P['g_sgu'], P['w_or'], P['w_os'],
             row(P['g_ffn']), P['w_rt'], P['b_rt']]
    h1, route = pl.pallas_call(
        _fwd_kernel,
        grid=(n_g, n_t),
        in_specs=[x_spec_f, rec_spec_f, rec_spec_f] + [_const_spec(w.shape) for w in fwd_w],
        out_specs=[x_spec_f, pl.BlockSpec((NB, ST, LANES), lambda g, j: (g, j, 0))],
        out_shape=[jax.ShapeDtypeStruct((bsz, seq, D_MODEL), f32),
                   jax.ShapeDtypeStruct((bsz, seq, LANES), f32)],
        scratch_shapes=scan_scratch,
        compiler_params=pltpu.CompilerParams(dimension_semantics=("arbitrary", "arbitrary"),
                                             vmem_limit_bytes=VMEM_LIMIT),
        name="fwd_mix",
    )(x, xc, hb, *fwd_w)

    return h1.reshape(bsz * seq, D_MODEL), route.reshape(bsz * seq, LANES)


def _expert_stage(h1, p, route, P):
    n_tok = h1.shape[0]
    assert n_tok % TM == 0
    n_tiles = n_tok // TM + N_CLASSES
    n_slots = n_tiles * TM
    cls = route[:, 0].astype(jnp.int32)

    onehot = (cls[:, None] == jnp.arange(N_CLASSES, dtype=jnp.int32)[None, :]).astype(jnp.int32)
    counts = jnp.sum(onehot, axis=0)
    rank = jnp.sum((jnp.cumsum(onehot, axis=0) - onehot) * onehot, axis=1)
    tiles_c = (counts + TM - 1) // TM
    tile_end = jnp.cumsum(tiles_c)
    tile_start = tile_end - tiles_c
    slot_of_tok = tile_start[cls] * TM + rank

    order = jnp.argsort(cls, stable=True).astype(jnp.int32)
    dense_start = jnp.cumsum(counts) - counts
    tile_ids = jnp.arange(n_tiles, dtype=jnp.int32)
    tile_cls = jnp.minimum(jnp.searchsorted(tile_end, tile_ids, side='right'), N_CLASSES - 1).astype(jnp.int32)
    active = (tile_ids < tile_end[-1]).astype(jnp.int32)
    slot_cls = jnp.repeat(tile_cls, TM)
    within = jnp.arange(n_slots, dtype=jnp.int32) - jnp.repeat(tile_start[tile_cls], TM) * TM
    valid = (within < counts[slot_cls]) & (jnp.repeat(active, TM) > 0)
    dense_idx = jnp.clip(dense_start[slot_cls] + within, 0, n_tok - 1)
    src = jnp.where(valid, order[dense_idx], 0)

    pair_lo = jnp.array([0, 0, 0, 1, 1, 2], jnp.int32)
    pair_hi = jnp.array([1, 2, 3, 2, 3, 3], jnp.int32)
    e_lo = (tile_cls // N_PAIRS) * N_EXP_PER_GROUP + pair_lo[tile_cls % N_PAIRS]
    e_hi = (tile_cls // N_PAIRS) * N_EXP_PER_GROUP + pair_hi[tile_cls % N_PAIRS]

    xs = jnp.take(h1, src, axis=0)
    ps = jnp.take(p, src, axis=0)
    ws = jnp.take(route, src, axis=0)

    row = lambda v: v.reshape(1, -1)
    tok_spec = lambda d: pl.BlockSpec((TM, d), lambda i, *_: (i, 0))
    lo_spec = lambda s: pl.BlockSpec((1,) + s, lambda i, elo, ehi, act: (elo[i], 0, 0))
    hi_spec = lambda s: pl.BlockSpec((1,) + s, lambda i, elo, ehi, act: (ehi[i], 0, 0))
    up_shape, down_shape = (D_MODEL, D_EXPERT), (D_EXPERT, D_MODEL)
    consts = [row(P['g_ple']), P['w_pg'], P['w_pp'], row(P['g_final'])]
    ys = pl.pallas_call(
        _expert_kernel,
        grid_spec=pltpu.PrefetchScalarGridSpec(
            num_scalar_prefetch=3,
            grid=(n_tiles,),
            in_specs=[tok_spec(D_MODEL), tok_spec(D_PLE), tok_spec(LANES), _const_spec((1, D_MODEL)),
                      lo_spec(up_shape), lo_spec(up_shape), lo_spec(down_shape),
                      hi_spec(up_shape), hi_spec(up_shape), hi_spec(down_shape)]
                     + [_const_spec(w.shape) for w in consts],
            out_specs=tok_spec(D_MODEL),
        ),
        out_shape=jax.ShapeDtypeStruct((n_slots, D_MODEL), f32),
        compiler_params=pltpu.CompilerParams(dimension_semantics=("arbitrary",), vmem_limit_bytes=VMEM_LIMIT),
        name="experts",
    )(e_lo, e_hi, active, xs, ps, ws, row(P['g_ffn']),
      P['w1'], P['w3'], P['w2'], P['w1'], P['w3'], P['w2'], *consts)
    return jnp.take(ys, slot_of_tok, axis=0)


def _prepare(g_mix, w_in, conv_w, conv_b, lru_w_a, lru_b_a, lru_w_x, lru_b_x, lru_lambda, sgu_ln_g, sgu_ln_b,
             sgu_w_s, sgu_b_s, g_rec_out, g_sgu_out, w_out, g_ffn, w_router_group, b_router_group,
             w_router_expert, b_router_expert, w_exp_gate, w_exp_up, w_exp_down, g_ple, w_ple_gate, w_ple_proj,
             g_final):
    l = 0
    w = w_in[l]
    ws_t = jnp.transpose(sgu_w_s[l], (0, 2, 1))
    z = jnp.zeros((CHUNK, CHUNK), f32)
    bd = jnp.stack([jnp.block([[ws_t[h], z], [z, ws_t[h]]]) for h in range(N_SGU_HEADS)]).astype(bf16)
    w_rt = jnp.concatenate([w_router_group[l].T, w_router_expert[l].T,
                            jnp.zeros((ROUTE_ROWS - N_GROUPS - N_EXPERTS, D_MODEL), f32)], axis=0).astype(bf16)
    b_rt = jnp.concatenate([b_router_group[l], b_router_expert[l],
                            jnp.zeros((ROUTE_ROWS - N_GROUPS - N_EXPERTS,), f32)]).reshape(ROUTE_ROWS, 1)
    return dict(
        g_mix=g_mix[l], w_xr=w[:, :D_REC].astype(bf16), w_gr=w[:, D_REC:2 * D_REC].astype(bf16),
        w_uvt=w[:, 2 * D_REC:].T.astype(bf16),
        conv_w=conv_w[l], conv_b=conv_b[l],
        wg_f=_gate_blocks(lru_w_a[l, 0], lru_w_x[l, 0]), wg_b=_gate_blocks(lru_w_a[l, 1], lru_w_x[l, 1]),
        ba_f=lru_b_a[l, 0], ba_b=lru_b_a[l, 1], bx_f=lru_b_x[l, 0], bx_b=lru_b_x[l, 1],
        lam_f=lru_lambda[l, 0], lam_b=lru_lambda[l, 1],
        ln_g=sgu_ln_g[l].reshape(D_SGU, 1), ln_b=sgu_ln_b[l].reshape(D_SGU, 1),
        bd=bd, bst=jnp.repeat(sgu_b_s[l], SGU_HEAD, axis=0),
        g_rec=g_rec_out[l], g_sgu=g_sgu_out[l].reshape(D_SGU, 1),
        w_or=w_out[l, :D_REC].astype(bf16), w_os=w_out[l, D_REC:].astype(bf16),
        g_ffn=g_ffn[l], w_rt=w_rt, b_rt=b_rt,
        w1=w_exp_gate[l].astype(bf16), w3=w_exp_up[l].astype(bf16), w2=w_exp_down[l].astype(bf16),
        g_ple=g_ple[l], w_pg=w_ple_gate[l].astype(bf16), w_pp=w_ple_proj[l].astype(bf16), g_final=g_final,
    )


def kernel(x_prompt, x_sample, p_prompt, p_sample, g_mix, w_in, conv_w, conv_b, lru_w_a, lru_b_a, lru_w_x, lru_b_x, lru_lambda, sgu_ln_g, sgu_ln_b, sgu_w_s, sgu_b_s, g_rec_out, g_sgu_out, w_out, g_ffn, w_router_group, b_router_group, w_router_expert, b_router_expert, w_exp_gate, w_exp_up, w_exp_down, g_ple, w_ple_gate, w_ple_proj, g_final):
    assert w_in.shape[0] == 1, "single-layer trunk"
    P = _prepare(g_mix, w_in, conv_w, conv_b, lru_w_a, lru_b_a, lru_w_x, lru_b_x, lru_lambda, sgu_ln_g, sgu_ln_b,
                 sgu_w_s, sgu_b_s, g_rec_out, g_sgu_out, w_out, g_ffn, w_router_group, b_router_group,
                 w_router_expert, b_router_expert, w_exp_gate, w_exp_up, w_exp_down, g_ple, w_ple_gate,
                 w_ple_proj, g_final)
    outs = []
    for x, p in ((x_prompt, p_prompt), (x_sample, p_sample)):
        h1, route = _mix_trunk(x, P)
        y = _expert_stage(h1, p[0].reshape(-1, D_PLE), route, P)
        outs.append(y.reshape(x.shape))
    return tuple(outs)
```

```python
import functools

import jax
import jax.numpy as jnp
from jax import lax
from jax.experimental import pallas as pl
from jax.experimental.pallas import tpu as pltpu

f32 = jnp.float32
bf16 = jnp.bfloat16

D_MODEL = 1024
D_REC = 512
N_REC_HEADS = 8
REC_HEAD = 64
CONV_W = 4
CONV_LEFT = 2
LRU_C = 8.0
D_SGU = 512
N_SGU_HEADS = 8
SGU_HEAD = 64
CHUNK = 128
N_GROUPS = 4
N_EXP_PER_GROUP = 4
N_EXPERTS = 16
D_EXPERT = 512
D_PLE = 256
EPS = 1e-6

LANES = 128
SUBLANES = 8
NB = SUBLANES
ST = CHUNK
TT = NB * ST
HALO = SUBLANES
PITCH = ST + SUBLANES
N_LANE_GROUPS = D_REC // LANES
MXU_DIM = 256
TM = 512
N_PAIRS = 6
N_CLASSES = N_GROUPS * N_PAIRS
D_ROW = D_MODEL + D_PLE + LANES
ROUTE_ROWS = 32
VMEM_LIMIT = 60 * 1024 * 1024


def _rms(x, g):
    ms = jnp.mean(x * x, axis=-1, keepdims=True)
    return x * lax.rsqrt(ms + EPS) * g


def _softplus(x):
    return jnp.maximum(x, 0.0) + jnp.log1p(jnp.exp(-jnp.abs(x)))


def _gate_ab(xc, wg_ref, ba, bx, lam):
    xcb = xc.astype(bf16)
    sp = _softplus(-lam)
    out = []
    for g in range(N_LANE_GROUPS):
        sl = slice(LANES * g, LANES * (g + 1))
        pre = jnp.dot(xcb[:, sl], wg_ref[g], preferred_element_type=f32)
        r = jax.nn.sigmoid(pre[:, :LANES] + ba[:, sl])
        i = jax.nn.sigmoid(pre[:, LANES:] + bx[:, sl])
        log_a = -LRU_C * r * sp[:, sl]
        a = jnp.exp(log_a)
        mult = jnp.sqrt(jnp.tanh(-log_a) * (1.0 + a * a))
        out.append((a, mult * (i * xc[:, sl])))
    return out


def _store_ab(ab, a_ref, b_ref):
    for g, (a, b) in enumerate(ab):
        for s in range(NB):
            a_ref[g, s * PITCH:s * PITCH + ST, :] = a[s * ST:(s + 1) * ST, :]
            b_ref[g, s * PITCH:s * PITCH + ST, :] = b[s * ST:(s + 1) * ST, :]


def _scan(a_ref, b_ref, carry_ref, reverse):
    def step(k, hs):
        t = (ST - 1 - k) if reverse else k
        new = []
        for g in range(N_LANE_GROUPS):
            a = a_ref[g, pl.ds(t, NB, stride=PITCH), :]
            b = b_ref[g, pl.ds(t, NB, stride=PITCH), :]
            h = a * hs[g] + b
            b_ref[g, pl.ds(t, NB, stride=PITCH), :] = h
            new.append(h)
        return tuple(new)

    init = tuple(carry_ref[:, LANES * g:LANES * (g + 1)] for g in range(N_LANE_GROUPS))
    hs = lax.fori_loop(0, ST, step, init, unroll=8)
    for g in range(N_LANE_GROUPS):
        carry_ref[:, LANES * g:LANES * (g + 1)] = hs[g]


def _bwd_kernel(x_ref, xp_ref, xn_ref, gmix_ref, wxr_ref, cw_ref, cb_ref, wg_ref, ba_ref, bx_ref, lam_ref,
                xc_ref, hb_ref, ext_ref, a_ref, b_ref, carry_ref):
    j = pl.program_id(1)
    n_t = pl.num_programs(1)
    tb = n_t - 1 - j

    @pl.when(j == 0)
    def _():
        carry_ref[...] = jnp.zeros_like(carry_ref)

    gm = gmix_ref[...]

    def proj(xv):
        return jnp.dot(_rms(xv, gm).astype(bf16), wxr_ref[...], preferred_element_type=f32)

    xr = proj(x_ref[...].reshape(TT, D_MODEL))
    ext_ref[:, HALO:HALO + ST, :] = xr.reshape(NB, ST, D_REC)
    hp = proj(xp_ref[...].reshape(NB * HALO, D_MODEL)).reshape(NB, HALO, D_REC)
    ext_ref[:, 0:HALO, :] = jnp.where(tb > 0, hp, 0.0)
    hn = proj(xn_ref[...].reshape(NB * HALO, D_MODEL)).reshape(NB, HALO, D_REC)
    ext_ref[:, HALO + ST:2 * HALO + ST, :] = jnp.where(tb < n_t - 1, hn, 0.0)

    cw = cw_ref[...]
    xc = None
    for k in range(CONV_W):
        off = HALO - CONV_LEFT + k
        term = ext_ref[:, off:off + ST, :] * cw[k:k + 1, :]
        xc = term if xc is None else xc + term
    xc = xc + cb_ref[...]
    xc_ref[...] = xc

    ab = _gate_ab(xc.reshape(TT, D_REC), wg_ref, ba_ref[...], bx_ref[...], lam_ref[...])
    _store_ab(ab, a_ref, b_ref)
    _scan(a_ref, b_ref, carry_ref, reverse=True)
    for g in range(N_LANE_GROUPS):
        for s in range(NB):
            hb_ref[s, :, LANES * g:LANES * (g + 1)] = b_ref[g, s * PITCH:s * PITCH + ST, :]


def _route(logits):
    gl = [logits[i:i + 1, :] for i in range(N_GROUPS)]
    gm = jnp.maximum(jnp.maximum(gl[0], gl[1]), jnp.maximum(gl[2], gl[3]))
    sg = sum(jnp.exp(v - gm) for v in gl)
    g_top = 1.0 / sg
    gidx = jnp.where(gl[0] == gm, 0, jnp.where(gl[1] == gm, 1, jnp.where(gl[2] == gm, 2, 3)))
    el = []
    for k in range(N_EXP_PER_GROUP):
        rows = [logits[N_GROUPS + N_EXP_PER_GROUP * g + k:N_GROUPS + N_EXP_PER_GROUP * g + k + 1, :]
                for g in range(N_GROUPS)]
        el.append(jnp.where(gidx == 0, rows[0], jnp.where(gidx == 1, rows[1], jnp.where(gidx == 2, rows[2], rows[3]))))
    em = jnp.maximum(jnp.maximum(el[0], el[1]), jnp.maximum(el[2], el[3]))
    ee = [jnp.exp(v - em) for v in el]
    se = ee[0] + ee[1] + ee[2] + ee[3]
    p = [v / se for v in ee]
    p1 = jnp.maximum(jnp.maximum(p[0], p[1]), jnp.maximum(p[2], p[3]))
    i1 = jnp.where(p[0] == p1, 0, jnp.where(p[1] == p1, 1, jnp.where(p[2] == p1, 2, 3)))
    q = [jnp.where(i1 == k, -1.0, p[k]) for k in range(N_EXP_PER_GROUP)]
    p2 = jnp.maximum(jnp.maximum(q[0], q[1]), jnp.maximum(q[2], q[3]))
    i2 = jnp.where(q[0] == p2, 0, jnp.where(q[1] == p2, 1, jnp.where(q[2] == p2, 2, 3)))
    norm = p1 + p2
    w1 = g_top * (p1 / norm)
    w2 = g_top * (p2 / norm)
    first_is_lo = i1 < i2
    lo = jnp.where(first_is_lo, i1, i2)
    hi = jnp.where(first_is_lo, i2, i1)
    w_lo = jnp.where(first_is_lo, w1, w2)
    w_hi = jnp.where(first_is_lo, w2, w1)
    base = jnp.where(lo == 0, 0, jnp.where(lo == 1, 3, 5))
    cls = gidx * N_PAIRS + base + hi - lo - 1
    return cls.astype(f32), w_lo, w_hi


def _fwd_kernel(x_ref, xc_ref, hb_ref, p_ref, gmix_ref, wgr_ref, wuvt_ref, wg_ref, ba_ref, bx_ref, lam_ref,
                lng_ref, lnb_ref, bd_ref, bst_ref, grec_ref, gsgu_ref, wor_ref, wos_ref, gffn_ref, wrt_ref, brt_ref,
                hx_ref, a_ref, b_ref, carry_ref):
    j = pl.program_id(1)

    @pl.when(j == 0)
    def _():
        carry_ref[...] = jnp.zeros_like(carry_ref)

    x2 = x_ref[...].reshape(TT, D_MODEL)
    n = _rms(x2, gmix_ref[...]).astype(bf16)

    ab = _gate_ab(xc_ref[...].reshape(TT, D_REC), wg_ref, ba_ref[...], bx_ref[...], lam_ref[...])
    _store_ab(ab, a_ref, b_ref)
    _scan(a_ref, b_ref, carry_ref, reverse=False)
    hf = jnp.concatenate(
        [jnp.concatenate([b_ref[g, s * PITCH:s * PITCH + ST, :] for s in range(NB)], axis=0)
         for g in range(N_LANE_GROUPS)], axis=1)
    gr = jnp.dot(n, wgr_ref[...], preferred_element_type=f32)
    y_rec = (hf + hb_ref[...].reshape(TT, D_REC)) * jax.nn.gelu(gr)
    rec_n = _rms(y_rec, grec_ref[...]).astype(bf16)

    zt = lax.dot_general(wuvt_ref[...], n, (((1,), (1,)), ((), ())), preferred_element_type=f32)
    ut = jax.nn.gelu(zt[:D_SGU, :])
    vt = jax.nn.gelu(zt[D_SGU:, :])
    mu = jnp.mean(vt, axis=0, keepdims=True)
    vc = vt - mu
    var = jnp.mean(vc * vc, axis=0, keepdims=True)
    vn = (vc * lax.rsqrt(var + EPS) * lng_ref[...] + lnb_ref[...]).astype(bf16)
    n_slab = TT // MXU_DIM
    heads = []
    for h in range(N_SGU_HEADS):
        rows = vn[SGU_HEAD * h:SGU_HEAD * (h + 1), :]
        lhs = jnp.concatenate([rows[:, MXU_DIM * c:MXU_DIM * (c + 1)] for c in range(n_slab)], axis=0)
        res = jnp.dot(lhs, bd_ref[h], preferred_element_type=f32)
        heads.append(jnp.concatenate([res[SGU_HEAD * c:SGU_HEAD * (c + 1), :] for c in range(n_slab)], axis=1))
    mixed = jnp.concatenate(heads, axis=0) + jnp.concatenate([bst_ref[...]] * (TT // CHUNK), axis=1)
    yst = ut * mixed
    ms = jnp.mean(yst * yst, axis=0, keepdims=True)
    sgu_n = (yst * lax.rsqrt(ms + EPS) * gsgu_ref[...]).T.astype(bf16)

    delta = (jnp.dot(rec_n, wor_ref[...], preferred_element_type=f32)
             + jnp.dot(sgu_n, wos_ref[...], preferred_element_type=f32))
    h1 = x2 + delta
    hx_ref[:, :, 0:D_MODEL] = h1.reshape(NB, ST, D_MODEL)
    hx_ref[:, :, D_MODEL:D_MODEL + D_PLE] = p_ref[...]

    n2 = _rms(h1, gffn_ref[...]).astype(bf16)
    logits = lax.dot_general(wrt_ref[...], n2, (((1,), (1,)), ((), ())), preferred_element_type=f32) + brt_ref[...]
    cls, w_lo, w_hi = _route(logits)
    rt = jnp.concatenate([cls, w_lo, w_hi, jnp.zeros((LANES - 3, TT), f32)], axis=0).T
    hx_ref[:, :, D_MODEL + D_PLE:D_ROW] = rt.reshape(NB, ST, LANES)


def _expert_kernel(elo_ref, ehi_ref, nval_ref, src_ref, srcn_ref, hx_hbm, gffn_ref,
                   w1lo_ref, w3lo_ref, w2lo_ref, w1hi_ref, w3hi_ref, w2hi_ref,
                   gple_ref, wpg_ref, wpp_ref, gfin_ref, y_hbm, xbuf, ybuf, gsem, ssem):
    i = pl.program_id(0)
    n = pl.num_programs(0)
    slot = i % 2
    cnt = nval_ref[i]
    nxt = jnp.minimum(i + 1, n - 1)

    def gather_row(idx_ref, s, r):
        t = idx_ref[0, 0, r]
        return pltpu.make_async_copy(hx_hbm.at[pl.ds(t, 1)], xbuf.at[s, pl.ds(r, 1)], gsem.at[s])

    def issue_gather(idx_ref, s):
        def body(r, c):
            gather_row(idx_ref, s, r).start()
            return c
        lax.fori_loop(0, TM, body, 0, unroll=8)

    def scatter_row(s, r):
        t = src_ref[0, 0, r]
        return pltpu.make_async_copy(ybuf.at[s, pl.ds(r, 1)], y_hbm.at[pl.ds(t, 1)], ssem.at[s])

    def wait_scatter(s, rows):
        @pl.when(rows == TM)
        def _():
            pltpu.make_async_copy(ybuf.at[s], y_hbm.at[pl.ds(0, TM)], ssem.at[s]).wait()

        @pl.when(rows < TM)
        def _():
            def body(r, c):
                pltpu.make_async_copy(ybuf.at[s, pl.ds(0, 1)], y_hbm.at[pl.ds(0, 1)], ssem.at[s]).wait()
                return c
            lax.fori_loop(0, rows, body, 0)

    @pl.when(cnt > 0)
    def _():
        @pl.when(i == 0)
        def _():
            issue_gather(src_ref, 0)

        @pl.when((i + 1 < n) & (nval_ref[nxt] > 0))
        def _():
            issue_gather(srcn_ref, 1 - slot)

        pltpu.make_async_copy(hx_hbm.at[pl.ds(0, TM)], xbuf.at[slot], gsem.at[slot]).wait()

        @pl.when(i >= 2)
        def _():
            wait_scatter(slot, nval_ref[jnp.maximum(i - 2, 0)])

        xs = xbuf[slot, :, 0:D_MODEL]
        n2 = _rms(xs, gffn_ref[...]).astype(bf16)

        def expert(w1_ref, w3_ref, w2_ref):
            a = jnp.dot(n2, w1_ref[0], preferred_element_type=f32)
            b = jnp.dot(n2, w3_ref[0], preferred_element_type=f32)
            hdn = (jax.nn.silu(a) * b).astype(bf16)
            return jnp.dot(hdn, w2_ref[0], preferred_element_type=f32)

        ws = xbuf[slot, :, D_MODEL + D_PLE:D_ROW]
        out = ws[:, 1:2] * expert(w1lo_ref, w3lo_ref, w2lo_ref)
        out = out + ws[:, 2:3] * expert(w1hi_ref, w3hi_ref, w2hi_ref)
        h2 = xs + out
        n3 = _rms(h2, gple_ref[...]).astype(bf16)
        gate = jax.nn.sigmoid(jnp.dot(n3, wpg_ref[...], preferred_element_type=f32))
        ps = xbuf[slot, :, D_MODEL:D_MODEL + D_PLE]
        proj = jnp.dot(ps.astype(bf16), wpp_ref[...], preferred_element_type=f32)
        h3 = h2 + gate * proj
        ybuf[slot] = _rms(h3, gfin_ref[...])

        @pl.when(cnt == TM)
        def _():
            def body(r, c):
                scatter_row(slot, r).start()
                return c
            lax.fori_loop(0, TM, body, 0, unroll=8)

        @pl.when(cnt < TM)
        def _():
            def body(r, c):
                scatter_row(slot, r).start()
                return c
            lax.fori_loop(0, cnt, body, 0)

        @pl.when((i == n - 1) | (nval_ref[nxt] == 0))
        def _():
            @pl.when(i >= 1)
            def _():
                wait_scatter(1 - slot, nval_ref[jnp.maximum(i - 1, 0)])
            wait_scatter(slot, cnt)


def _const_spec(shape):
    nd = len(shape)
    return pl.BlockSpec(shape, lambda *_: (0,) * nd)


def _gate_blocks(w_a, w_x):
    def pair(w):
        z = jnp.zeros((REC_HEAD, REC_HEAD), w.dtype)
        return jnp.stack([jnp.block([[w[2 * g], z], [z, w[2 * g + 1]]]) for g in range(N_LANE_GROUPS)])
    return jnp.concatenate([pair(w_a), pair(w_x)], axis=-1).astype(bf16)


def _mix_trunk(x, p, P):
    bsz, seq, _ = x.shape
    assert bsz % NB == 0 and seq % ST == 0
    n_g, n_t = bsz // NB, seq // ST
    halo_blocks = seq // HALO
    row = lambda v: v.reshape(1, -1)

    x_spec_b = pl.BlockSpec((NB, ST, D_MODEL), lambda g, j: (g, n_t - 1 - j, 0))
    xp_spec = pl.BlockSpec((NB, HALO, D_MODEL),
                           lambda g, j: (g, jnp.maximum((n_t - 1 - j) * (ST // HALO) - 1, 0), 0))
    xn_spec = pl.BlockSpec((NB, HALO, D_MODEL),
                           lambda g, j: (g, jnp.minimum((n_t - j) * (ST // HALO), halo_blocks - 1), 0))
    rec_spec_b = pl.BlockSpec((NB, ST, D_REC), lambda g, j: (g, n_t - 1 - j, 0))
    scan_scratch = [pltpu.VMEM((N_LANE_GROUPS, NB * PITCH, LANES), f32),
                    pltpu.VMEM((N_LANE_GROUPS, NB * PITCH, LANES), f32),
                    pltpu.VMEM((NB, D_REC), f32)]

    bwd_w = [row(P['g_mix']), P['w_xr'], P['conv_w'], row(P['conv_b']), P['wg_b'],
             row(P['ba_b']), row(P['bx_b']), row(P['lam_b'])]
    xc, hb = pl.pallas_call(
        _bwd_kernel,
        grid=(n_g, n_t),
        in_specs=[x_spec_b, xp_spec, xn_spec] + [_const_spec(w.shape) for w in bwd_w],
        out_specs=[rec_spec_b, rec_spec_b],
        out_shape=[jax.ShapeDtypeStruct((bsz, seq, D_REC), f32)] * 2,
        scratch_shapes=[pltpu.VMEM((NB, ST + 2 * HALO, D_REC), f32)] + scan_scratch,
        compiler_params=pltpu.CompilerParams(dimension_semantics=("arbitrary", "arbitrary"),
                                             vmem_limit_bytes=VMEM_LIMIT),
        name="bwd_scan",
    )(x, x, x, *bwd_w)

    tile_spec = lambda d: pl.BlockSpec((NB, ST, d), lambda g, j: (g, j, 0))
    fwd_w = [row(P['g_mix']), P['w_gr'], P['w_uvt'], P['wg_f'], row(P['ba_f']), row(P['bx_f']), row(P['lam_f']),
             P['ln_g'], P['ln_b'], P['bd'], P['bst'], row(P['g_rec']), P['g_sgu'], P['w_or'], P['w_os'],
             row(P['g_ffn']), P['w_rt'], P['b_rt']]
    hx = pl.pallas_call(
        _fwd_kernel,
        grid=(n_g, n_t),
        in_specs=[tile_spec(D_MODEL), tile_spec(D_REC), tile_spec(D_REC), tile_spec(D_PLE)]
                 + [_const_spec(w.shape) for w in fwd_w],
        out_specs=tile_spec(D_ROW),
        out_shape=jax.ShapeDtypeStruct((bsz, seq, D_ROW), f32),
        scratch_shapes=scan_scratch,
        compiler_params=pltpu.CompilerParams(dimension_semantics=("arbitrary", "arbitrary"),
                                             vmem_limit_bytes=VMEM_LIMIT),
        name="fwd_mix",
    )(x, xc, hb, p, *fwd_w)
    return hx.reshape(bsz * seq, D_ROW)


def _expert_stage(hx, P):
    n_tok = hx.shape[0]
    assert n_tok % TM == 0
    n_tiles = n_tok // TM + N_CLASSES
    n_slots = n_tiles * TM
    cls = hx[:, D_MODEL + D_PLE].astype(jnp.int32)

    onehot = (cls[:, None] == jnp.arange(N_CLASSES, dtype=jnp.int32)[None, :]).astype(jnp.int32)
    counts = jnp.sum(onehot, axis=0)
    tiles_c = (counts + TM - 1) // TM
    tile_end = jnp.cumsum(tiles_c)
    tile_start = tile_end - tiles_c
    order = jnp.argsort(cls, stable=True).astype(jnp.int32)
    dense_start = jnp.cumsum(counts) - counts
    tile_ids = jnp.arange(n_tiles, dtype=jnp.int32)
    tile_cls = jnp.minimum(jnp.searchsorted(tile_end, tile_ids, side='right'), N_CLASSES - 1).astype(jnp.int32)
    active = tile_ids < tile_end[-1]
    n_valid = jnp.where(active, jnp.clip(counts[tile_cls] - (tile_ids - tile_start[tile_cls]) * TM, 0, TM), 0)
    within = jnp.arange(n_slots, dtype=jnp.int32).reshape(n_tiles, TM) - (tile_start[tile_cls] * TM)[:, None]
    dense_idx = jnp.clip(dense_start[tile_cls][:, None] + within, 0, n_tok - 1)
    in_tile = jnp.arange(TM, dtype=jnp.int32)[None, :] < n_valid[:, None]
    src = jnp.where(in_tile, order[dense_idx], 0).reshape(n_tiles, 1, TM)

    pair_lo = jnp.array([0, 0, 0, 1, 1, 2], jnp.int32)
    pair_hi = jnp.array([1, 2, 3, 2, 3, 3], jnp.int32)
    e_lo = (tile_cls // N_PAIRS) * N_EXP_PER_GROUP + pair_lo[tile_cls % N_PAIRS]
    e_hi = (tile_cls // N_PAIRS) * N_EXP_PER_GROUP + pair_hi[tile_cls % N_PAIRS]

    row = lambda v: v.reshape(1, -1)
    idx_spec = lambda f: pl.BlockSpec((1, 1, TM), f, memory_space=pltpu.SMEM)
    any_spec = pl.BlockSpec(memory_space=pl.ANY)
    lo_spec = lambda s: pl.BlockSpec((1,) + s, lambda i, elo, ehi, nv: (elo[i], 0, 0))
    hi_spec = lambda s: pl.BlockSpec((1,) + s, lambda i, elo, ehi, nv: (ehi[i], 0, 0))
    up_shape, down_shape = (D_MODEL, D_EXPERT), (D_EXPERT, D_MODEL)
    consts = [row(P['g_ple']), P['w_pg'], P['w_pp'], row(P['g_final'])]
    return pl.pallas_call(
        _expert_kernel,
        grid_spec=pltpu.PrefetchScalarGridSpec(
            num_scalar_prefetch=3,
            grid=(n_tiles,),
            in_specs=[idx_spec(lambda i, *_: (i, 0, 0)),
                      idx_spec(lambda i, *_: (jnp.minimum(i + 1, n_tiles - 1), 0, 0)),
                      any_spec, _const_spec((1, D_MODEL)),
                      lo_spec(up_shape), lo_spec(up_shape), lo_spec(down_shape),
                      hi_spec(up_shape), hi_spec(up_shape), hi_spec(down_shape)]
                     + [_const_spec(w.shape) for w in consts],
            out_specs=any_spec,
            scratch_shapes=[pltpu.VMEM((2, TM, D_ROW), f32), pltpu.VMEM((2, TM, D_MODEL), f32),
                            pltpu.SemaphoreType.DMA((2,)), pltpu.SemaphoreType.DMA((2,))],
        ),
        out_shape=jax.ShapeDtypeStruct((n_tok, D_MODEL), f32),
        compiler_params=pltpu.CompilerParams(dimension_semantics=("arbitrary",), vmem_limit_bytes=VMEM_LIMIT),
        name="experts",
    )(e_lo, e_hi, n_valid.astype(jnp.int32), src, src, hx, row(P['g_ffn']),
      P['w1'], P['w3'], P['w2'], P['w1'], P['w3'], P['w2'], *consts)


def _prepare(g_mix, w_in, conv_w, conv_b, lru_w_a, lru_b_a, lru_w_x, lru_b_x, lru_lambda, sgu_ln_g, sgu_ln_b,
             sgu_w_s, sgu_b_s, g_rec_out, g_sgu_out, w_out, g_ffn, w_router_group, b_router_group,
             w_router_expert, b_router_expert, w_exp_gate, w_exp_up, w_exp_down, g_ple, w_ple_gate, w_ple_proj,
             g_final):
    l = 0
    w = w_in[l]
    ws_t = jnp.transpose(sgu_w_s[l], (0, 2, 1))
    z = jnp.zeros((CHUNK, CHUNK), f32)
    bd = jnp.stack([jnp.block([[ws_t[h], z], [z, ws_t[h]]]) for h in range(N_SGU_HEADS)]).astype(bf16)
    w_rt = jnp.concatenate([w_router_group[l].T, w_router_expert[l].T,
                            jnp.zeros((ROUTE_ROWS - N_GROUPS - N_EXPERTS, D_MODEL), f32)], axis=0).astype(bf16)
    b_rt = jnp.concatenate([b_router_group[l], b_router_expert[l],
                            jnp.zeros((ROUTE_ROWS - N_GROUPS - N_EXPERTS,), f32)]).reshape(ROUTE_ROWS, 1)
    return dict(
        g_mix=g_mix[l], w_xr=w[:, :D_REC].astype(bf16), w_gr=w[:, D_REC:2 * D_REC].astype(bf16),
        w_uvt=w[:, 2 * D_REC:].T.astype(bf16),
        conv_w=conv_w[l], conv_b=conv_b[l],
        wg_f=_gate_blocks(lru_w_a[l, 0], lru_w_x[l, 0]), wg_b=_gate_blocks(lru_w_a[l, 1], lru_w_x[l, 1]),
        ba_f=lru_b_a[l, 0], ba_b=lru_b_a[l, 1], bx_f=lru_b_x[l, 0], bx_b=lru_b_x[l, 1],
        lam_f=lru_lambda[l, 0], lam_b=lru_lambda[l, 1],
        ln_g=sgu_ln_g[l].reshape(D_SGU, 1), ln_b=sgu_ln_b[l].reshape(D_SGU, 1),
        bd=bd, bst=jnp.repeat(sgu_b_s[l], SGU_HEAD, axis=0),
        g_rec=g_rec_out[l], g_sgu=g_sgu_out[l].reshape(D_SGU, 1),
        w_or=w_out[l, :D_REC].astype(bf16), w_os=w_out[l, D_REC:].astype(bf16),
        g_ffn=g_ffn[l], w_rt=w_rt, b_rt=b_rt,
        w1=w_exp_gate[l].astype(bf16), w3=w_exp_up[l].astype(bf16), w2=w_exp_down[l].astype(bf16),
        g_ple=g_ple[l], w_pg=w_ple_gate[l].astype(bf16), w_pp=w_ple_proj[l].astype(bf16), g_final=g_final,
    )


def kernel(x_prompt, x_sample, p_prompt, p_sample, g_mix, w_in, conv_w, conv_b, lru_w_a, lru_b_a, lru_w_x, lru_b_x, lru_lambda, sgu_ln_g, sgu_ln_b, sgu_w_s, sgu_b_s, g_rec_out, g_sgu_out, w_out, g_ffn, w_router_group, b_router_group, w_router_expert, b_router_expert, w_exp_gate, w_exp_up, w_exp_down, g_ple, w_ple_gate, w_ple_proj, g_final):
    assert w_in.shape[0] == 1, "single-layer trunk"
    P = _prepare(g_mix, w_in, conv_w, conv_b, lru_w_a, lru_b_a, lru_w_x, lru_b_x, lru_lambda, sgu_ln_g, sgu_ln_b,
                 sgu_w_s, sgu_b_s, g_rec_out, g_sgu_out, w_out, g_ffn, w_router_group, b_router_group,
                 w_router_expert, b_router_expert, w_exp_gate, w_exp_up, w_exp_down, g_ple, w_ple_gate,
                 w_ple_proj, g_final)
    outs = []
    for x, p in ((x_prompt, p_prompt), (x_sample, p_sample)):
        y = _expert_stage(_mix_trunk(x, p[0], P), P)
        outs.append(y.reshape(x.shape))
    return tuple(outs)
```

```python
import functools

import jax
import jax.numpy as jnp
from jax import lax
from jax.experimental import pallas as pl
from jax.experimental.pallas import tpu as pltpu

f32 = jnp.float32
bf16 = jnp.bfloat16

D_MODEL = 1024
D_REC = 512
N_REC_HEADS = 8
REC_HEAD = 64
CONV_W = 4
CONV_LEFT = 2
LRU_C = 8.0
D_SGU = 512
N_SGU_HEADS = 8
SGU_HEAD = 64
CHUNK = 128
N_GROUPS = 4
N_EXP_PER_GROUP = 4
N_EXPERTS = 16
D_EXPERT = 512
D_PLE = 256
EPS = 1e-6

LANES = 128
SUBLANES = 8
NB = SUBLANES
ST = CHUNK
TT = NB * ST
HALO = SUBLANES
PITCH = ST + SUBLANES
N_LANE_GROUPS = D_REC // LANES
MXU_DIM = 256
TM = 512
N_PAIRS = 6
N_CLASSES = N_GROUPS * N_PAIRS
D_ROW = D_MODEL + D_PLE + LANES
ROUTE_ROWS = 32
VMEM_LIMIT = 60 * 1024 * 1024


def _rms(x, g):
    ms = jnp.mean(x * x, axis=-1, keepdims=True)
    return x * lax.rsqrt(ms + EPS) * g


def _softplus(x):
    return jnp.maximum(x, 0.0) + jnp.log1p(jnp.exp(-jnp.abs(x)))


def _gate_ab(xc, wg_ref, ba, bx, lam):
    xcb = xc.astype(bf16)
    sp = _softplus(-lam)
    out = []
    for g in range(N_LANE_GROUPS):
        sl = slice(LANES * g, LANES * (g + 1))
        pre = jnp.dot(xcb[:, sl], wg_ref[g], preferred_element_type=f32)
        r = jax.nn.sigmoid(pre[:, :LANES] + ba[:, sl])
        i = jax.nn.sigmoid(pre[:, LANES:] + bx[:, sl])
        log_a = -LRU_C * r * sp[:, sl]
        a = jnp.exp(log_a)
        mult = jnp.sqrt(jnp.tanh(-log_a) * (1.0 + a * a))
        out.append((a, mult * (i * xc[:, sl])))
    return out


def _store_ab(ab, a_ref, b_ref):
    for g, (a, b) in enumerate(ab):
        for s in range(NB):
            a_ref[g, s * PITCH:s * PITCH + ST, :] = a[s * ST:(s + 1) * ST, :]
            b_ref[g, s * PITCH:s * PITCH + ST, :] = b[s * ST:(s + 1) * ST, :]


def _scan(a_ref, b_ref, carry_ref, reverse):
    def step(k, hs):
        t = (ST - 1 - k) if reverse else k
        new = []
        for g in range(N_LANE_GROUPS):
            a = a_ref[g, pl.ds(t, NB, stride=PITCH), :]
            b = b_ref[g, pl.ds(t, NB, stride=PITCH), :]
            h = a * hs[g] + b
            b_ref[g, pl.ds(t, NB, stride=PITCH), :] = h
            new.append(h)
        return tuple(new)

    init = tuple(carry_ref[:, LANES * g:LANES * (g + 1)] for g in range(N_LANE_GROUPS))
    hs = lax.fori_loop(0, ST, step, init, unroll=8)
    for g in range(N_LANE_GROUPS):
        carry_ref[:, LANES * g:LANES * (g + 1)] = hs[g]


def _bwd_kernel(x_ref, xp_ref, xn_ref, gmix_ref, wxr_ref, cw_ref, cb_ref, wg_ref, ba_ref, bx_ref, lam_ref,
                xc_ref, hb_ref, ext_ref, a_ref, b_ref, carry_ref):
    j = pl.program_id(1)
    n_t = pl.num_programs(1)
    tb = n_t - 1 - j

    @pl.when(j == 0)
    def _():
        carry_ref[...] = jnp.zeros_like(carry_ref)

    gm = gmix_ref[...]

    def proj(xv):
        return jnp.dot(_rms(xv, gm).astype(bf16), wxr_ref[...], preferred_element_type=f32)

    xr = proj(x_ref[...].reshape(TT, D_MODEL))
    ext_ref[:, HALO:HALO + ST, :] = xr.reshape(NB, ST, D_REC)
    hp = proj(xp_ref[...].reshape(NB * HALO, D_MODEL)).reshape(NB, HALO, D_REC)
    ext_ref[:, 0:HALO, :] = jnp.where(tb > 0, hp, 0.0)
    hn = proj(xn_ref[...].reshape(NB * HALO, D_MODEL)).reshape(NB, HALO, D_REC)
    ext_ref[:, HALO + ST:2 * HALO + ST, :] = jnp.where(tb < n_t - 1, hn, 0.0)

    cw = cw_ref[...]
    xc = None
    for k in range(CONV_W):
        off = HALO - CONV_LEFT + k
        term = ext_ref[:, off:off + ST, :] * cw[k:k + 1, :]
        xc = term if xc is None else xc + term
    xc = xc + cb_ref[...]
    xc_ref[...] = xc

    ab = _gate_ab(xc.reshape(TT, D_REC), wg_ref, ba_ref[...], bx_ref[...], lam_ref[...])
    _store_ab(ab, a_ref, b_ref)
    _scan(a_ref, b_ref, carry_ref, reverse=True)
    for g in range(N_LANE_GROUPS):
        for s in range(NB):
            hb_ref[s, :, LANES * g:LANES * (g + 1)] = b_ref[g, s * PITCH:s * PITCH + ST, :]


def _route(logits):
    gl = [logits[i:i + 1, :] for i in range(N_GROUPS)]
    gm = jnp.maximum(jnp.maximum(gl[0], gl[1]), jnp.maximum(gl[2], gl[3]))
    sg = sum(jnp.exp(v - gm) for v in gl)
    g_top = 1.0 / sg
    gidx = jnp.where(gl[0] == gm, 0, jnp.where(gl[1] == gm, 1, jnp.where(gl[2] == gm, 2, 3)))
    el = []
    for k in range(N_EXP_PER_GROUP):
        rows = [logits[N_GROUPS + N_EXP_PER_GROUP * g + k:N_GROUPS + N_EXP_PER_GROUP * g + k + 1, :]
                for g in range(N_GROUPS)]
        el.append(jnp.where(gidx == 0, rows[0], jnp.where(gidx == 1, rows[1], jnp.where(gidx == 2, rows[2], rows[3]))))
    em = jnp.maximum(jnp.maximum(el[0], el[1]), jnp.maximum(el[2], el[3]))
    ee = [jnp.exp(v - em) for v in el]
    se = ee[0] + ee[1] + ee[2] + ee[3]
    p = [v / se for v in ee]
    p1 = jnp.maximum(jnp.maximum(p[0], p[1]), jnp.maximum(p[2], p[3]))
    i1 = jnp.where(p[0] == p1, 0, jnp.where(p[1] == p1, 1, jnp.where(p[2] == p1, 2, 3)))
    q = [jnp.where(i1 == k, -1.0, p[k]) for k in range(N_EXP_PER_GROUP)]
    p2 = jnp.maximum(jnp.maximum(q[0], q[1]), jnp.maximum(q[2], q[3]))
    i2 = jnp.where(q[0] == p2, 0, jnp.where(q[1] == p2, 1, jnp.where(q[2] == p2, 2, 3)))
    norm = p1 + p2
    w1 = g_top * (p1 / norm)
    w2 = g_top * (p2 / norm)
    first_is_lo = i1 < i2
    lo = jnp.where(first_is_lo, i1, i2)
    hi = jnp.where(first_is_lo, i2, i1)
    w_lo = jnp.where(first_is_lo, w1, w2)
    w_hi = jnp.where(first_is_lo, w2, w1)
    base = jnp.where(lo == 0, 0, jnp.where(lo == 1, 3, 5))
    cls = gidx * N_PAIRS + base + hi - lo - 1
    return cls.astype(f32), w_lo, w_hi


def _fwd_kernel(x_ref, xc_ref, hb_ref, p_ref, gmix_ref, wgr_ref, wuvt_ref, wg_ref, ba_ref, bx_ref, lam_ref,
                lng_ref, lnb_ref, bd_ref, bst_ref, grec_ref, gsgu_ref, wor_ref, wos_ref, gffn_ref, wrt_ref, brt_ref,
                hx_ref, a_ref, b_ref, carry_ref):
    j = pl.program_id(1)

    @pl.when(j == 0)
    def _():
        carry_ref[...] = jnp.zeros_like(carry_ref)

    x2 = x_ref[...].reshape(TT, D_MODEL)
    n = _rms(x2, gmix_ref[...]).astype(bf16)

    ab = _gate_ab(xc_ref[...].reshape(TT, D_REC), wg_ref, ba_ref[...], bx_ref[...], lam_ref[...])
    _store_ab(ab, a_ref, b_ref)
    _scan(a_ref, b_ref, carry_ref, reverse=False)
    hf = jnp.concatenate(
        [jnp.concatenate([b_ref[g, s * PITCH:s * PITCH + ST, :] for s in range(NB)], axis=0)
         for g in range(N_LANE_GROUPS)], axis=1)
    gr = jnp.dot(n, wgr_ref[...], preferred_element_type=f32)
    y_rec = (hf + hb_ref[...].reshape(TT, D_REC)) * jax.nn.gelu(gr)
    rec_n = _rms(y_rec, grec_ref[...]).astype(bf16)

    zt = lax.dot_general(wuvt_ref[...], n, (((1,), (1,)), ((), ())), preferred_element_type=f32)
    ut = jax.nn.gelu(zt[:D_SGU, :])
    vt = jax.nn.gelu(zt[D_SGU:, :])
    mu = jnp.mean(vt, axis=0, keepdims=True)
    vc = vt - mu
    var = jnp.mean(vc * vc, axis=0, keepdims=True)
    vn = (vc * lax.rsqrt(var + EPS) * lng_ref[...] + lnb_ref[...]).astype(bf16)
    n_slab = TT // MXU_DIM
    heads = []
    for h in range(N_SGU_HEADS):
        rows = vn[SGU_HEAD * h:SGU_HEAD * (h + 1), :]
        lhs = jnp.concatenate([rows[:, MXU_DIM * c:MXU_DIM * (c + 1)] for c in range(n_slab)], axis=0)
        res = jnp.dot(lhs, bd_ref[h], preferred_element_type=f32)
        heads.append(jnp.concatenate([res[SGU_HEAD * c:SGU_HEAD * (c + 1), :] for c in range(n_slab)], axis=1))
    mixed = jnp.concatenate(heads, axis=0) + jnp.concatenate([bst_ref[...]] * (TT // CHUNK), axis=1)
    yst = ut * mixed
    ms = jnp.mean(yst * yst, axis=0, keepdims=True)
    sgu_n = (yst * lax.rsqrt(ms + EPS) * gsgu_ref[...]).T.astype(bf16)

    delta = (jnp.dot(rec_n, wor_ref[...], preferred_element_type=f32)
             + jnp.dot(sgu_n, wos_ref[...], preferred_element_type=f32))
    h1 = x2 + delta
    hx_ref[:, :, 0:D_MODEL] = h1.reshape(NB, ST, D_MODEL)
    hx_ref[:, :, D_MODEL:D_MODEL + D_PLE] = p_ref[...]

    n2 = _rms(h1, gffn_ref[...]).astype(bf16)
    logits = lax.dot_general(wrt_ref[...], n2, (((1,), (1,)), ((), ())), preferred_element_type=f32) + brt_ref[...]
    cls, w_lo, w_hi = _route(logits)
    rt = jnp.concatenate([cls, w_lo, w_hi, jnp.zeros((LANES - 3, TT), f32)], axis=0).T
    hx_ref[:, :, D_MODEL + D_PLE:D_ROW] = rt.reshape(NB, ST, LANES)


def _expert_kernel(elo_ref, ehi_ref, nval_ref, sq_ref, ss_ref, sqn_ref, ssn_ref, hx_hbm, gffn_ref,
                   w1lo_ref, w3lo_ref, w2lo_ref, w1hi_ref, w3hi_ref, w2hi_ref,
                   gple_ref, wpg_ref, wpp_ref, gfin_ref, y_hbm, xbuf, ybuf, gsem, ssem):
    i = pl.program_id(0)
    n = pl.num_programs(0)
    slot = i % 2
    cnt = nval_ref[i]
    nxt = jnp.minimum(i + 1, n - 1)
    n_grp = TM // SUBLANES

    def gather_row(q_ref, s_ref, s, k, u):
        r = k * SUBLANES + u
        return pltpu.make_async_copy(hx_hbm.at[q_ref[0, 0, r], pl.ds(s_ref[0, 0, r], 1)],
                                     xbuf.at[s, k, pl.ds(u, 1)], gsem.at[s])

    def issue_gather(q_ref, s_ref, s):
        def body(k, c):
            for u in range(SUBLANES):
                gather_row(q_ref, s_ref, s, k, u).start(priority=u % 2)
            return c
        lax.fori_loop(0, n_grp, body, 0)

    def scatter_row(s, k, u):
        r = k * SUBLANES + u
        return pltpu.make_async_copy(ybuf.at[s, k, pl.ds(u, 1)],
                                     y_hbm.at[sq_ref[0, 0, r], pl.ds(ss_ref[0, 0, r], 1)], ssem.at[s])

    def wait_scatter(s, rows):
        @pl.when(rows == TM)
        def _():
            pltpu.make_async_copy(ybuf.at[s], y_hbm.at[pl.ds(0, n_grp)], ssem.at[s]).wait()

        @pl.when(rows < TM)
        def _():
            def body(r, c):
                pltpu.make_async_copy(ybuf.at[s, 0, pl.ds(0, 1)], y_hbm.at[0, pl.ds(0, 1)], ssem.at[s]).wait()
                return c
            lax.fori_loop(0, rows, body, 0)

    @pl.when(cnt > 0)
    def _():
        @pl.when(i == 0)
        def _():
            issue_gather(sq_ref, ss_ref, 0)

        @pl.when((i + 1 < n) & (nval_ref[nxt] > 0))
        def _():
            issue_gather(sqn_ref, ssn_ref, 1 - slot)

        pltpu.make_async_copy(hx_hbm.at[pl.ds(0, n_grp)], xbuf.at[slot], gsem.at[slot]).wait()

        @pl.when(i >= 2)
        def _():
            wait_scatter(slot, nval_ref[jnp.maximum(i - 2, 0)])

        xrow = xbuf[slot].reshape(TM, D_ROW)
        xs = xrow[:, 0:D_MODEL]
        n2 = _rms(xs, gffn_ref[...]).astype(bf16)

        def expert(w1_ref, w3_ref, w2_ref):
            a = jnp.dot(n2, w1_ref[0], preferred_element_type=f32)
            b = jnp.dot(n2, w3_ref[0], preferred_element_type=f32)
            hdn = (jax.nn.silu(a) * b).astype(bf16)
            return jnp.dot(hdn, w2_ref[0], preferred_element_type=f32)

        ws = xrow[:, D_MODEL + D_PLE:D_ROW]
        out = ws[:, 1:2] * expert(w1lo_ref, w3lo_ref, w2lo_ref)
        out = out + ws[:, 2:3] * expert(w1hi_ref, w3hi_ref, w2hi_ref)
        h2 = xs + out
        n3 = _rms(h2, gple_ref[...]).astype(bf16)
        gate = jax.nn.sigmoid(jnp.dot(n3, wpg_ref[...], preferred_element_type=f32))
        ps = xrow[:, D_MODEL:D_MODEL + D_PLE]
        proj = jnp.dot(ps.astype(bf16), wpp_ref[...], preferred_element_type=f32)
        h3 = h2 + gate * proj
        ybuf[slot] = _rms(h3, gfin_ref[...]).reshape(n_grp, SUBLANES, D_MODEL)

        @pl.when(cnt == TM)
        def _():
            def body(k, c):
                for u in range(SUBLANES):
                    scatter_row(slot, k, u).start(priority=u % 2)
                return c
            lax.fori_loop(0, n_grp, body, 0)

        @pl.when(cnt < TM)
        def _():
            def body(r, c):
                scatter_row(slot, r // SUBLANES, r % SUBLANES).start()
                return c
            lax.fori_loop(0, cnt, body, 0)

        @pl.when((i == n - 1) | (nval_ref[nxt] == 0))
        def _():
            @pl.when(i >= 1)
            def _():
                wait_scatter(1 - slot, nval_ref[jnp.maximum(i - 1, 0)])
            wait_scatter(slot, cnt)


def _const_spec(shape):
    nd = len(shape)
    return pl.BlockSpec(shape, lambda *_: (0,) * nd)


def _gate_blocks(w_a, w_x):
    def pair(w):
        z = jnp.zeros((REC_HEAD, REC_HEAD), w.dtype)
        return jnp.stack([jnp.block([[w[2 * g], z], [z, w[2 * g + 1]]]) for g in range(N_LANE_GROUPS)])
    return jnp.concatenate([pair(w_a), pair(w_x)], axis=-1).astype(bf16)


def _mix_trunk(x, p, P):
    bsz, seq, _ = x.shape
    assert bsz % NB == 0 and seq % ST == 0
    n_g, n_t = bsz // NB, seq // ST
    halo_blocks = seq // HALO
    row = lambda v: v.reshape(1, -1)

    x_spec_b = pl.BlockSpec((NB, ST, D_MODEL), lambda g, j: (g, n_t - 1 - j, 0))
    xp_spec = pl.BlockSpec((NB, HALO, D_MODEL),
                           lambda g, j: (g, jnp.maximum((n_t - 1 - j) * (ST // HALO) - 1, 0), 0))
    xn_spec = pl.BlockSpec((NB, HALO, D_MODEL),
                           lambda g, j: (g, jnp.minimum((n_t - j) * (ST // HALO), halo_blocks - 1), 0))
    rec_spec_b = pl.BlockSpec((NB, ST, D_REC), lambda g, j: (g, n_t - 1 - j, 0))
    scan_scratch = [pltpu.VMEM((N_LANE_GROUPS, NB * PITCH, LANES), f32),
                    pltpu.VMEM((N_LANE_GROUPS, NB * PITCH, LANES), f32),
                    pltpu.VMEM((NB, D_REC), f32)]

    bwd_w = [row(P['g_mix']), P['w_xr'], P['conv_w'], row(P['conv_b']), P['wg_b'],
             row(P['ba_b']), row(P['bx_b']), row(P['lam_b'])]
    xc, hb = pl.pallas_call(
        _bwd_kernel,
        grid=(n_g, n_t),
        in_specs=[x_spec_b, xp_spec, xn_spec] + [_const_spec(w.shape) for w in bwd_w],
        out_specs=[rec_spec_b, rec_spec_b],
        out_shape=[jax.ShapeDtypeStruct((bsz, seq, D_REC), f32)] * 2,
        scratch_shapes=[pltpu.VMEM((NB, ST + 2 * HALO, D_REC), f32)] + scan_scratch,
        compiler_params=pltpu.CompilerParams(dimension_semantics=("arbitrary", "arbitrary"),
                                             vmem_limit_bytes=VMEM_LIMIT),
        name="bwd_scan",
    )(x, x, x, *bwd_w)

    tile_spec = lambda d: pl.BlockSpec((NB, ST, d), lambda g, j: (g, j, 0))
    fwd_w = [row(P['g_mix']), P['w_gr'], P['w_uvt'], P['wg_f'], row(P['ba_f']), row(P['bx_f']), row(P['lam_f']),
             P['ln_g'], P['ln_b'], P['bd'], P['bst'], row(P['g_rec']), P['g_sgu'], P['w_or'], P['w_os'],
             row(P['g_ffn']), P['w_rt'], P['b_rt']]
    hx = pl.pallas_call(
        _fwd_kernel,
        grid=(n_g, n_t),
        in_specs=[tile_spec(D_MODEL), tile_spec(D_REC), tile_spec(D_REC), tile_spec(D_PLE)]
                 + [_const_spec(w.shape) for w in fwd_w],
        out_specs=tile_spec(D_ROW),
        out_shape=jax.ShapeDtypeStruct((bsz, seq, D_ROW), f32),
        scratch_shapes=scan_scratch,
        compiler_params=pltpu.CompilerParams(dimension_semantics=("arbitrary", "arbitrary"),
                                             vmem_limit_bytes=VMEM_LIMIT),
        name="fwd_mix",
    )(x, xc, hb, p, *fwd_w)
    return hx.reshape(bsz * seq, D_ROW)


def _expert_stage(hx, P):
    n_tok = hx.shape[0]
    assert n_tok % TM == 0
    n_tiles = n_tok // TM + N_CLASSES
    n_slots = n_tiles * TM
    cls = hx[:, D_MODEL + D_PLE].astype(jnp.int32)

    onehot = (cls[:, None] == jnp.arange(N_CLASSES, dtype=jnp.int32)[None, :]).astype(jnp.int32)
    counts = jnp.sum(onehot, axis=0)
    tiles_c = (counts + TM - 1) // TM
    tile_end = jnp.cumsum(tiles_c)
    tile_start = tile_end - tiles_c
    order = jnp.argsort(cls, stable=True).astype(jnp.int32)
    dense_start = jnp.cumsum(counts) - counts
    tile_ids = jnp.arange(n_tiles, dtype=jnp.int32)
    tile_cls = jnp.minimum(jnp.sum((tile_ids[:, None] >= tile_end[None, :]).astype(jnp.int32), axis=1),
                           N_CLASSES - 1)
    active = tile_ids < tile_end[-1]
    n_valid = jnp.where(active, jnp.clip(counts[tile_cls] - (tile_ids - tile_start[tile_cls]) * TM, 0, TM), 0)
    within = jnp.arange(n_slots, dtype=jnp.int32).reshape(n_tiles, TM) - (tile_start[tile_cls] * TM)[:, None]
    dense_idx = jnp.clip(dense_start[tile_cls][:, None] + within, 0, n_tok - 1)
    in_tile = jnp.arange(TM, dtype=jnp.int32)[None, :] < n_valid[:, None]
    src = jnp.where(in_tile, order[dense_idx], 0).reshape(n_tiles, 1, TM)
    src_q, src_s = src // SUBLANES, src % SUBLANES

    pair_lo = jnp.array([0, 0, 0, 1, 1, 2], jnp.int32)
    pair_hi = jnp.array([1, 2, 3, 2, 3, 3], jnp.int32)
    e_lo = (tile_cls // N_PAIRS) * N_EXP_PER_GROUP + pair_lo[tile_cls % N_PAIRS]
    e_hi = (tile_cls // N_PAIRS) * N_EXP_PER_GROUP + pair_hi[tile_cls % N_PAIRS]

    row = lambda v: v.reshape(1, -1)
    idx_spec = lambda f: pl.BlockSpec((1, 1, TM), f, memory_space=pltpu.SMEM)
    any_spec = pl.BlockSpec(memory_space=pl.ANY)
    lo_spec = lambda s: pl.BlockSpec((1,) + s, lambda i, elo, ehi, nv: (elo[i], 0, 0))
    hi_spec = lambda s: pl.BlockSpec((1,) + s, lambda i, elo, ehi, nv: (ehi[i], 0, 0))
    up_shape, down_shape = (D_MODEL, D_EXPERT), (D_EXPERT, D_MODEL)
    consts = [row(P['g_ple']), P['w_pg'], P['w_pp'], row(P['g_final'])]
    this_tile = lambda i, *_: (i, 0, 0)
    next_tile = lambda i, *_: (jnp.minimum(i + 1, n_tiles - 1), 0, 0)
    y = pl.pallas_call(
        _expert_kernel,
        grid_spec=pltpu.PrefetchScalarGridSpec(
            num_scalar_prefetch=3,
            grid=(n_tiles,),
            in_specs=[idx_spec(this_tile), idx_spec(this_tile), idx_spec(next_tile), idx_spec(next_tile),
                      any_spec, _const_spec((1, D_MODEL)),
                      lo_spec(up_shape), lo_spec(up_shape), lo_spec(down_shape),
                      hi_spec(up_shape), hi_spec(up_shape), hi_spec(down_shape)]
                     + [_const_spec(w.shape) for w in consts],
            out_specs=any_spec,
            scratch_shapes=[pltpu.VMEM((2, TM // SUBLANES, SUBLANES, D_ROW), f32),
                            pltpu.VMEM((2, TM // SUBLANES, SUBLANES, D_MODEL), f32),
                            pltpu.SemaphoreType.DMA((2,)), pltpu.SemaphoreType.DMA((2,))],
        ),
        out_shape=jax.ShapeDtypeStruct((n_tok // SUBLANES, SUBLANES, D_MODEL), f32),
        compiler_params=pltpu.CompilerParams(dimension_semantics=("arbitrary",), vmem_limit_bytes=VMEM_LIMIT),
        name="experts",
    )(e_lo, e_hi, n_valid.astype(jnp.int32), src_q, src_s, src_q, src_s,
      hx.reshape(n_tok // SUBLANES, SUBLANES, D_ROW), row(P['g_ffn']),
      P['w1'], P['w3'], P['w2'], P['w1'], P['w3'], P['w2'], *consts)
    return y.reshape(n_tok, D_MODEL)


def _prepare(g_mix, w_in, conv_w, conv_b, lru_w_a, lru_b_a, lru_w_x, lru_b_x, lru_lambda, sgu_ln_g, sgu_ln_b,
             sgu_w_s, sgu_b_s, g_rec_out, g_sgu_out, w_out, g_ffn, w_router_group, b_router_group,
             w_router_expert, b_router_expert, w_exp_gate, w_exp_up, w_exp_down, g_ple, w_ple_gate, w_ple_proj,
             g_final):
    l = 0
    w = w_in[l]
    ws_t = jnp.transpose(sgu_w_s[l], (0, 2, 1))
    z = jnp.zeros((CHUNK, CHUNK), f32)
    bd = jnp.stack([jnp.block([[ws_t[h], z], [z, ws_t[h]]]) for h in range(N_SGU_HEADS)]).astype(bf16)
    w_rt = jnp.concatenate([w_router_group[l].T, w_router_expert[l].T,
                            jnp.zeros((ROUTE_ROWS - N_GROUPS - N_EXPERTS, D_MODEL), f32)], axis=0).astype(bf16)
    b_rt = jnp.concatenate([b_router_group[l], b_router_expert[l],
                            jnp.zeros((ROUTE_ROWS - N_GROUPS - N_EXPERTS,), f32)]).reshape(ROUTE_ROWS, 1)
    return dict(
        g_mix=g_mix[l], w_xr=w[:, :D_REC].astype(bf16), w_gr=w[:, D_REC:2 * D_REC].astype(bf16),
        w_uvt=w[:, 2 * D_REC:].T.astype(bf16),
        conv_w=conv_w[l], conv_b=conv_b[l],
        wg_f=_gate_blocks(lru_w_a[l, 0], lru_w_x[l, 0]), wg_b=_gate_blocks(lru_w_a[l, 1], lru_w_x[l, 1]),
        ba_f=lru_b_a[l, 0], ba_b=lru_b_a[l, 1], bx_f=lru_b_x[l, 0], bx_b=lru_b_x[l, 1],
        lam_f=lru_lambda[l, 0], lam_b=lru_lambda[l, 1],
        ln_g=sgu_ln_g[l].reshape(D_SGU, 1), ln_b=sgu_ln_b[l].reshape(D_SGU, 1),
        bd=bd, bst=jnp.repeat(sgu_b_s[l], SGU_HEAD, axis=0),
        g_rec=g_rec_out[l], g_sgu=g_sgu_out[l].reshape(D_SGU, 1),
        w_or=w_out[l, :D_REC].astype(bf16), w_os=w_out[l, D_REC:].astype(bf16),
        g_ffn=g_ffn[l], w_rt=w_rt, b_rt=b_rt,
        w1=w_exp_gate[l].astype(bf16), w3=w_exp_up[l].astype(bf16), w2=w_exp_down[l].astype(bf16),
        g_ple=g_ple[l], w_pg=w_ple_gate[l].astype(bf16), w_pp=w_ple_proj[l].astype(bf16), g_final=g_final,
    )


def kernel(x_prompt, x_sample, p_prompt, p_sample, g_mix, w_in, conv_w, conv_b, lru_w_a, lru_b_a, lru_w_x, lru_b_x, lru_lambda, sgu_ln_g, sgu_ln_b, sgu_w_s, sgu_b_s, g_rec_out, g_sgu_out, w_out, g_ffn, w_router_group, b_router_group, w_router_expert, b_router_expert, w_exp_gate, w_exp_up, w_exp_down, g_ple, w_ple_gate, w_ple_proj, g_final):
    assert w_in.shape[0] == 1, "single-layer trunk"
    P = _prepare(g_mix, w_in, conv_w, conv_b, lru_w_a, lru_b_a, lru_w_x, lru_b_x, lru_lambda, sgu_ln_g, sgu_ln_b,
                 sgu_w_s, sgu_b_s, g_rec_out, g_sgu_out, w_out, g_ffn, w_router_group, b_router_group,
                 w_router_expert, b_router_expert, w_exp_gate, w_exp_up, w_exp_down, g_ple, w_ple_gate,
                 w_ple_proj, g_final)
    outs = []
    for x, p in ((x_prompt, p_prompt), (x_sample, p_sample)):
        y = _expert_stage(_mix_trunk(x, p[0], P), P)
        outs.append(y.reshape(x.shape))
    return tuple(outs)
```

```python
import functools

import jax
import jax.numpy as jnp
from jax import lax
from jax.experimental import pallas as pl
from jax.experimental.pallas import tpu as pltpu

f32 = jnp.float32
bf16 = jnp.bfloat16

D_MODEL = 1024
D_REC = 512
N_REC_HEADS = 8
REC_HEAD = 64
CONV_W = 4
CONV_LEFT = 2
LRU_C = 8.0
D_SGU = 512
N_SGU_HEADS = 8
SGU_HEAD = 64
CHUNK = 128
N_GROUPS = 4
N_EXP_PER_GROUP = 4
N_EXPERTS = 16
D_EXPERT = 512
D_PLE = 256
EPS = 1e-6

LANES = 128
SUBLANES = 8
NB = SUBLANES
ST = CHUNK
TT = NB * ST
HALO = SUBLANES
PITCH = ST + SUBLANES
N_LANE_GROUPS = D_REC // LANES
MXU_DIM = 256
TM = 512
N_PAIRS = 6
N_CLASSES = N_GROUPS * N_PAIRS
D_ROW = D_MODEL + D_PLE + LANES
ROUTE_ROWS = 32
VMEM_LIMIT = 60 * 1024 * 1024


def _rms(x, g):
    ms = jnp.mean(x * x, axis=-1, keepdims=True)
    return x * lax.rsqrt(ms + EPS) * g


def _softplus(x):
    return jnp.maximum(x, 0.0) + jnp.log1p(jnp.exp(-jnp.abs(x)))


def _gate_ab(xc, wg_ref, ba, bx, lam):
    xcb = xc.astype(bf16)
    sp = _softplus(-lam)
    out = []
    for g in range(N_LANE_GROUPS):
        sl = slice(LANES * g, LANES * (g + 1))
        pre = jnp.dot(xcb[:, sl], wg_ref[g], preferred_element_type=f32)
        r = jax.nn.sigmoid(pre[:, :LANES] + ba[:, sl])
        i = jax.nn.sigmoid(pre[:, LANES:] + bx[:, sl])
        log_a = -LRU_C * r * sp[:, sl]
        a = jnp.exp(log_a)
        mult = jnp.sqrt(jnp.tanh(-log_a) * (1.0 + a * a))
        out.append((a, mult * (i * xc[:, sl])))
    return out


def _store_ab(ab, a_ref, b_ref):
    for g, (a, b) in enumerate(ab):
        for s in range(NB):
            a_ref[g, s * PITCH:s * PITCH + ST, :] = a[s * ST:(s + 1) * ST, :]
            b_ref[g, s * PITCH:s * PITCH + ST, :] = b[s * ST:(s + 1) * ST, :]


def _scan(a_ref, b_ref, carry_ref, reverse):
    def step(k, hs):
        t = (ST - 1 - k) if reverse else k
        new = []
        for g in range(N_LANE_GROUPS):
            a = a_ref[g, pl.ds(t, NB, stride=PITCH), :]
            b = b_ref[g, pl.ds(t, NB, stride=PITCH), :]
            h = a * hs[g] + b
            b_ref[g, pl.ds(t, NB, stride=PITCH), :] = h
            new.append(h)
        return tuple(new)

    init = tuple(carry_ref[:, LANES * g:LANES * (g + 1)] for g in range(N_LANE_GROUPS))
    hs = lax.fori_loop(0, ST, step, init, unroll=8)
    for g in range(N_LANE_GROUPS):
        carry_ref[:, LANES * g:LANES * (g + 1)] = hs[g]


def _bwd_kernel(x_ref, xp_ref, xn_ref, gmix_ref, wxr_ref, cw_ref, cb_ref, wg_ref, ba_ref, bx_ref, lam_ref,
                xc_ref, hb_ref, ext_ref, a_ref, b_ref, carry_ref):
    j = pl.program_id(1)
    n_t = pl.num_programs(1)
    tb = n_t - 1 - j

    @pl.when(j == 0)
    def _():
        carry_ref[...] = jnp.zeros_like(carry_ref)

    gm = gmix_ref[...]

    def proj(xv):
        return jnp.dot(_rms(xv, gm).astype(bf16), wxr_ref[...], preferred_element_type=f32)

    xr = proj(x_ref[...].reshape(TT, D_MODEL))
    ext_ref[:, HALO:HALO + ST, :] = xr.reshape(NB, ST, D_REC)
    hp = proj(xp_ref[...].reshape(NB * HALO, D_MODEL)).reshape(NB, HALO, D_REC)
    ext_ref[:, 0:HALO, :] = jnp.where(tb > 0, hp, 0.0)
    hn = proj(xn_ref[...].reshape(NB * HALO, D_MODEL)).reshape(NB, HALO, D_REC)
    ext_ref[:, HALO + ST:2 * HALO + ST, :] = jnp.where(tb < n_t - 1, hn, 0.0)

    cw = cw_ref[...]
    xc = None
    for k in range(CONV_W):
        off = HALO - CONV_LEFT + k
        term = ext_ref[:, off:off + ST, :] * cw[k:k + 1, :]
        xc = term if xc is None else xc + term
    xc = xc + cb_ref[...]
    xc_ref[...] = xc

    ab = _gate_ab(xc.reshape(TT, D_REC), wg_ref, ba_ref[...], bx_ref[...], lam_ref[...])
    _store_ab(ab, a_ref, b_ref)
    _scan(a_ref, b_ref, carry_ref, reverse=True)
    for g in range(N_LANE_GROUPS):
        for s in range(NB):
            hb_ref[s, :, LANES * g:LANES * (g + 1)] = b_ref[g, s * PITCH:s * PITCH + ST, :]


def _route(logits):
    gl = [logits[i:i + 1, :] for i in range(N_GROUPS)]
    gm = jnp.maximum(jnp.maximum(gl[0], gl[1]), jnp.maximum(gl[2], gl[3]))
    sg = sum(jnp.exp(v - gm) for v in gl)
    g_top = 1.0 / sg
    gidx = jnp.where(gl[0] == gm, 0, jnp.where(gl[1] == gm, 1, jnp.where(gl[2] == gm, 2, 3)))
    el = []
    for k in range(N_EXP_PER_GROUP):
        rows = [logits[N_GROUPS + N_EXP_PER_GROUP * g + k:N_GROUPS + N_EXP_PER_GROUP * g + k + 1, :]
                for g in range(N_GROUPS)]
        el.append(jnp.where(gidx == 0, rows[0], jnp.where(gidx == 1, rows[1], jnp.where(gidx == 2, rows[2], rows[3]))))
    em = jnp.maximum(jnp.maximum(el[0], el[1]), jnp.maximum(el[2], el[3]))
    ee = [jnp.exp(v - em) for v in el]
    se = ee[0] + ee[1] + ee[2] + ee[3]
    p = [v / se for v in ee]
    p1 = jnp.maximum(jnp.maximum(p[0], p[1]), jnp.maximum(p[2], p[3]))
    i1 = jnp.where(p[0] == p1, 0, jnp.where(p[1] == p1, 1, jnp.where(p[2] == p1, 2, 3)))
    q = [jnp.where(i1 == k, -1.0, p[k]) for k in range(N_EXP_PER_GROUP)]
    p2 = jnp.maximum(jnp.maximum(q[0], q[1]), jnp.maximum(q[2], q[3]))
    i2 = jnp.where(q[0] == p2, 0, jnp.where(q[1] == p2, 1, jnp.where(q[2] == p2, 2, 3)))
    norm = p1 + p2
    w1 = g_top * (p1 / norm)
    w2 = g_top * (p2 / norm)
    first_is_lo = i1 < i2
    lo = jnp.where(first_is_lo, i1, i2)
    hi = jnp.where(first_is_lo, i2, i1)
    w_lo = jnp.where(first_is_lo, w1, w2)
    w_hi = jnp.where(first_is_lo, w2, w1)
    base = jnp.where(lo == 0, 0, jnp.where(lo == 1, 3, 5))
    cls = gidx * N_PAIRS + base + hi - lo - 1
    return cls.astype(f32), w_lo, w_hi


def _fwd_kernel(x_ref, xc_ref, hb_ref, p_ref, gmix_ref, wgr_ref, wuvt_ref, wg_ref, ba_ref, bx_ref, lam_ref,
                lng_ref, lnb_ref, bd_ref, bst_ref, grec_ref, gsgu_ref, wor_ref, wos_ref, gffn_ref, wrt_ref, brt_ref,
                hx_ref, a_ref, b_ref, carry_ref):
    j = pl.program_id(1)

    @pl.when(j == 0)
    def _():
        carry_ref[...] = jnp.zeros_like(carry_ref)

    x2 = x_ref[...].reshape(TT, D_MODEL)
    n = _rms(x2, gmix_ref[...]).astype(bf16)

    ab = _gate_ab(xc_ref[...].reshape(TT, D_REC), wg_ref, ba_ref[...], bx_ref[...], lam_ref[...])
    _store_ab(ab, a_ref, b_ref)
    _scan(a_ref, b_ref, carry_ref, reverse=False)
    hf = jnp.concatenate(
        [jnp.concatenate([b_ref[g, s * PITCH:s * PITCH + ST, :] for s in range(NB)], axis=0)
         for g in range(N_LANE_GROUPS)], axis=1)
    gr = jnp.dot(n, wgr_ref[...], preferred_element_type=f32)
    y_rec = (hf + hb_ref[...].reshape(TT, D_REC)) * jax.nn.gelu(gr)
    rec_n = _rms(y_rec, grec_ref[...]).astype(bf16)

    zt = lax.dot_general(wuvt_ref[...], n, (((1,), (1,)), ((), ())), preferred_element_type=f32)
    ut = jax.nn.gelu(zt[:D_SGU, :])
    vt = jax.nn.gelu(zt[D_SGU:, :])
    mu = jnp.mean(vt, axis=0, keepdims=True)
    vc = vt - mu
    var = jnp.mean(vc * vc, axis=0, keepdims=True)
    vn = (vc * lax.rsqrt(var + EPS) * lng_ref[...] + lnb_ref[...]).astype(bf16)
    n_slab = TT // MXU_DIM
    heads = []
    for h in range(N_SGU_HEADS):
        rows = vn[SGU_HEAD * h:SGU_HEAD * (h + 1), :]
        lhs = jnp.concatenate([rows[:, MXU_DIM * c:MXU_DIM * (c + 1)] for c in range(n_slab)], axis=0)
        res = jnp.dot(lhs, bd_ref[h], preferred_element_type=f32)
        heads.append(jnp.concatenate([res[SGU_HEAD * c:SGU_HEAD * (c + 1), :] for c in range(n_slab)], axis=1))
    mixed = jnp.concatenate(heads, axis=0) + jnp.concatenate([bst_ref[...]] * (TT // CHUNK), axis=1)
    yst = ut * mixed
    ms = jnp.mean(yst * yst, axis=0, keepdims=True)
    sgu_n = (yst * lax.rsqrt(ms + EPS) * gsgu_ref[...]).T.astype(bf16)

    delta = (jnp.dot(rec_n, wor_ref[...], preferred_element_type=f32)
             + jnp.dot(sgu_n, wos_ref[...], preferred_element_type=f32))
    h1 = x2 + delta
    hx_ref[:, :, 0:D_MODEL] = h1.reshape(NB, ST, D_MODEL)
    hx_ref[:, :, D_MODEL:D_MODEL + D_PLE] = p_ref[...]

    n2 = _rms(h1, gffn_ref[...]).astype(bf16)
    logits = lax.dot_general(wrt_ref[...], n2, (((1,), (1,)), ((), ())), preferred_element_type=f32) + brt_ref[...]
    cls, w_lo, w_hi = _route(logits)
    rt = jnp.concatenate([cls, w_lo, w_hi, jnp.zeros((LANES - 3, TT), f32)], axis=0).T
    hx_ref[:, :, D_MODEL + D_PLE:D_ROW] = rt.reshape(NB, ST, LANES)


def _expert_kernel(elo_ref, ehi_ref, nval_ref, cq_ref, cs_ref, nq_ref, ns_ref, pq_ref, ps_ref, hx_hbm, gffn_ref,
                   w1lo_ref, w3lo_ref, w2lo_ref, w1hi_ref, w3hi_ref, w2hi_ref,
                   gple_ref, wpg_ref, wpp_ref, gfin_ref, y_hbm, xbuf0, xbuf1, ybuf0, ybuf1, gsem, ssem):
    i = pl.program_id(0)
    n = pl.num_programs(0)
    cnt = nval_ref[i]
    nxt = jnp.minimum(i + 1, n - 1)
    n_grp = TM // SUBLANES

    def gather_row(q_ref, s_ref, buf, par, k, u):
        r = k * SUBLANES + u
        return pltpu.make_async_copy(hx_hbm.at[q_ref[0, 0, r], pl.ds(s_ref[0, 0, r], 1)],
                                     buf.at[k, pl.ds(u, 1)], gsem.at[par])

    def scatter_row(q_ref, s_ref, buf, par, k, u):
        r = k * SUBLANES + u
        return pltpu.make_async_copy(buf.at[k, pl.ds(u, 1)],
                                     y_hbm.at[q_ref[0, 0, r], pl.ds(s_ref[0, 0, r], 1)], ssem.at[par])

    def wait_tile_in(buf, par):
        pltpu.make_async_copy(hx_hbm.at[pl.ds(0, n_grp)], buf, gsem.at[par]).wait()

    def wait_tile_out(buf, par):
        pltpu.make_async_copy(buf, y_hbm.at[pl.ds(0, n_grp)], ssem.at[par]).wait()

    def compute(xb, yb):
        xrow = xb[...].reshape(TM, D_ROW)
        xs = xrow[:, 0:D_MODEL]
        n2 = _rms(xs, gffn_ref[...]).astype(bf16)

        def expert(w1_ref, w3_ref, w2_ref):
            a = jnp.dot(n2, w1_ref[0], preferred_element_type=f32)
            b = jnp.dot(n2, w3_ref[0], preferred_element_type=f32)
            hdn = (jax.nn.silu(a) * b).astype(bf16)
            return jnp.dot(hdn, w2_ref[0], preferred_element_type=f32)

        ws = xrow[:, D_MODEL + D_PLE:D_ROW]
        out = ws[:, 1:2] * expert(w1lo_ref, w3lo_ref, w2lo_ref)
        out = out + ws[:, 2:3] * expert(w1hi_ref, w3hi_ref, w2hi_ref)
        h2 = xs + out
        n3 = _rms(h2, gple_ref[...]).astype(bf16)
        gate = jax.nn.sigmoid(jnp.dot(n3, wpg_ref[...], preferred_element_type=f32))
        ps = xrow[:, D_MODEL:D_MODEL + D_PLE]
        proj = jnp.dot(ps.astype(bf16), wpp_ref[...], preferred_element_type=f32)
        h3 = h2 + gate * proj
        yb[...] = _rms(h3, gfin_ref[...]).reshape(n_grp, SUBLANES, D_MODEL)

    def step(par):
        xb, xo = (xbuf0, xbuf1) if par == 0 else (xbuf1, xbuf0)
        yb, yo = (ybuf0, ybuf1) if par == 0 else (ybuf1, ybuf0)

        @pl.when(i == 0)
        def _():
            def body(k, c):
                for u in range(SUBLANES):
                    gather_row(cq_ref, cs_ref, xb, par, k, u).start(priority=u % 2)
                return c
            lax.fori_loop(0, n_grp, body, 0)
            yo[...] = jnp.zeros_like(yo)

        wait_tile_in(xb, par)

        @pl.when(i >= 1)
        def _():
            wait_tile_out(yb, par)

        for k in range(n_grp):
            for u in range(SUBLANES):
                gather_row(nq_ref, ns_ref, xo, 1 - par, k, u).start(priority=u % 2)
                scatter_row(pq_ref, ps_ref, yo, 1 - par, k, u).start(priority=u % 2)
        compute(xb, yb)

        @pl.when((i == n - 1) | (nval_ref[nxt] == 0))
        def _():
            wait_tile_in(xo, 1 - par)
            wait_tile_out(yo, 1 - par)

            def send(r, c):
                scatter_row(cq_ref, cs_ref, yb, par, r // SUBLANES, r % SUBLANES).start()
                return c
            lax.fori_loop(0, cnt, send, 0)

            def drain(r, c):
                scatter_row(cq_ref, cs_ref, yb, par, 0, 0).wait()
                return c
            lax.fori_loop(0, cnt, drain, 0)

    @pl.when(cnt > 0)
    def _():
        @pl.when(i % 2 == 0)
        def _():
            step(0)

        @pl.when(i % 2 == 1)
        def _():
            step(1)


def _const_spec(shape):
    nd = len(shape)
    return pl.BlockSpec(shape, lambda *_: (0,) * nd)


def _gate_blocks(w_a, w_x):
    def pair(w):
        z = jnp.zeros((REC_HEAD, REC_HEAD), w.dtype)
        return jnp.stack([jnp.block([[w[2 * g], z], [z, w[2 * g + 1]]]) for g in range(N_LANE_GROUPS)])
    return jnp.concatenate([pair(w_a), pair(w_x)], axis=-1).astype(bf16)


def _mix_trunk(x, p, P):
    bsz, seq, _ = x.shape
    assert bsz % NB == 0 and seq % ST == 0
    n_g, n_t = bsz // NB, seq // ST
    halo_blocks = seq // HALO
    row = lambda v: v.reshape(1, -1)

    x_spec_b = pl.BlockSpec((NB, ST, D_MODEL), lambda g, j: (g, n_t - 1 - j, 0))
    xp_spec = pl.BlockSpec((NB, HALO, D_MODEL),
                           lambda g, j: (g, jnp.maximum((n_t - 1 - j) * (ST // HALO) - 1, 0), 0))
    xn_spec = pl.BlockSpec((NB, HALO, D_MODEL),
                           lambda g, j: (g, jnp.minimum((n_t - j) * (ST // HALO), halo_blocks - 1), 0))
    rec_spec_b = pl.BlockSpec((NB, ST, D_REC), lambda g, j: (g, n_t - 1 - j, 0))
    scan_scratch = [pltpu.VMEM((N_LANE_GROUPS, NB * PITCH, LANES), f32),
                    pltpu.VMEM((N_LANE_GROUPS, NB * PITCH, LANES), f32),
                    pltpu.VMEM((NB, D_REC), f32)]

    bwd_w = [row(P['g_mix']), P['w_xr'], P['conv_w'], row(P['conv_b']), P['wg_b'],
             row(P['ba_b']), row(P['bx_b']), row(P['lam_b'])]
    xc, hb = pl.pallas_call(
        _bwd_kernel,
        grid=(n_g, n_t),
        in_specs=[x_spec_b, xp_spec, xn_spec] + [_const_spec(w.shape) for w in bwd_w],
        out_specs=[rec_spec_b, rec_spec_b],
        out_shape=[jax.ShapeDtypeStruct((bsz, seq, D_REC), f32)] * 2,
        scratch_shapes=[pltpu.VMEM((NB, ST + 2 * HALO, D_REC), f32)] + scan_scratch,
        compiler_params=pltpu.CompilerParams(dimension_semantics=("arbitrary", "arbitrary"),
                                             vmem_limit_bytes=VMEM_LIMIT),
        name="bwd_scan",
    )(x, x, x, *bwd_w)

    tile_spec = lambda d: pl.BlockSpec((NB, ST, d), lambda g, j: (g, j, 0))
    fwd_w = [row(P['g_mix']), P['w_gr'], P['w_uvt'], P['wg_f'], row(P['ba_f']), row(P['bx_f']), row(P['lam_f']),
             P['ln_g'], P['ln_b'], P['bd'], P['bst'], row(P['g_rec']), P['g_sgu'], P['w_or'], P['w_os'],
             row(P['g_ffn']), P['w_rt'], P['b_rt']]
    hx = pl.pallas_call(
        _fwd_kernel,
        grid=(n_g, n_t),
        in_specs=[tile_spec(D_MODEL), tile_spec(D_REC), tile_spec(D_REC), tile_spec(D_PLE)]
                 + [_const_spec(w.shape) for w in fwd_w],
        out_specs=tile_spec(D_ROW),
        out_shape=jax.ShapeDtypeStruct((bsz, seq, D_ROW), f32),
        scratch_shapes=scan_scratch,
        compiler_params=pltpu.CompilerParams(dimension_semantics=("arbitrary", "arbitrary"),
                                             vmem_limit_bytes=VMEM_LIMIT),
        name="fwd_mix",
    )(x, xc, hb, p, *fwd_w)
    return hx.reshape(bsz * seq, D_ROW)


def _expert_stage(hx, P):
    n_tok = hx.shape[0]
    assert n_tok % TM == 0
    n_tiles = n_tok // TM + N_CLASSES
    assert n_tok // TM > N_CLASSES
    i32 = jnp.int32
    cls = hx[:, D_MODEL + D_PLE].astype(i32)

    class_ids = jnp.arange(N_CLASSES, dtype=i32)
    counts = jnp.sum((cls[:, None] == class_ids[None, :]).astype(i32), axis=0)
    order = jnp.argsort(cls, stable=True).astype(i32)
    dense_start = jnp.cumsum(counts) - counts
    full_c, rem_c = counts // TM, counts % TM
    cum_part = jnp.cumsum((rem_c > 0).astype(i32))
    cum_full = jnp.cumsum(full_c)
    n_part, n_full = cum_part[-1], cum_full[-1]
    tile_ids = jnp.arange(n_tiles, dtype=i32)
    is_part = tile_ids < n_part
    active = tile_ids < n_part + n_full
    first_reaching = lambda cum, k: jnp.minimum(
        jnp.sum((cum[None, :] <= k[:, None]).astype(i32), axis=1), N_CLASSES - 1)
    cls_part = first_reaching(cum_part, tile_ids)
    full_id = jnp.maximum(tile_ids - n_part, 0)
    cls_full = first_reaching(cum_full, full_id)
    tile_cls = jnp.where(is_part, cls_part, cls_full)
    n_valid = jnp.where(active, jnp.where(is_part, rem_c[tile_cls], TM), 0)
    in_class = jnp.where(is_part, full_c[tile_cls], full_id - (cum_full - full_c)[tile_cls]) * TM
    tile_dense = dense_start[tile_cls] + in_class
    rank_end = jnp.cumsum(n_valid)
    rank_start = rank_end - n_valid
    ranks = jnp.arange(n_tok, dtype=i32)
    tile_of_rank = jnp.sum((rank_end[None, :] <= ranks[:, None]).astype(i32), axis=1)
    tok_by_rank = order[tile_dense[tile_of_rank] + ranks - rank_start[tile_of_rank]]
    look = jnp.minimum(rank_start[:, None] + jnp.arange(TM, dtype=i32)[None, :], n_tok - 1)
    src = tok_by_rank[look].reshape(n_tiles, 1, TM)
    src_q, src_s = src // SUBLANES, src % SUBLANES

    pair_lo = jnp.array([0, 0, 0, 1, 1, 2], jnp.int32)
    pair_hi = jnp.array([1, 2, 3, 2, 3, 3], jnp.int32)
    e_lo = (tile_cls // N_PAIRS) * N_EXP_PER_GROUP + pair_lo[tile_cls % N_PAIRS]
    e_hi = (tile_cls // N_PAIRS) * N_EXP_PER_GROUP + pair_hi[tile_cls % N_PAIRS]

    row = lambda v: v.reshape(1, -1)
    idx_spec = lambda f: pl.BlockSpec((1, 1, TM), f, memory_space=pltpu.SMEM)
    any_spec = pl.BlockSpec(memory_space=pl.ANY)
    lo_spec = lambda s: pl.BlockSpec((1,) + s, lambda i, elo, ehi, nv: (elo[i], 0, 0))
    hi_spec = lambda s: pl.BlockSpec((1,) + s, lambda i, elo, ehi, nv: (ehi[i], 0, 0))
    up_shape, down_shape = (D_MODEL, D_EXPERT), (D_EXPERT, D_MODEL)
    consts = [row(P['g_ple']), P['w_pg'], P['w_pp'], row(P['g_final'])]
    this_tile = lambda i, *_: (i, 0, 0)
    next_tile = lambda i, *_: (jnp.minimum(i + 1, n_tiles - 1), 0, 0)
    prev_tile = lambda i, *_: (jnp.maximum(i - 1, 0), 0, 0)
    tile_buf = lambda d: pltpu.VMEM((TM // SUBLANES, SUBLANES, d), f32)
    y = pl.pallas_call(
        _expert_kernel,
        grid_spec=pltpu.PrefetchScalarGridSpec(
            num_scalar_prefetch=3,
            grid=(n_tiles,),
            in_specs=[idx_spec(this_tile), idx_spec(this_tile), idx_spec(next_tile), idx_spec(next_tile),
                      idx_spec(prev_tile), idx_spec(prev_tile),
                      any_spec, _const_spec((1, D_MODEL)),
                      lo_spec(up_shape), lo_spec(up_shape), lo_spec(down_shape),
                      hi_spec(up_shape), hi_spec(up_shape), hi_spec(down_shape)]
                     + [_const_spec(w.shape) for w in consts],
            out_specs=any_spec,
            scratch_shapes=[tile_buf(D_ROW), tile_buf(D_ROW), tile_buf(D_MODEL), tile_buf(D_MODEL),
                            pltpu.SemaphoreType.DMA((2,)), pltpu.SemaphoreType.DMA((2,))],
        ),
        out_shape=jax.ShapeDtypeStruct((n_tok // SUBLANES, SUBLANES, D_MODEL), f32),
        compiler_params=pltpu.CompilerParams(dimension_semantics=("arbitrary",), vmem_limit_bytes=VMEM_LIMIT),
        name="experts",
    )(e_lo, e_hi, n_valid.astype(i32), src_q, src_s, src_q, src_s, src_q, src_s,
      hx.reshape(n_tok // SUBLANES, SUBLANES, D_ROW), row(P['g_ffn']),
      P['w1'], P['w3'], P['w2'], P['w1'], P['w3'], P['w2'], *consts)
    return y.reshape(n_tok, D_MODEL)


def _prepare(g_mix, w_in, conv_w, conv_b, lru_w_a, lru_b_a, lru_w_x, lru_b_x, lru_lambda, sgu_ln_g, sgu_ln_b,
             sgu_w_s, sgu_b_s, g_rec_out, g_sgu_out, w_out, g_ffn, w_router_group, b_router_group,
             w_router_expert, b_router_expert, w_exp_gate, w_exp_up, w_exp_down, g_ple, w_ple_gate, w_ple_proj,
             g_final):
    l = 0
    w = w_in[l]
    ws_t = jnp.transpose(sgu_w_s[l], (0, 2, 1))
    z = jnp.zeros((CHUNK, CHUNK), f32)
    bd = jnp.stack([jnp.block([[ws_t[h], z], [z, ws_t[h]]]) for h in range(N_SGU_HEADS)]).astype(bf16)
    w_rt = jnp.concatenate([w_router_group[l].T, w_router_expert[l].T,
                            jnp.zeros((ROUTE_ROWS - N_GROUPS - N_EXPERTS, D_MODEL), f32)], axis=0).astype(bf16)
    b_rt = jnp.concatenate([b_router_group[l], b_router_expert[l],
                            jnp.zeros((ROUTE_ROWS - N_GROUPS - N_EXPERTS,), f32)]).reshape(ROUTE_ROWS, 1)
    return dict(
        g_mix=g_mix[l], w_xr=w[:, :D_REC].astype(bf16), w_gr=w[:, D_REC:2 * D_REC].astype(bf16),
        w_uvt=w[:, 2 * D_REC:].T.astype(bf16),
        conv_w=conv_w[l], conv_b=conv_b[l],
        wg_f=_gate_blocks(lru_w_a[l, 0], lru_w_x[l, 0]), wg_b=_gate_blocks(lru_w_a[l, 1], lru_w_x[l, 1]),
        ba_f=lru_b_a[l, 0], ba_b=lru_b_a[l, 1], bx_f=lru_b_x[l, 0], bx_b=lru_b_x[l, 1],
        lam_f=lru_lambda[l, 0], lam_b=lru_lambda[l, 1],
        ln_g=sgu_ln_g[l].reshape(D_SGU, 1), ln_b=sgu_ln_b[l].reshape(D_SGU, 1),
        bd=bd, bst=jnp.repeat(sgu_b_s[l], SGU_HEAD, axis=0),
        g_rec=g_rec_out[l], g_sgu=g_sgu_out[l].reshape(D_SGU, 1),
        w_or=w_out[l, :D_REC].astype(bf16), w_os=w_out[l, D_REC:].astype(bf16),
        g_ffn=g_ffn[l], w_rt=w_rt, b_rt=b_rt,
        w1=w_exp_gate[l].astype(bf16), w3=w_exp_up[l].astype(bf16), w2=w_exp_down[l].astype(bf16),
        g_ple=g_ple[l], w_pg=w_ple_gate[l].astype(bf16), w_pp=w_ple_proj[l].astype(bf16), g_final=g_final,
    )


def kernel(x_prompt, x_sample, p_prompt, p_sample, g_mix, w_in, conv_w, conv_b, lru_w_a, lru_b_a, lru_w_x, lru_b_x, lru_lambda, sgu_ln_g, sgu_ln_b, sgu_w_s, sgu_b_s, g_rec_out, g_sgu_out, w_out, g_ffn, w_router_group, b_router_group, w_router_expert, b_router_expert, w_exp_gate, w_exp_up, w_exp_down, g_ple, w_ple_gate, w_ple_proj, g_final):
    assert w_in.shape[0] == 1, "single-layer trunk"
    P = _prepare(g_mix, w_in, conv_w, conv_b, lru_w_a, lru_b_a, lru_w_x, lru_b_x, lru_lambda, sgu_ln_g, sgu_ln_b,
                 sgu_w_s, sgu_b_s, g_rec_out, g_sgu_out, w_out, g_ffn, w_router_group, b_router_group,
                 w_router_expert, b_router_expert, w_exp_gate, w_exp_up, w_exp_down, g_ple, w_ple_gate,
                 w_ple_proj, g_final)
    outs = []
    for x, p in ((x_prompt, p_prompt), (x_sample, p_sample)):
        y = _expert_stage(_mix_trunk(x, p[0], P), P)
        outs.append(y.reshape(x.shape))
    return tuple(outs)
```

```python
import functools

import jax
import jax.numpy as jnp
from jax import lax
from jax.experimental import pallas as pl
from jax.experimental.pallas import tpu as pltpu

f32 = jnp.float32
bf16 = jnp.bfloat16

D_MODEL = 1024
D_REC = 512
N_REC_HEADS = 8
REC_HEAD = 64
CONV_W = 4
CONV_LEFT = 2
LRU_C = 8.0
D_SGU = 512
N_SGU_HEADS = 8
SGU_HEAD = 64
CHUNK = 128
N_GROUPS = 4
N_EXP_PER_GROUP = 4
N_EXPERTS = 16
D_EXPERT = 512
D_PLE = 256
EPS = 1e-6

LANES = 128
SUBLANES = 8
NB = SUBLANES
ST = CHUNK
TT = NB * ST
HALO = SUBLANES
PITCH = ST + SUBLANES
N_LANE_GROUPS = D_REC // LANES
MXU_DIM = 256
TM = 512
N_PAIRS = 6
N_CLASSES = N_GROUPS * N_PAIRS
D_ROW = D_MODEL + D_PLE + LANES
ROUTE_ROWS = 32
VMEM_LIMIT = 60 * 1024 * 1024


def _rms(x, g):
    ms = jnp.mean(x * x, axis=-1, keepdims=True)
    return x * lax.rsqrt(ms + EPS) * g


def _softplus(x):
    return jnp.maximum(x, 0.0) + jnp.log1p(jnp.exp(-jnp.abs(x)))


def _gate_ab(xc, wg_ref, ba, bx, lam):
    xcb = xc.astype(bf16)
    sp = _softplus(-lam)
    out = []
    for g in range(N_LANE_GROUPS):
        sl = slice(LANES * g, LANES * (g + 1))
        pre = jnp.dot(xcb[:, sl], wg_ref[g], preferred_element_type=f32)
        r = jax.nn.sigmoid(pre[:, :LANES] + ba[:, sl])
        i = jax.nn.sigmoid(pre[:, LANES:] + bx[:, sl])
        log_a = -LRU_C * r * sp[:, sl]
        a = jnp.exp(log_a)
        mult = jnp.sqrt(jnp.tanh(-log_a) * (1.0 + a * a))
        out.append((a, mult * (i * xc[:, sl])))
    return out


def _store_ab(ab, a_ref, b_ref):
    for g, (a, b) in enumerate(ab):
        for s in range(NB):
            a_ref[g, s * PITCH:s * PITCH + ST, :] = a[s * ST:(s + 1) * ST, :]
            b_ref[g, s * PITCH:s * PITCH + ST, :] = b[s * ST:(s + 1) * ST, :]


def _scan(a_ref, b_ref, carry_ref, reverse):
    def step(k, hs):
        t = (ST - 1 - k) if reverse else k
        new = []
        for g in range(N_LANE_GROUPS):
            a = a_ref[g, pl.ds(t, NB, stride=PITCH), :]
            b = b_ref[g, pl.ds(t, NB, stride=PITCH), :]
            h = a * hs[g] + b
            b_ref[g, pl.ds(t, NB, stride=PITCH), :] = h
            new.append(h)
        return tuple(new)

    init = tuple(carry_ref[:, LANES * g:LANES * (g + 1)] for g in range(N_LANE_GROUPS))
    hs = lax.fori_loop(0, ST, step, init, unroll=8)
    for g in range(N_LANE_GROUPS):
        carry_ref[:, LANES * g:LANES * (g + 1)] = hs[g]


def _bwd_kernel(x_ref, xp_ref, xn_ref, gmix_ref, wxr_ref, cw_ref, cb_ref, wg_ref, ba_ref, bx_ref, lam_ref,
                xc_ref, hb_ref, ext_ref, a_ref, b_ref, carry_ref):
    j = pl.program_id(1)
    n_t = pl.num_programs(1)
    tb = n_t - 1 - j

    @pl.when(j == 0)
    def _():
        carry_ref[...] = jnp.zeros_like(carry_ref)

    gm = gmix_ref[...]

    def proj(xv):
        return jnp.dot(_rms(xv, gm).astype(bf16), wxr_ref[...], preferred_element_type=f32)

    xr = proj(x_ref[...].reshape(TT, D_MODEL))
    ext_ref[:, HALO:HALO + ST, :] = xr.reshape(NB, ST, D_REC)
    hp = proj(xp_ref[...].reshape(NB * HALO, D_MODEL)).reshape(NB, HALO, D_REC)
    ext_ref[:, 0:HALO, :] = jnp.where(tb > 0, hp, 0.0)
    hn = proj(xn_ref[...].reshape(NB * HALO, D_MODEL)).reshape(NB, HALO, D_REC)
    ext_ref[:, HALO + ST:2 * HALO + ST, :] = jnp.where(tb < n_t - 1, hn, 0.0)

    cw = cw_ref[...]
    xc = None
    for k in range(CONV_W):
        off = HALO - CONV_LEFT + k
        term = ext_ref[:, off:off + ST, :] * cw[k:k + 1, :]
        xc = term if xc is None else xc + term
    xc = xc + cb_ref[...]
    xc_ref[...] = xc

    ab = _gate_ab(xc.reshape(TT, D_REC), wg_ref, ba_ref[...], bx_ref[...], lam_ref[...])
    _store_ab(ab, a_ref, b_ref)
    _scan(a_ref, b_ref, carry_ref, reverse=True)
    for g in range(N_LANE_GROUPS):
        for s in range(NB):
            hb_ref[s, :, LANES * g:LANES * (g + 1)] = b_ref[g, s * PITCH:s * PITCH + ST, :]


def _route(logits):
    gl = [logits[i:i + 1, :] for i in range(N_GROUPS)]
    gm = jnp.maximum(jnp.maximum(gl[0], gl[1]), jnp.maximum(gl[2], gl[3]))
    sg = sum(jnp.exp(v - gm) for v in gl)
    g_top = 1.0 / sg
    gidx = jnp.where(gl[0] == gm, 0, jnp.where(gl[1] == gm, 1, jnp.where(gl[2] == gm, 2, 3)))
    el = []
    for k in range(N_EXP_PER_GROUP):
        rows = [logits[N_GROUPS + N_EXP_PER_GROUP * g + k:N_GROUPS + N_EXP_PER_GROUP * g + k + 1, :]
                for g in range(N_GROUPS)]
        el.append(jnp.where(gidx == 0, rows[0], jnp.where(gidx == 1, rows[1], jnp.where(gidx == 2, rows[2], rows[3]))))
    em = jnp.maximum(jnp.maximum(el[0], el[1]), jnp.maximum(el[2], el[3]))
    ee = [jnp.exp(v - em) for v in el]
    se = ee[0] + ee[1] + ee[2] + ee[3]
    p = [v / se for v in ee]
    p1 = jnp.maximum(jnp.maximum(p[0], p[1]), jnp.maximum(p[2], p[3]))
    i1 = jnp.where(p[0] == p1, 0, jnp.where(p[1] == p1, 1, jnp.where(p[2] == p1, 2, 3)))
    q = [jnp.where(i1 == k, -1.0, p[k]) for k in range(N_EXP_PER_GROUP)]
    p2 = jnp.maximum(jnp.maximum(q[0], q[1]), jnp.maximum(q[2], q[3]))
    i2 = jnp.where(q[0] == p2, 0, jnp.where(q[1] == p2, 1, jnp.where(q[2] == p2, 2, 3)))
    norm = p1 + p2
    w1 = g_top * (p1 / norm)
    w2 = g_top * (p2 / norm)
    first_is_lo = i1 < i2
    lo = jnp.where(first_is_lo, i1, i2)
    hi = jnp.where(first_is_lo, i2, i1)
    w_lo = jnp.where(first_is_lo, w1, w2)
    w_hi = jnp.where(first_is_lo, w2, w1)
    base = jnp.where(lo == 0, 0, jnp.where(lo == 1, 3, 5))
    cls = gidx * N_PAIRS + base + hi - lo - 1
    return cls.astype(f32), w_lo, w_hi


def _fwd_kernel(x_ref, xc_ref, hb_ref, p_ref, gmix_ref, wgr_ref, wuvt_ref, wg_ref, ba_ref, bx_ref, lam_ref,
                lng_ref, lnb_ref, bd_ref, bst_ref, grec_ref, gsgu_ref, wor_ref, wos_ref, gffn_ref, wrt_ref, brt_ref,
                hx_ref, a_ref, b_ref, carry_ref):
    j = pl.program_id(1)

    @pl.when(j == 0)
    def _():
        carry_ref[...] = jnp.zeros_like(carry_ref)

    x2 = x_ref[...].reshape(TT, D_MODEL)
    n = _rms(x2, gmix_ref[...]).astype(bf16)

    ab = _gate_ab(xc_ref[...].reshape(TT, D_REC), wg_ref, ba_ref[...], bx_ref[...], lam_ref[...])
    _store_ab(ab, a_ref, b_ref)
    _scan(a_ref, b_ref, carry_ref, reverse=False)
    hf = jnp.concatenate(
        [jnp.concatenate([b_ref[g, s * PITCH:s * PITCH + ST, :] for s in range(NB)], axis=0)
         for g in range(N_LANE_GROUPS)], axis=1)
    gr = jnp.dot(n, wgr_ref[...], preferred_element_type=f32)
    y_rec = (hf + hb_ref[...].reshape(TT, D_REC)) * jax.nn.gelu(gr)
    rec_n = _rms(y_rec, grec_ref[...]).astype(bf16)

    zt = lax.dot_general(wuvt_ref[...], n, (((1,), (1,)), ((), ())), preferred_element_type=f32)
    ut = jax.nn.gelu(zt[:D_SGU, :])
    vt = jax.nn.gelu(zt[D_SGU:, :])
    mu = jnp.mean(vt, axis=0, keepdims=True)
    vc = vt - mu
    var = jnp.mean(vc * vc, axis=0, keepdims=True)
    vn = (vc * lax.rsqrt(var + EPS) * lng_ref[...] + lnb_ref[...]).astype(bf16)
    n_slab = TT // MXU_DIM
    heads = []
    for h in range(N_SGU_HEADS):
        rows = vn[SGU_HEAD * h:SGU_HEAD * (h + 1), :]
        lhs = jnp.concatenate([rows[:, MXU_DIM * c:MXU_DIM * (c + 1)] for c in range(n_slab)], axis=0)
        res = jnp.dot(lhs, bd_ref[h], preferred_element_type=f32)
        heads.append(jnp.concatenate([res[SGU_HEAD * c:SGU_HEAD * (c + 1), :] for c in range(n_slab)], axis=1))
    mixed = jnp.concatenate(heads, axis=0) + jnp.concatenate([bst_ref[...]] * (TT // CHUNK), axis=1)
    yst = ut * mixed
    ms = jnp.mean(yst * yst, axis=0, keepdims=True)
    sgu_n = (yst * lax.rsqrt(ms + EPS) * gsgu_ref[...]).T.astype(bf16)

    delta = (jnp.dot(rec_n, wor_ref[...], preferred_element_type=f32)
             + jnp.dot(sgu_n, wos_ref[...], preferred_element_type=f32))
    h1 = x2 + delta
    hx_ref[:, :, 0:D_MODEL] = h1.reshape(NB, ST, D_MODEL)
    hx_ref[:, :, D_MODEL:D_MODEL + D_PLE] = p_ref[...]

    n2 = _rms(h1, gffn_ref[...]).astype(bf16)
    logits = lax.dot_general(wrt_ref[...], n2, (((1,), (1,)), ((), ())), preferred_element_type=f32) + brt_ref[...]
    cls, w_lo, w_hi = _route(logits)
    rt = jnp.concatenate([cls, w_lo, w_hi, jnp.zeros((LANES - 3, TT), f32)], axis=0).T
    hx_ref[:, :, D_MODEL + D_PLE:D_ROW] = rt.reshape(NB, ST, LANES)


def _expert_kernel(elo_ref, ehi_ref, nval_ref, cq_ref, cs_ref, nq_ref, ns_ref, pq_ref, ps_ref, hx_hbm, gffn_ref,
                   w1lo_ref, w3lo_ref, w2lo_ref, w1hi_ref, w3hi_ref, w2hi_ref,
                   gple_ref, wpg_ref, wpp_ref, gfin_ref, y_hbm, xbuf0, xbuf1, ybuf0, ybuf1, gsem, ssem):
    i = pl.program_id(0)
    n = pl.num_programs(0)
    cnt = nval_ref[i]
    nxt = jnp.minimum(i + 1, n - 1)
    n_grp = TM // SUBLANES

    def gather_row(q_ref, s_ref, buf, par, k, u):
        r = k * SUBLANES + u
        return pltpu.make_async_copy(hx_hbm.at[q_ref[0, 0, r], pl.ds(s_ref[0, 0, r], 1)],
                                     buf.at[k, pl.ds(u, 1)], gsem.at[par])

    def scatter_row(q_ref, s_ref, buf, par, k, u):
        r = k * SUBLANES + u
        return pltpu.make_async_copy(buf.at[k, pl.ds(u, 1)],
                                     y_hbm.at[q_ref[0, 0, r], pl.ds(s_ref[0, 0, r], 1)], ssem.at[par])

    def wait_tile_in(buf, par):
        pltpu.make_async_copy(hx_hbm.at[pl.ds(0, n_grp)], buf, gsem.at[par]).wait()

    def wait_tile_out(buf, par):
        pltpu.make_async_copy(buf, y_hbm.at[pl.ds(0, n_grp)], ssem.at[par]).wait()

    def compute(xb, yb):
        xrow = xb[...].reshape(TM, D_ROW)
        xs = xrow[:, 0:D_MODEL]
        n2 = _rms(xs, gffn_ref[...]).astype(bf16)

        def expert(w1_ref, w3_ref, w2_ref):
            a = jnp.dot(n2, w1_ref[0], preferred_element_type=f32)
            b = jnp.dot(n2, w3_ref[0], preferred_element_type=f32)
            hdn = (jax.nn.silu(a) * b).astype(bf16)
            return jnp.dot(hdn, w2_ref[0], preferred_element_type=f32)

        ws = xrow[:, D_MODEL + D_PLE:D_ROW]
        out = ws[:, 1:2] * expert(w1lo_ref, w3lo_ref, w2lo_ref)
        out = out + ws[:, 2:3] * expert(w1hi_ref, w3hi_ref, w2hi_ref)
        h2 = xs + out
        n3 = _rms(h2, gple_ref[...]).astype(bf16)
        gate = jax.nn.sigmoid(jnp.dot(n3, wpg_ref[...], preferred_element_type=f32))
        ps = xrow[:, D_MODEL:D_MODEL + D_PLE]
        proj = jnp.dot(ps.astype(bf16), wpp_ref[...], preferred_element_type=f32)
        h3 = h2 + gate * proj
        yb[...] = _rms(h3, gfin_ref[...]).reshape(n_grp, SUBLANES, D_MODEL)

    def step(par):
        xb, xo = (xbuf0, xbuf1) if par == 0 else (xbuf1, xbuf0)
        yb, yo = (ybuf0, ybuf1) if par == 0 else (ybuf1, ybuf0)

        @pl.when(i == 0)
        def _():
            def body(k, c):
                for u in range(SUBLANES):
                    gather_row(cq_ref, cs_ref, xb, par, k, u).start(priority=u % 2)
                return c
            lax.fori_loop(0, n_grp, body, 0)
            yo[...] = jnp.zeros_like(yo)

        wait_tile_in(xb, par)

        @pl.when(i >= 1)
        def _():
            wait_tile_out(yb, par)

        for k in range(n_grp):
            for u in range(SUBLANES):
                gather_row(nq_ref, ns_ref, xo, 1 - par, k, u).start(priority=u % 2)
                scatter_row(pq_ref, ps_ref, yo, 1 - par, k, u).start(priority=u % 2)
        compute(xb, yb)

        @pl.when((i == n - 1) | (nval_ref[nxt] == 0))
        def _():
            wait_tile_in(xo, 1 - par)
            wait_tile_out(yo, 1 - par)

            def send(r, c):
                scatter_row(cq_ref, cs_ref, yb, par, r // SUBLANES, r % SUBLANES).start()
                return c
            lax.fori_loop(0, cnt, send, 0)

            def drain(r, c):
                scatter_row(cq_ref, cs_ref, yb, par, 0, 0).wait()
                return c
            lax.fori_loop(0, cnt, drain, 0)

    @pl.when(cnt > 0)
    def _():
        @pl.when(i % 2 == 0)
        def _():
            step(0)

        @pl.when(i % 2 == 1)
        def _():
            step(1)


def _const_spec(shape):
    nd = len(shape)
    return pl.BlockSpec(shape, lambda *_: (0,) * nd)


def _gate_blocks(w_a, w_x):
    def pair(w):
        z = jnp.zeros((REC_HEAD, REC_HEAD), w.dtype)
        return jnp.stack([jnp.block([[w[2 * g], z], [z, w[2 * g + 1]]]) for g in range(N_LANE_GROUPS)])
    return jnp.concatenate([pair(w_a), pair(w_x)], axis=-1).astype(bf16)


def _mix_trunk(x, p, P):
    bsz, seq, _ = x.shape
    assert bsz % NB == 0 and seq % ST == 0
    n_g, n_t = bsz // NB, seq // ST
    halo_blocks = seq // HALO
    row = lambda v: v.reshape(1, -1)

    x_spec_b = pl.BlockSpec((NB, ST, D_MODEL), lambda g, j: (g, n_t - 1 - j, 0))
    xp_spec = pl.BlockSpec((NB, HALO, D_MODEL),
                           lambda g, j: (g, jnp.maximum((n_t - 1 - j) * (ST // HALO) - 1, 0), 0))
    xn_spec = pl.BlockSpec((NB, HALO, D_MODEL),
                           lambda g, j: (g, jnp.minimum((n_t - j) * (ST // HALO), halo_blocks - 1), 0))
    rec_spec_b = pl.BlockSpec((NB, ST, D_REC), lambda g, j: (g, n_t - 1 - j, 0))
    scan_scratch = [pltpu.VMEM((N_LANE_GROUPS, NB * PITCH, LANES), f32),
                    pltpu.VMEM((N_LANE_GROUPS, NB * PITCH, LANES), f32),
                    pltpu.VMEM((NB, D_REC), f32)]

    bwd_w = [row(P['g_mix']), P['w_xr'], P['conv_w'], row(P['conv_b']), P['wg_b'],
             row(P['ba_b']), row(P['bx_b']), row(P['lam_b'])]
    xc, hb = pl.pallas_call(
        _bwd_kernel,
        grid=(n_g, n_t),
        in_specs=[x_spec_b, xp_spec, xn_spec] + [_const_spec(w.shape) for w in bwd_w],
        out_specs=[rec_spec_b, rec_spec_b],
        out_shape=[jax.ShapeDtypeStruct((bsz, seq, D_REC), f32)] * 2,
        scratch_shapes=[pltpu.VMEM((NB, ST + 2 * HALO, D_REC), f32)] + scan_scratch,
        compiler_params=pltpu.CompilerParams(dimension_semantics=("arbitrary", "arbitrary"),
                                             vmem_limit_bytes=VMEM_LIMIT),
        name="bwd_scan",
    )(x, x, x, *bwd_w)

    tile_spec = lambda d: pl.BlockSpec((NB, ST, d), lambda g, j: (g, j, 0))
    fwd_w = [row(P['g_mix']), P['w_gr'], P['w_uvt'], P['wg_f'], row(P['ba_f']), row(P['bx_f']), row(P['lam_f']),
             P['ln_g'], P['ln_b'], P['bd'], P['bst'], row(P['g_rec']), P['g_sgu'], P['w_or'], P['w_os'],
             row(P['g_ffn']), P['w_rt'], P['b_rt']]
    hx = pl.pallas_call(
        _fwd_kernel,
        grid=(n_g, n_t),
        in_specs=[tile_spec(D_MODEL), tile_spec(D_REC), tile_spec(D_REC), tile_spec(D_PLE)]
                 + [_const_spec(w.shape) for w in fwd_w],
        out_specs=tile_spec(D_ROW),
        out_shape=jax.ShapeDtypeStruct((bsz, seq, D_ROW), f32),
        scratch_shapes=scan_scratch,
        compiler_params=pltpu.CompilerParams(dimension_semantics=("arbitrary", "arbitrary"),
                                             vmem_limit_bytes=VMEM_LIMIT),
        name="fwd_mix",
    )(x, xc, hb, p, *fwd_w)
    return hx.reshape(bsz * seq, D_ROW)


def _expert_stage(hx, P):
    n_tok = hx.shape[0]
    assert n_tok % TM == 0
    n_tiles = n_tok // TM + N_CLASSES
    assert n_tok // TM > N_CLASSES
    i32 = jnp.int32
    cls = hx[:, D_MODEL + D_PLE].astype(i32)

    class_ids = jnp.arange(N_CLASSES, dtype=i32)
    counts = jnp.sum((cls[:, None] == class_ids[None, :]).astype(i32), axis=0)
    order = jnp.argsort(cls, stable=True).astype(i32)
    dense_start = jnp.cumsum(counts) - counts
    full_c, rem_c = counts // TM, counts % TM
    cum_part = jnp.cumsum((rem_c > 0).astype(i32))
    cum_full = jnp.cumsum(full_c)
    n_part, n_full = cum_part[-1], cum_full[-1]
    tile_ids = jnp.arange(n_tiles, dtype=i32)
    is_part = tile_ids < n_part
    active = tile_ids < n_part + n_full
    first_reaching = lambda cum, k: jnp.minimum(
        jnp.sum((cum[None, :] <= k[:, None]).astype(i32), axis=1), N_CLASSES - 1)
    cls_part = first_reaching(cum_part, tile_ids)
    full_id = jnp.maximum(tile_ids - n_part, 0)
    cls_full = first_reaching(cum_full, full_id)
    tile_cls = jnp.where(is_part, cls_part, cls_full)
    n_valid = jnp.where(active, jnp.where(is_part, rem_c[tile_cls], TM), 0)
    in_class = jnp.where(is_part, full_c[tile_cls], full_id - (cum_full - full_c)[tile_cls]) * TM
    tile_dense = dense_start[tile_cls] + in_class
    rank_end = jnp.cumsum(n_valid)
    rank_start = rank_end - n_valid
    ranks = jnp.arange(n_tok, dtype=i32)
    in_tile = (rank_start[None, :] <= ranks[:, None]) & (ranks[:, None] < rank_end[None, :])
    shift = jnp.sum(jnp.where(in_tile, (tile_dense - rank_start)[None, :], 0), axis=1)
    tok_by_rank = order[ranks + shift]
    look = jnp.minimum(rank_start[:, None] + jnp.arange(TM, dtype=i32)[None, :], n_tok - 1)
    src = tok_by_rank[look].reshape(n_tiles, 1, TM)
    src_q, src_s = src // SUBLANES, src % SUBLANES

    pair_lo = jnp.array([0, 0, 0, 1, 1, 2], jnp.int32)
    pair_hi = jnp.array([1, 2, 3, 2, 3, 3], jnp.int32)
    e_lo = (tile_cls // N_PAIRS) * N_EXP_PER_GROUP + pair_lo[tile_cls % N_PAIRS]
    e_hi = (tile_cls // N_PAIRS) * N_EXP_PER_GROUP + pair_hi[tile_cls % N_PAIRS]

    row = lambda v: v.reshape(1, -1)
    idx_spec = lambda f: pl.BlockSpec((1, 1, TM), f, memory_space=pltpu.SMEM)
    any_spec = pl.BlockSpec(memory_space=pl.ANY)
    lo_spec = lambda s: pl.BlockSpec((1,) + s, lambda i, elo, ehi, nv: (elo[i], 0, 0))
    hi_spec = lambda s: pl.BlockSpec((1,) + s, lambda i, elo, ehi, nv: (ehi[i], 0, 0))
    up_shape, down_shape = (D_MODEL, D_EXPERT), (D_EXPERT, D_MODEL)
    consts = [row(P['g_ple']), P['w_pg'], P['w_pp'], row(P['g_final'])]
    this_tile = lambda i, *_: (i, 0, 0)
    next_tile = lambda i, *_: (jnp.minimum(i + 1, n_tiles - 1), 0, 0)
    prev_tile = lambda i, *_: (jnp.maximum(i - 1, 0), 0, 0)
    tile_buf = lambda d: pltpu.VMEM((TM // SUBLANES, SUBLANES, d), f32)
    y = pl.pallas_call(
        _expert_kernel,
        grid_spec=pltpu.PrefetchScalarGridSpec(
            num_scalar_prefetch=3,
            grid=(n_tiles,),
            in_specs=[idx_spec(this_tile), idx_spec(this_tile), idx_spec(next_tile), idx_spec(next_tile),
                      idx_spec(prev_tile), idx_spec(prev_tile),
                      any_spec, _const_spec((1, D_MODEL)),
                      lo_spec(up_shape), lo_spec(up_shape), lo_spec(down_shape),
                      hi_spec(up_shape), hi_spec(up_shape), hi_spec(down_shape)]
                     + [_const_spec(w.shape) for w in consts],
            out_specs=any_spec,
            scratch_shapes=[tile_buf(D_ROW), tile_buf(D_ROW), tile_buf(D_MODEL), tile_buf(D_MODEL),
                            pltpu.SemaphoreType.DMA((2,)), pltpu.SemaphoreType.DMA((2,))],
        ),
        out_shape=jax.ShapeDtypeStruct((n_tok // SUBLANES, SUBLANES, D_MODEL), f32),
        compiler_params=pltpu.CompilerParams(dimension_semantics=("arbitrary",), vmem_limit_bytes=VMEM_LIMIT),
        name="experts",
    )(e_lo, e_hi, n_valid.astype(i32), src_q, src_s, src_q, src_s, src_q, src_s,
      hx.reshape(n_tok // SUBLANES, SUBLANES, D_ROW), row(P['g_ffn']),
      P['w1'], P['w3'], P['w2'], P['w1'], P['w3'], P['w2'], *consts)
    return y.reshape(n_tok, D_MODEL)


def _prepare(g_mix, w_in, conv_w, conv_b, lru_w_a, lru_b_a, lru_w_x, lru_b_x, lru_lambda, sgu_ln_g, sgu_ln_b,
             sgu_w_s, sgu_b_s, g_rec_out, g_sgu_out, w_out, g_ffn, w_router_group, b_router_group,
             w_router_expert, b_router_expert, w_exp_gate, w_exp_up, w_exp_down, g_ple, w_ple_gate, w_ple_proj,
             g_final):
    l = 0
    w = w_in[l]
    ws_t = jnp.transpose(sgu_w_s[l], (0, 2, 1))
    z = jnp.zeros((CHUNK, CHUNK), f32)
    bd = jnp.stack([jnp.block([[ws_t[h], z], [z, ws_t[h]]]) for h in range(N_SGU_HEADS)]).astype(bf16)
    w_rt = jnp.concatenate([w_router_group[l].T, w_router_expert[l].T,
                            jnp.zeros((ROUTE_ROWS - N_GROUPS - N_EXPERTS, D_MODEL), f32)], axis=0).astype(bf16)
    b_rt = jnp.concatenate([b_router_group[l], b_router_expert[l],
                            jnp.zeros((ROUTE_ROWS - N_GROUPS - N_EXPERTS,), f32)]).reshape(ROUTE_ROWS, 1)
    return dict(
        g_mix=g_mix[l], w_xr=w[:, :D_REC].astype(bf16), w_gr=w[:, D_REC:2 * D_REC].astype(bf16),
        w_uvt=w[:, 2 * D_REC:].T.astype(bf16),
        conv_w=conv_w[l], conv_b=conv_b[l],
        wg_f=_gate_blocks(lru_w_a[l, 0], lru_w_x[l, 0]), wg_b=_gate_blocks(lru_w_a[l, 1], lru_w_x[l, 1]),
        ba_f=lru_b_a[l, 0], ba_b=lru_b_a[l, 1], bx_f=lru_b_x[l, 0], bx_b=lru_b_x[l, 1],
        lam_f=lru_lambda[l, 0], lam_b=lru_lambda[l, 1],
        ln_g=sgu_ln_g[l].reshape(D_SGU, 1), ln_b=sgu_ln_b[l].reshape(D_SGU, 1),
        bd=bd, bst=jnp.repeat(sgu_b_s[l], SGU_HEAD, axis=0),
        g_rec=g_rec_out[l], g_sgu=g_sgu_out[l].reshape(D_SGU, 1),
        w_or=w_out[l, :D_REC].astype(bf16), w_os=w_out[l, D_REC:].astype(bf16),
        g_ffn=g_ffn[l], w_rt=w_rt, b_rt=b_rt,
        w1=w_exp_gate[l].astype(bf16), w3=w_exp_up[l].astype(bf16), w2=w_exp_down[l].astype(bf16),
        g_ple=g_ple[l], w_pg=w_ple_gate[l].astype(bf16), w_pp=w_ple_proj[l].astype(bf16), g_final=g_final,
    )


def kernel(x_prompt, x_sample, p_prompt, p_sample, g_mix, w_in, conv_w, conv_b, lru_w_a, lru_b_a, lru_w_x, lru_b_x, lru_lambda, sgu_ln_g, sgu_ln_b, sgu_w_s, sgu_b_s, g_rec_out, g_sgu_out, w_out, g_ffn, w_router_group, b_router_group, w_router_expert, b_router_expert, w_exp_gate, w_exp_up, w_exp_down, g_ple, w_ple_gate, w_ple_proj, g_final):
    assert w_in.shape[0] == 1, "single-layer trunk"
    P = _prepare(g_mix, w_in, conv_w, conv_b, lru_w_a, lru_b_a, lru_w_x, lru_b_x, lru_lambda, sgu_ln_g, sgu_ln_b,
                 sgu_w_s, sgu_b_s, g_rec_out, g_sgu_out, w_out, g_ffn, w_router_group, b_router_group,
                 w_router_expert, b_router_expert, w_exp_gate, w_exp_up, w_exp_down, g_ple, w_ple_gate,
                 w_ple_proj, g_final)
    outs = []
    for x, p in ((x_prompt, p_prompt), (x_sample, p_sample)):
        y = _expert_stage(_mix_trunk(x, p[0], P), P)
        outs.append(y.reshape(x.shape))
    return tuple(outs)
```

```python
import functools

import jax
import jax.numpy as jnp
from jax import lax
from jax.experimental import pallas as pl
from jax.experimental.pallas import tpu as pltpu

f32 = jnp.float32
bf16 = jnp.bfloat16

D_MODEL = 1024
D_REC = 512
N_REC_HEADS = 8
REC_HEAD = 64
CONV_W = 4
CONV_LEFT = 2
LRU_C = 8.0
D_SGU = 512
N_SGU_HEADS = 8
SGU_HEAD = 64
CHUNK = 128
N_GROUPS = 4
N_EXP_PER_GROUP = 4
N_EXPERTS = 16
D_EXPERT = 512
D_PLE = 256
EPS = 1e-6

LANES = 128
SUBLANES = 8
NB = SUBLANES
ST = CHUNK
TT = NB * ST
HALO = SUBLANES
PITCH = ST + SUBLANES
N_LANE_GROUPS = D_REC // LANES
MXU_DIM = 256
TM = 512
N_PAIRS = 6
N_CLASSES = N_GROUPS * N_PAIRS
D_ROW = D_MODEL + D_PLE + LANES
ROUTE_ROWS = 32
VMEM_LIMIT = 60 * 1024 * 1024


def _rms(x, g):
    ms = jnp.mean(x * x, axis=-1, keepdims=True)
    return x * lax.rsqrt(ms + EPS) * g


def _softplus(x):
    return jnp.maximum(x, 0.0) + jnp.log1p(jnp.exp(-jnp.abs(x)))


def _gate_ab(xc, wg_ref, ba, bx, lam):
    xcb = xc.astype(bf16)
    sp = _softplus(-lam)
    out = []
    for g in range(N_LANE_GROUPS):
        sl = slice(LANES * g, LANES * (g + 1))
        pre = jnp.dot(xcb[:, sl], wg_ref[g], preferred_element_type=f32)
        r = jax.nn.sigmoid(pre[:, :LANES] + ba[:, sl])
        i = jax.nn.sigmoid(pre[:, LANES:] + bx[:, sl])
        log_a = -LRU_C * r * sp[:, sl]
        a = jnp.exp(log_a)
        z = jnp.tanh(-log_a) * (1.0 + a * a)
        mult = jnp.where(z > 0.0, z * lax.rsqrt(z), 0.0)
        out.append((a, mult * (i * xc[:, sl])))
    return out


def _store_ab(ab, a_ref, b_ref):
    for g, (a, b) in enumerate(ab):
        for s in range(NB):
            a_ref[g, s * PITCH:s * PITCH + ST, :] = a[s * ST:(s + 1) * ST, :]
            b_ref[g, s * PITCH:s * PITCH + ST, :] = b[s * ST:(s + 1) * ST, :]


def _scan(a_ref, b_ref, h_ref, carry_ref, reverse):
    def step(k, hs):
        t = (ST - 1 - k) if reverse else k
        new = []
        for g in range(N_LANE_GROUPS):
            a = a_ref[g, pl.ds(t, NB, stride=PITCH), :]
            b = b_ref[g, pl.ds(t, NB, stride=PITCH), :]
            h = a * hs[g] + b
            h_ref[g, pl.ds(t, NB, stride=PITCH), :] = h
            new.append(h)
        return tuple(new)

    init = tuple(carry_ref[:, LANES * g:LANES * (g + 1)] for g in range(N_LANE_GROUPS))
    hs = lax.fori_loop(0, ST, step, init, unroll=8)
    for g in range(N_LANE_GROUPS):
        carry_ref[:, LANES * g:LANES * (g + 1)] = hs[g]


def _bwd_kernel(x_ref, xp_ref, xn_ref, gmix_ref, wxr_ref, cw_ref, cb_ref, wg_ref, ba_ref, bx_ref, lam_ref,
                xc_ref, hb_ref, ext_ref, a_ref, b_ref, h_ref, carry_ref):
    j = pl.program_id(1)
    n_t = pl.num_programs(1)
    tb = n_t - 1 - j

    @pl.when(j == 0)
    def _():
        carry_ref[...] = jnp.zeros_like(carry_ref)

    gm = gmix_ref[...]

    def proj(xv):
        return jnp.dot(_rms(xv, gm).astype(bf16), wxr_ref[...], preferred_element_type=f32)

    xr = proj(x_ref[...].reshape(TT, D_MODEL))
    ext_ref[:, HALO:HALO + ST, :] = xr.reshape(NB, ST, D_REC)
    hp = proj(xp_ref[...].reshape(NB * HALO, D_MODEL)).reshape(NB, HALO, D_REC)
    ext_ref[:, 0:HALO, :] = jnp.where(tb > 0, hp, 0.0)
    hn = proj(xn_ref[...].reshape(NB * HALO, D_MODEL)).reshape(NB, HALO, D_REC)
    ext_ref[:, HALO + ST:2 * HALO + ST, :] = jnp.where(tb < n_t - 1, hn, 0.0)

    cw = cw_ref[...]
    xc = None
    for k in range(CONV_W):
        off = HALO - CONV_LEFT + k
        term = ext_ref[:, off:off + ST, :] * cw[k:k + 1, :]
        xc = term if xc is None else xc + term
    xc = xc + cb_ref[...]
    xc_ref[...] = xc

    ab = _gate_ab(xc.reshape(TT, D_REC), wg_ref, ba_ref[...], bx_ref[...], lam_ref[...])
    _store_ab(ab, a_ref, b_ref)
    _scan(a_ref, b_ref, h_ref, carry_ref, reverse=True)
    for g in range(N_LANE_GROUPS):
        for s in range(NB):
            hb_ref[s, :, LANES * g:LANES * (g + 1)] = h_ref[g, s * PITCH:s * PITCH + ST, :]


def _route(logits):
    gl = [logits[i:i + 1, :] for i in range(N_GROUPS)]
    gm = jnp.maximum(jnp.maximum(gl[0], gl[1]), jnp.maximum(gl[2], gl[3]))
    sg = sum(jnp.exp(v - gm) for v in gl)
    g_top = 1.0 / sg
    gidx = jnp.where(gl[0] == gm, 0, jnp.where(gl[1] == gm, 1, jnp.where(gl[2] == gm, 2, 3)))
    el = []
    for k in range(N_EXP_PER_GROUP):
        rows = [logits[N_GROUPS + N_EXP_PER_GROUP * g + k:N_GROUPS + N_EXP_PER_GROUP * g + k + 1, :]
                for g in range(N_GROUPS)]
        el.append(jnp.where(gidx == 0, rows[0], jnp.where(gidx == 1, rows[1], jnp.where(gidx == 2, rows[2], rows[3]))))
    em = jnp.maximum(jnp.maximum(el[0], el[1]), jnp.maximum(el[2], el[3]))
    ee = [jnp.exp(v - em) for v in el]
    se = ee[0] + ee[1] + ee[2] + ee[3]
    p = [v / se for v in ee]
    p1 = jnp.maximum(jnp.maximum(p[0], p[1]), jnp.maximum(p[2], p[3]))
    i1 = jnp.where(p[0] == p1, 0, jnp.where(p[1] == p1, 1, jnp.where(p[2] == p1, 2, 3)))
    q = [jnp.where(i1 == k, -1.0, p[k]) for k in range(N_EXP_PER_GROUP)]
    p2 = jnp.maximum(jnp.maximum(q[0], q[1]), jnp.maximum(q[2], q[3]))
    i2 = jnp.where(q[0] == p2, 0, jnp.where(q[1] == p2, 1, jnp.where(q[2] == p2, 2, 3)))
    norm = p1 + p2
    w1 = g_top * (p1 / norm)
    w2 = g_top * (p2 / norm)
    first_is_lo = i1 < i2
    lo = jnp.where(first_is_lo, i1, i2)
    hi = jnp.where(first_is_lo, i2, i1)
    w_lo = jnp.where(first_is_lo, w1, w2)
    w_hi = jnp.where(first_is_lo, w2, w1)
    base = jnp.where(lo == 0, 0, jnp.where(lo == 1, 3, 5))
    cls = gidx * N_PAIRS + base + hi - lo - 1
    return cls.astype(f32), w_lo, w_hi


def _fwd_kernel(x_ref, xc_ref, hb_ref, p_ref, gmix_ref, wgr_ref, wuvt_ref, wg_ref, ba_ref, bx_ref, lam_ref,
                lng_ref, lnb_ref, bd_ref, bst_ref, grec_ref, gsgu_ref, wor_ref, wos_ref, gffn_ref, wrt_ref, brt_ref,
                hx_ref, a_ref, b_ref, h_ref, carry_ref):
    j = pl.program_id(1)

    @pl.when(j == 0)
    def _():
        carry_ref[...] = jnp.zeros_like(carry_ref)

    x2 = x_ref[...].reshape(TT, D_MODEL)
    n = _rms(x2, gmix_ref[...]).astype(bf16)

    ab = _gate_ab(xc_ref[...].reshape(TT, D_REC), wg_ref, ba_ref[...], bx_ref[...], lam_ref[...])
    _store_ab(ab, a_ref, b_ref)
    _scan(a_ref, b_ref, h_ref, carry_ref, reverse=False)
    hf = jnp.concatenate(
        [jnp.concatenate([h_ref[g, s * PITCH:s * PITCH + ST, :] for s in range(NB)], axis=0)
         for g in range(N_LANE_GROUPS)], axis=1)
    gr = jnp.dot(n, wgr_ref[...], preferred_element_type=f32)
    y_rec = (hf + hb_ref[...].reshape(TT, D_REC)) * jax.nn.gelu(gr)
    rec_n = _rms(y_rec, grec_ref[...]).astype(bf16)

    zt = lax.dot_general(wuvt_ref[...], n, (((1,), (1,)), ((), ())), preferred_element_type=f32)
    ut = jax.nn.gelu(zt[:D_SGU, :])
    vt = jax.nn.gelu(zt[D_SGU:, :])
    mu = jnp.mean(vt, axis=0, keepdims=True)
    vc = vt - mu
    var = jnp.mean(vc * vc, axis=0, keepdims=True)
    vn = (vc * lax.rsqrt(var + EPS) * lng_ref[...] + lnb_ref[...]).astype(bf16)
    n_slab = TT // MXU_DIM
    heads = []
    for h in range(N_SGU_HEADS):
        rows = vn[SGU_HEAD * h:SGU_HEAD * (h + 1), :]
        lhs = jnp.concatenate([rows[:, MXU_DIM * c:MXU_DIM * (c + 1)] for c in range(n_slab)], axis=0)
        res = jnp.dot(lhs, bd_ref[h], preferred_element_type=f32)
        heads.append(jnp.concatenate([res[SGU_HEAD * c:SGU_HEAD * (c + 1), :] for c in range(n_slab)], axis=1))
    mixed = jnp.concatenate(heads, axis=0) + jnp.concatenate([bst_ref[...]] * (TT // CHUNK), axis=1)
    yst = ut * mixed
    ms = jnp.mean(yst * yst, axis=0, keepdims=True)
    sgu_n = (yst * lax.rsqrt(ms + EPS) * gsgu_ref[...]).T.astype(bf16)

    delta = (jnp.dot(rec_n, wor_ref[...], preferred_element_type=f32)
             + jnp.dot(sgu_n, wos_ref[...], preferred_element_type=f32))
    h1 = x2 + delta
    hx_ref[:, :, 0:D_MODEL] = h1.reshape(NB, ST, D_MODEL)
    hx_ref[:, :, D_MODEL:D_MODEL + D_PLE] = p_ref[...]

    n2 = _rms(h1, gffn_ref[...]).astype(bf16)
    logits = lax.dot_general(wrt_ref[...], n2, (((1,), (1,)), ((), ())), preferred_element_type=f32) + brt_ref[...]
    cls, w_lo, w_hi = _route(logits)
    rt = jnp.concatenate([cls, w_lo, w_hi, jnp.zeros((LANES - 3, TT), f32)], axis=0).T
    hx_ref[:, :, D_MODEL + D_PLE:D_ROW] = rt.reshape(NB, ST, LANES)


def _expert_kernel(elo_ref, ehi_ref, nval_ref, cq_ref, cs_ref, nq_ref, ns_ref, pq_ref, ps_ref, hx_hbm, gffn_ref,
                   w1lo_ref, w3lo_ref, w2lo_ref, w1hi_ref, w3hi_ref, w2hi_ref,
                   gple_ref, wpg_ref, wpp_ref, gfin_ref, y_hbm, xbuf0, xbuf1, ybuf0, ybuf1, gsem, ssem):
    i = pl.program_id(0)
    n = pl.num_programs(0)
    cnt = nval_ref[i]
    nxt = jnp.minimum(i + 1, n - 1)
    n_grp = TM // SUBLANES

    def gather_row(q_ref, s_ref, buf, par, k, u):
        r = k * SUBLANES + u
        return pltpu.make_async_copy(hx_hbm.at[q_ref[0, 0, r], pl.ds(s_ref[0, 0, r], 1)],
                                     buf.at[k, pl.ds(u, 1)], gsem.at[par])

    def scatter_row(q_ref, s_ref, buf, par, k, u):
        r = k * SUBLANES + u
        return pltpu.make_async_copy(buf.at[k, pl.ds(u, 1)],
                                     y_hbm.at[q_ref[0, 0, r], pl.ds(s_ref[0, 0, r], 1)], ssem.at[par])

    def wait_tile_in(buf, par):
        pltpu.make_async_copy(hx_hbm.at[pl.ds(0, n_grp)], buf, gsem.at[par]).wait()

    def wait_tile_out(buf, par):
        pltpu.make_async_copy(buf, y_hbm.at[pl.ds(0, n_grp)], ssem.at[par]).wait()

    def compute(xb, yb):
        xrow = xb[...].reshape(TM, D_ROW)
        xs = xrow[:, 0:D_MODEL]
        n2 = _rms(xs, gffn_ref[...]).astype(bf16)

        def expert(w1_ref, w3_ref, w2_ref):
            a = jnp.dot(n2, w1_ref[0], preferred_element_type=f32)
            b = jnp.dot(n2, w3_ref[0], preferred_element_type=f32)
            hdn = (jax.nn.silu(a) * b).astype(bf16)
            return jnp.dot(hdn, w2_ref[0], preferred_element_type=f32)

        ws = xrow[:, D_MODEL + D_PLE:D_ROW]
        out = ws[:, 1:2] * expert(w1lo_ref, w3lo_ref, w2lo_ref)
        out = out + ws[:, 2:3] * expert(w1hi_ref, w3hi_ref, w2hi_ref)
        h2 = xs + out
        n3 = _rms(h2, gple_ref[...]).astype(bf16)
        gate = jax.nn.sigmoid(jnp.dot(n3, wpg_ref[...], preferred_element_type=f32))
        ps = xrow[:, D_MODEL:D_MODEL + D_PLE]
        proj = jnp.dot(ps.astype(bf16), wpp_ref[...], preferred_element_type=f32)
        h3 = h2 + gate * proj
        yb[...] = _rms(h3, gfin_ref[...]).reshape(n_grp, SUBLANES, D_MODEL)

    def step(par):
        xb, xo = (xbuf0, xbuf1) if par == 0 else (xbuf1, xbuf0)
        yb, yo = (ybuf0, ybuf1) if par == 0 else (ybuf1, ybuf0)

        @pl.when(i == 0)
        def _():
            def body(k, c):
                for u in range(SUBLANES):
                    gather_row(cq_ref, cs_ref, xb, par, k, u).start(priority=u % 2)
                return c
            lax.fori_loop(0, n_grp, body, 0)
            yo[...] = jnp.zeros_like(yo)

        wait_tile_in(xb, par)

        @pl.when(i >= 1)
        def _():
            wait_tile_out(yb, par)

        for k in range(n_grp):
            for u in range(SUBLANES):
                gather_row(nq_ref, ns_ref, xo, 1 - par, k, u).start(priority=u % 2)
                scatter_row(pq_ref, ps_ref, yo, 1 - par, k, u).start(priority=u % 2)
        compute(xb, yb)

        @pl.when((i == n - 1) | (nval_ref[nxt] == 0))
        def _():
            wait_tile_in(xo, 1 - par)
            wait_tile_out(yo, 1 - par)

            def send(r, c):
                scatter_row(cq_ref, cs_ref, yb, par, r // SUBLANES, r % SUBLANES).start()
                return c
            lax.fori_loop(0, cnt, send, 0)

            def drain(r, c):
                scatter_row(cq_ref, cs_ref, yb, par, 0, 0).wait()
                return c
            lax.fori_loop(0, cnt, drain, 0)

    @pl.when(cnt > 0)
    def _():
        @pl.when(i % 2 == 0)
        def _():
            step(0)

        @pl.when(i % 2 == 1)
        def _():
            step(1)


def _const_spec(shape):
    nd = len(shape)
    return pl.BlockSpec(shape, lambda *_: (0,) * nd)


def _gate_blocks(w_a, w_x):
    def pair(w):
        z = jnp.zeros((REC_HEAD, REC_HEAD), w.dtype)
        return jnp.stack([jnp.block([[w[2 * g], z], [z, w[2 * g + 1]]]) for g in range(N_LANE_GROUPS)])
    return jnp.concatenate([pair(w_a), pair(w_x)], axis=-1).astype(bf16)


def _mix_trunk(x, p, P):
    bsz, seq, _ = x.shape
    assert bsz % NB == 0 and seq % ST == 0
    n_g, n_t = bsz // NB, seq // ST
    halo_blocks = seq // HALO
    row = lambda v: v.reshape(1, -1)

    x_spec_b = pl.BlockSpec((NB, ST, D_MODEL), lambda g, j: (g, n_t - 1 - j, 0))
    xp_spec = pl.BlockSpec((NB, HALO, D_MODEL),
                           lambda g, j: (g, jnp.maximum((n_t - 1 - j) * (ST // HALO) - 1, 0), 0))
    xn_spec = pl.BlockSpec((NB, HALO, D_MODEL),
                           lambda g, j: (g, jnp.minimum((n_t - j) * (ST // HALO), halo_blocks - 1), 0))
    rec_spec_b = pl.BlockSpec((NB, ST, D_REC), lambda g, j: (g, n_t - 1 - j, 0))
    scan_scratch = [pltpu.VMEM((N_LANE_GROUPS, NB * PITCH, LANES), f32),
                    pltpu.VMEM((N_LANE_GROUPS, NB * PITCH, LANES), f32),
                    pltpu.VMEM((N_LANE_GROUPS, NB * PITCH, LANES), f32),
                    pltpu.VMEM((NB, D_REC), f32)]

    bwd_w = [row(P['g_mix']), P['w_xr'], P['conv_w'], row(P['conv_b']), P['wg_b'],
             row(P['ba_b']), row(P['bx_b']), row(P['lam_b'])]
    xc, hb = pl.pallas_call(
        _bwd_kernel,
        grid=(n_g, n_t),
        in_specs=[x_spec_b, xp_spec, xn_spec] + [_const_spec(w.shape) for w in bwd_w],
        out_specs=[rec_spec_b, rec_spec_b],
        out_shape=[jax.ShapeDtypeStruct((bsz, seq, D_REC), f32)] * 2,
        scratch_shapes=[pltpu.VMEM((NB, ST + 2 * HALO, D_REC), f32)] + scan_scratch,
        compiler_params=pltpu.CompilerParams(dimension_semantics=("arbitrary", "arbitrary"),
                                             vmem_limit_bytes=VMEM_LIMIT),
        name="bwd_scan",
    )(x, x, x, *bwd_w)

    tile_spec = lambda d: pl.BlockSpec((NB, ST, d), lambda g, j: (g, j, 0))
    fwd_w = [row(P['g_mix']), P['w_gr'], P['w_uvt'], P['wg_f'], row(P['ba_f']), row(P['bx_f']), row(P['lam_f']),
             P['ln_g'], P['ln_b'], P['bd'], P['bst'], row(P['g_rec']), P['g_sgu'], P['w_or'], P['w_os'],
             row(P['g_ffn']), P['w_rt'], P['b_rt']]
    hx = pl.pallas_call(
        _fwd_kernel,
        grid=(n_g, n_t),
        in_specs=[tile_spec(D_MODEL), tile_spec(D_REC), tile_spec(D_REC), tile_spec(D_PLE)]
                 + [_const_spec(w.shape) for w in fwd_w],
        out_specs=tile_spec(D_ROW),
        out_shape=jax.ShapeDtypeStruct((bsz, seq, D_ROW), f32),
        scratch_shapes=scan_scratch,
        compiler_params=pltpu.CompilerParams(dimension_semantics=("arbitrary", "arbitrary"),
                                             vmem_limit_bytes=VMEM_LIMIT),
        name="fwd_mix",
    )(x, xc, hb, p, *fwd_w)
    return hx.reshape(bsz * seq, D_ROW)


def _expert_stage(hx, P):
    n_tok = hx.shape[0]
    assert n_tok % TM == 0
    n_tiles = n_tok // TM + N_CLASSES
    assert n_tok // TM > N_CLASSES
    i32 = jnp.int32
    cls = hx[:, D_MODEL + D_PLE].astype(i32)

    class_ids = jnp.arange(N_CLASSES, dtype=i32)
    counts = jnp.sum((cls[:, None] == class_ids[None, :]).astype(i32), axis=0)
    order = jnp.argsort(cls, stable=True).astype(i32)
    dense_start = jnp.cumsum(counts) - counts
    full_c, rem_c = counts // TM, counts % TM
    cum_part = jnp.cumsum((rem_c > 0).astype(i32))
    cum_full = jnp.cumsum(full_c)
    n_part, n_full = cum_part[-1], cum_full[-1]
    tile_ids = jnp.arange(n_tiles, dtype=i32)
    is_part = tile_ids < n_part
    active = tile_ids < n_part + n_full
    first_reaching = lambda cum, k: jnp.minimum(
        jnp.sum((cum[None, :] <= k[:, None]).astype(i32), axis=1), N_CLASSES - 1)
    cls_part = first_reaching(cum_part, tile_ids)
    full_id = jnp.maximum(tile_ids - n_part, 0)
    cls_full = first_reaching(cum_full, full_id)
    tile_cls = jnp.where(is_part, cls_part, cls_full)
    n_valid = jnp.where(active, jnp.where(is_part, rem_c[tile_cls], TM), 0)
    in_class = jnp.where(is_part, full_c[tile_cls], full_id - (cum_full - full_c)[tile_cls]) * TM
    tile_dense = dense_start[tile_cls] + in_class
    rank_end = jnp.cumsum(n_valid)
    rank_start = rank_end - n_valid
    ranks = jnp.arange(n_tok, dtype=i32)
    in_tile = (rank_start[None, :] <= ranks[:, None]) & (ranks[:, None] < rank_end[None, :])
    shift = jnp.sum(jnp.where(in_tile, (tile_dense - rank_start)[None, :], 0), axis=1)
    tok_by_rank = order[ranks + shift]
    look = jnp.minimum(rank_start[:, None] + jnp.arange(TM, dtype=i32)[None, :], n_tok - 1)
    src = tok_by_rank[look].reshape(n_tiles, 1, TM)
    src_q, src_s = src // SUBLANES, src % SUBLANES

    pair_lo = jnp.array([0, 0, 0, 1, 1, 2], jnp.int32)
    pair_hi = jnp.array([1, 2, 3, 2, 3, 3], jnp.int32)
    e_lo = (tile_cls // N_PAIRS) * N_EXP_PER_GROUP + pair_lo[tile_cls % N_PAIRS]
    e_hi = (tile_cls // N_PAIRS) * N_EXP_PER_GROUP + pair_hi[tile_cls % N_PAIRS]

    row = lambda v: v.reshape(1, -1)
    idx_spec = lambda f: pl.BlockSpec((1, 1, TM), f, memory_space=pltpu.SMEM)
    any_spec = pl.BlockSpec(memory_space=pl.ANY)
    lo_spec = lambda s: pl.BlockSpec((1,) + s, lambda i, elo, ehi, nv: (elo[i], 0, 0))
    hi_spec = lambda s: pl.BlockSpec((1,) + s, lambda i, elo, ehi, nv: (ehi[i], 0, 0))
    up_shape, down_shape = (D_MODEL, D_EXPERT), (D_EXPERT, D_MODEL)
    consts = [row(P['g_ple']), P['w_pg'], P['w_pp'], row(P['g_final'])]
    this_tile = lambda i, *_: (i, 0, 0)
    next_tile = lambda i, *_: (jnp.minimum(i + 1, n_tiles - 1), 0, 0)
    prev_tile = lambda i, *_: (jnp.maximum(i - 1, 0), 0, 0)
    tile_buf = lambda d: pltpu.VMEM((TM // SUBLANES, SUBLANES, d), f32)
    y = pl.pallas_call(
        _expert_kernel,
        grid_spec=pltpu.PrefetchScalarGridSpec(
            num_scalar_prefetch=3,
            grid=(n_tiles,),
            in_specs=[idx_spec(this_tile), idx_spec(this_tile), idx_spec(next_tile), idx_spec(next_tile),
                      idx_spec(prev_tile), idx_spec(prev_tile),
                      any_spec, _const_spec((1, D_MODEL)),
                      lo_spec(up_shape), lo_spec(up_shape), lo_spec(down_shape),
                      hi_spec(up_shape), hi_spec(up_shape), hi_spec(down_shape)]
                     + [_const_spec(w.shape) for w in consts],
            out_specs=any_spec,
            scratch_shapes=[tile_buf(D_ROW), tile_buf(D_ROW), tile_buf(D_MODEL), tile_buf(D_MODEL),
                            pltpu.SemaphoreType.DMA((2,)), pltpu.SemaphoreType.DMA((2,))],
        ),
        out_shape=jax.ShapeDtypeStruct((n_tok // SUBLANES, SUBLANES, D_MODEL), f32),
        compiler_params=pltpu.CompilerParams(dimension_semantics=("arbitrary",), vmem_limit_bytes=VMEM_LIMIT),
        name="experts",
    )(e_lo, e_hi, n_valid.astype(i32), src_q, src_s, src_q, src_s, src_q, src_s,
      hx.reshape(n_tok // SUBLANES, SUBLANES, D_ROW), row(P['g_ffn']),
      P['w1'], P['w3'], P['w2'], P['w1'], P['w3'], P['w2'], *consts)
    return y.reshape(n_tok, D_MODEL)


def _prepare(g_mix, w_in, conv_w, conv_b, lru_w_a, lru_b_a, lru_w_x, lru_b_x, lru_lambda, sgu_ln_g, sgu_ln_b,
             sgu_w_s, sgu_b_s, g_rec_out, g_sgu_out, w_out, g_ffn, w_router_group, b_router_group,
             w_router_expert, b_router_expert, w_exp_gate, w_exp_up, w_exp_down, g_ple, w_ple_gate, w_ple_proj,
             g_final):
    l = 0
    w = w_in[l]
    ws_t = jnp.transpose(sgu_w_s[l], (0, 2, 1))
    z = jnp.zeros((CHUNK, CHUNK), f32)
    bd = jnp.stack([jnp.block([[ws_t[h], z], [z, ws_t[h]]]) for h in range(N_SGU_HEADS)]).astype(bf16)
    w_rt = jnp.concatenate([w_router_group[l].T, w_router_expert[l].T,
                            jnp.zeros((ROUTE_ROWS - N_GROUPS - N_EXPERTS, D_MODEL), f32)], axis=0).astype(bf16)
    b_rt = jnp.concatenate([b_router_group[l], b_router_expert[l],
                            jnp.zeros((ROUTE_ROWS - N_GROUPS - N_EXPERTS,), f32)]).reshape(ROUTE_ROWS, 1)
    return dict(
        g_mix=g_mix[l], w_xr=w[:, :D_REC].astype(bf16), w_gr=w[:, D_REC:2 * D_REC].astype(bf16),
        w_uvt=w[:, 2 * D_REC:].T.astype(bf16),
        conv_w=conv_w[l], conv_b=conv_b[l],
        wg_f=_gate_blocks(lru_w_a[l, 0], lru_w_x[l, 0]), wg_b=_gate_blocks(lru_w_a[l, 1], lru_w_x[l, 1]),
        ba_f=lru_b_a[l, 0], ba_b=lru_b_a[l, 1], bx_f=lru_b_x[l, 0], bx_b=lru_b_x[l, 1],
        lam_f=lru_lambda[l, 0], lam_b=lru_lambda[l, 1],
        ln_g=sgu_ln_g[l].reshape(D_SGU, 1), ln_b=sgu_ln_b[l].reshape(D_SGU, 1),
        bd=bd, bst=jnp.repeat(sgu_b_s[l], SGU_HEAD, axis=0),
        g_rec=g_rec_out[l], g_sgu=g_sgu_out[l].reshape(D_SGU, 1),
        w_or=w_out[l, :D_REC].astype(bf16), w_os=w_out[l, D_REC:].astype(bf16),
        g_ffn=g_ffn[l], w_rt=w_rt, b_rt=b_rt,
        w1=w_exp_gate[l].astype(bf16), w3=w_exp_up[l].astype(bf16), w2=w_exp_down[l].astype(bf16),
        g_ple=g_ple[l], w_pg=w_ple_gate[l].astype(bf16), w_pp=w_ple_proj[l].astype(bf16), g_final=g_final,
    )


def kernel(x_prompt, x_sample, p_prompt, p_sample, g_mix, w_in, conv_w, conv_b, lru_w_a, lru_b_a, lru_w_x, lru_b_x, lru_lambda, sgu_ln_g, sgu_ln_b, sgu_w_s, sgu_b_s, g_rec_out, g_sgu_out, w_out, g_ffn, w_router_group, b_router_group, w_router_expert, b_router_expert, w_exp_gate, w_exp_up, w_exp_down, g_ple, w_ple_gate, w_ple_proj, g_final):
    assert w_in.shape[0] == 1, "single-layer trunk"
    P = _prepare(g_mix, w_in, conv_w, conv_b, lru_w_a, lru_b_a, lru_w_x, lru_b_x, lru_lambda, sgu_ln_g, sgu_ln_b,
                 sgu_w_s, sgu_b_s, g_rec_out, g_sgu_out, w_out, g_ffn, w_router_group, b_router_group,
                 w_router_expert, b_router_expert, w_exp_gate, w_exp_up, w_exp_down, g_ple, w_ple_gate,
                 w_ple_proj, g_final)
    outs = []
    for x, p in ((x_prompt, p_prompt), (x_sample, p_sample)):
        y = _expert_stage(_mix_trunk(x, p[0], P), P)
        outs.append(y.reshape(x.shape))
    return tuple(outs)
```

```python
import functools

import jax
import jax.numpy as jnp
from jax import lax
from jax.experimental import pallas as pl
from jax.experimental.pallas import tpu as pltpu

f32 = jnp.float32
bf16 = jnp.bfloat16

D_MODEL = 1024
D_REC = 512
N_REC_HEADS = 8
REC_HEAD = 64
CONV_W = 4
CONV_LEFT = 2
LRU_C = 8.0
D_SGU = 512
N_SGU_HEADS = 8
SGU_HEAD = 64
CHUNK = 128
N_GROUPS = 4
N_EXP_PER_GROUP = 4
N_EXPERTS = 16
D_EXPERT = 512
D_PLE = 256
EPS = 1e-6

LANES = 128
SUBLANES = 8
NB = SUBLANES
ST = CHUNK
TT = NB * ST
HALO = SUBLANES
PITCH = ST + SUBLANES
N_LANE_GROUPS = D_REC // LANES
MXU_DIM = 256
TM = 512
N_PAIRS = 6
N_CLASSES = N_GROUPS * N_PAIRS
D_ROW = D_MODEL + D_PLE + LANES
ROUTE_ROWS = 32
VMEM_LIMIT = 60 * 1024 * 1024


def _rms(x, g):
    ms = jnp.mean(x * x, axis=-1, keepdims=True)
    return x * lax.rsqrt(ms + EPS) * g


def _softplus(x):
    return jnp.maximum(x, 0.0) + jnp.log1p(jnp.exp(-jnp.abs(x)))


def _gate_ab(xc, wg_ref, ba, bx, lam):
    xcb = xc.astype(bf16)
    sp = _softplus(-lam)
    out = []
    for g in range(N_LANE_GROUPS):
        sl = slice(LANES * g, LANES * (g + 1))
        pre = jnp.dot(xcb[:, sl], wg_ref[g], preferred_element_type=f32)
        r = jax.nn.sigmoid(pre[:, :LANES] + ba[:, sl])
        i = jax.nn.sigmoid(pre[:, LANES:] + bx[:, sl])
        log_a = -LRU_C * r * sp[:, sl]
        a = jnp.exp(log_a)
        z = jnp.tanh(-log_a) * (1.0 + a * a)
        mult = jnp.where(z > 0.0, z * lax.rsqrt(z), 0.0)
        out.append((a, mult * (i * xc[:, sl])))
    return out


def _store_ab(ab, a_ref, b_ref):
    for g, (a, b) in enumerate(ab):
        for s in range(NB):
            a_ref[g, s * PITCH:s * PITCH + ST, :] = a[s * ST:(s + 1) * ST, :]
            b_ref[g, s * PITCH:s * PITCH + ST, :] = b[s * ST:(s + 1) * ST, :]


def _scan(a_ref, b_ref, h_ref, carry_ref, reverse):
    def step(k, hs):
        t = (ST - 1 - k) if reverse else k
        new = []
        for g in range(N_LANE_GROUPS):
            a = a_ref[g, pl.ds(t, NB, stride=PITCH), :]
            b = b_ref[g, pl.ds(t, NB, stride=PITCH), :]
            h = a * hs[g] + b
            h_ref[g, pl.ds(t, NB, stride=PITCH), :] = h
            new.append(h)
        return tuple(new)

    init = tuple(carry_ref[:, LANES * g:LANES * (g + 1)] for g in range(N_LANE_GROUPS))
    hs = lax.fori_loop(0, ST, step, init, unroll=8)
    for g in range(N_LANE_GROUPS):
        carry_ref[:, LANES * g:LANES * (g + 1)] = hs[g]


def _bwd_kernel(x_ref, xp_ref, xn_ref, gmix_ref, wxr_ref, cw_ref, cb_ref, wg_ref, ba_ref, bx_ref, lam_ref,
                xc_ref, hb_ref, ext_ref, a_ref, b_ref, h_ref, carry_ref):
    j = pl.program_id(1)
    n_t = pl.num_programs(1)
    tb = n_t - 1 - j

    @pl.when(j == 0)
    def _():
        carry_ref[...] = jnp.zeros_like(carry_ref)

    gm = gmix_ref[...]

    def proj(xv):
        return jnp.dot(_rms(xv, gm).astype(bf16), wxr_ref[...], preferred_element_type=f32)

    xr = proj(x_ref[...].reshape(TT, D_MODEL))
    ext_ref[:, HALO:HALO + ST, :] = xr.reshape(NB, ST, D_REC)
    hp = proj(xp_ref[...].reshape(NB * HALO, D_MODEL)).reshape(NB, HALO, D_REC)
    ext_ref[:, 0:HALO, :] = jnp.where(tb > 0, hp, 0.0)
    hn = proj(xn_ref[...].reshape(NB * HALO, D_MODEL)).reshape(NB, HALO, D_REC)
    ext_ref[:, HALO + ST:2 * HALO + ST, :] = jnp.where(tb < n_t - 1, hn, 0.0)

    cw = cw_ref[...]
    xc = None
    for k in range(CONV_W):
        off = HALO - CONV_LEFT + k
        term = ext_ref[:, off:off + ST, :] * cw[k:k + 1, :]
        xc = term if xc is None else xc + term
    xc = xc + cb_ref[...]
    xc_ref[...] = xc

    ab = _gate_ab(xc.reshape(TT, D_REC), wg_ref, ba_ref[...], bx_ref[...], lam_ref[...])
    _store_ab(ab, a_ref, b_ref)
    _scan(a_ref, b_ref, h_ref, carry_ref, reverse=True)
    for g in range(N_LANE_GROUPS):
        for s in range(NB):
            hb_ref[s, :, LANES * g:LANES * (g + 1)] = h_ref[g, s * PITCH:s * PITCH + ST, :]


def _route(logits):
    gl = [logits[i:i + 1, :] for i in range(N_GROUPS)]
    gm = jnp.maximum(jnp.maximum(gl[0], gl[1]), jnp.maximum(gl[2], gl[3]))
    sg = sum(jnp.exp(v - gm) for v in gl)
    g_top = 1.0 / sg
    gidx = jnp.where(gl[0] == gm, 0, jnp.where(gl[1] == gm, 1, jnp.where(gl[2] == gm, 2, 3)))
    el = []
    for k in range(N_EXP_PER_GROUP):
        rows = [logits[N_GROUPS + N_EXP_PER_GROUP * g + k:N_GROUPS + N_EXP_PER_GROUP * g + k + 1, :]
                for g in range(N_GROUPS)]
        el.append(jnp.where(gidx == 0, rows[0], jnp.where(gidx == 1, rows[1], jnp.where(gidx == 2, rows[2], rows[3]))))
    em = jnp.maximum(jnp.maximum(el[0], el[1]), jnp.maximum(el[2], el[3]))
    ee = [jnp.exp(v - em) for v in el]
    se = ee[0] + ee[1] + ee[2] + ee[3]
    p = [v / se for v in ee]
    p1 = jnp.maximum(jnp.maximum(p[0], p[1]), jnp.maximum(p[2], p[3]))
    i1 = jnp.where(p[0] == p1, 0, jnp.where(p[1] == p1, 1, jnp.where(p[2] == p1, 2, 3)))
    q = [jnp.where(i1 == k, -1.0, p[k]) for k in range(N_EXP_PER_GROUP)]
    p2 = jnp.maximum(jnp.maximum(q[0], q[1]), jnp.maximum(q[2], q[3]))
    i2 = jnp.where(q[0] == p2, 0, jnp.where(q[1] == p2, 1, jnp.where(q[2] == p2, 2, 3)))
    norm = p1 + p2
    w1 = g_top * (p1 / norm)
    w2 = g_top * (p2 / norm)
    first_is_lo = i1 < i2
    lo = jnp.where(first_is_lo, i1, i2)
    hi = jnp.where(first_is_lo, i2, i1)
    w_lo = jnp.where(first_is_lo, w1, w2)
    w_hi = jnp.where(first_is_lo, w2, w1)
    base = jnp.where(lo == 0, 0, jnp.where(lo == 1, 3, 5))
    cls = gidx * N_PAIRS + base + hi - lo - 1
    return cls.astype(f32), w_lo, w_hi


def _fwd_kernel(x_ref, xc_ref, hb_ref, p_ref, gmix_ref, wgr_ref, wuvt_ref, wg_ref, ba_ref, bx_ref, lam_ref,
                lng_ref, lnb_ref, bd_ref, bst_ref, grec_ref, gsgu_ref, wor_ref, wos_ref, gffn_ref, wrt_ref, brt_ref,
                hx_ref, a_ref, b_ref, h_ref, carry_ref):
    j = pl.program_id(1)

    @pl.when(j == 0)
    def _():
        carry_ref[...] = jnp.zeros_like(carry_ref)

    x2 = x_ref[...].reshape(TT, D_MODEL)
    n = _rms(x2, gmix_ref[...]).astype(bf16)

    ab = _gate_ab(xc_ref[...].reshape(TT, D_REC), wg_ref, ba_ref[...], bx_ref[...], lam_ref[...])
    _store_ab(ab, a_ref, b_ref)
    _scan(a_ref, b_ref, h_ref, carry_ref, reverse=False)
    hf = jnp.concatenate(
        [jnp.concatenate([h_ref[g, s * PITCH:s * PITCH + ST, :] for s in range(NB)], axis=0)
         for g in range(N_LANE_GROUPS)], axis=1)
    gr = jnp.dot(n, wgr_ref[...], preferred_element_type=f32)
    y_rec = (hf + hb_ref[...].reshape(TT, D_REC)) * jax.nn.gelu(gr)
    rec_n = _rms(y_rec, grec_ref[...]).astype(bf16)

    zt = lax.dot_general(wuvt_ref[...], n, (((1,), (1,)), ((), ())), preferred_element_type=f32)
    ut = jax.nn.gelu(zt[:D_SGU, :])
    vt = jax.nn.gelu(zt[D_SGU:, :])
    mu = jnp.mean(vt, axis=0, keepdims=True)
    vc = vt - mu
    var = jnp.mean(vc * vc, axis=0, keepdims=True)
    vn = (vc * lax.rsqrt(var + EPS) * lng_ref[...] + lnb_ref[...]).astype(bf16)
    n_slab = TT // MXU_DIM
    heads = []
    for h in range(N_SGU_HEADS):
        rows = vn[SGU_HEAD * h:SGU_HEAD * (h + 1), :]
        lhs = jnp.concatenate([rows[:, MXU_DIM * c:MXU_DIM * (c + 1)] for c in range(n_slab)], axis=0)
        res = jnp.dot(lhs, bd_ref[h], preferred_element_type=f32)
        heads.append(jnp.concatenate([res[SGU_HEAD * c:SGU_HEAD * (c + 1), :] for c in range(n_slab)], axis=1))
    mixed = jnp.concatenate(heads, axis=0) + jnp.concatenate([bst_ref[...]] * (TT // CHUNK), axis=1)
    yst = ut * mixed
    ms = jnp.mean(yst * yst, axis=0, keepdims=True)
    sgu_n = (yst * lax.rsqrt(ms + EPS) * gsgu_ref[...]).T.astype(bf16)

    delta = (jnp.dot(rec_n, wor_ref[...], preferred_element_type=f32)
             + jnp.dot(sgu_n, wos_ref[...], preferred_element_type=f32))
    h1 = x2 + delta
    hx_ref[:, :, 0:D_MODEL] = h1.reshape(NB, ST, D_MODEL)
    hx_ref[:, :, D_MODEL:D_MODEL + D_PLE] = p_ref[...]

    n2 = _rms(h1, gffn_ref[...]).astype(bf16)
    logits = lax.dot_general(wrt_ref[...], n2, (((1,), (1,)), ((), ())), preferred_element_type=f32) + brt_ref[...]
    cls, w_lo, w_hi = _route(logits)
    rt = jnp.concatenate([cls, w_lo, w_hi, jnp.zeros((LANES - 3, TT), f32)], axis=0).T
    hx_ref[:, :, D_MODEL + D_PLE:D_ROW] = rt.reshape(NB, ST, LANES)


def _expert_kernel(elo_ref, ehi_ref, nval_ref, ci_ref, ni_ref, pi_ref, hx_hbm, gffn_ref,
                   w1lo_ref, w3lo_ref, w2lo_ref, w1hi_ref, w3hi_ref, w2hi_ref,
                   gple_ref, wpg_ref, wpp_ref, gfin_ref, y_hbm, xbuf0, xbuf1, ybuf0, ybuf1, gsem, ssem):
    i = pl.program_id(0)
    n = pl.num_programs(0)
    cnt = nval_ref[i]
    nxt = jnp.minimum(i + 1, n - 1)
    n_grp = TM // SUBLANES

    def gather_row(i_ref, buf, par, r):
        return pltpu.make_async_copy(hx_hbm.at[pl.ds(i_ref[0, 0, r], 1)], buf.at[pl.ds(r, 1)], gsem.at[par])

    def scatter_row(i_ref, buf, par, r):
        return pltpu.make_async_copy(buf.at[pl.ds(r, 1)], y_hbm.at[pl.ds(i_ref[0, 0, r], 1)], ssem.at[par])

    def wait_tile_in(buf, par):
        pltpu.make_async_copy(hx_hbm.at[pl.ds(0, TM)], buf, gsem.at[par]).wait()

    def wait_tile_out(buf, par):
        pltpu.make_async_copy(buf, y_hbm.at[pl.ds(0, TM)], ssem.at[par]).wait()

    def expert(n2, w1_ref, w3_ref, w2_ref):
        a = jnp.dot(n2, w1_ref[0], preferred_element_type=f32)
        b = jnp.dot(n2, w3_ref[0], preferred_element_type=f32)
        hdn = (jax.nn.silu(a) * b).astype(bf16)
        return jnp.dot(hdn, w2_ref[0], preferred_element_type=f32)

    def compute(xb, yb):
        xrow = xb[...]
        xs = xrow[:, 0:D_MODEL]
        ws = xrow[:, D_MODEL + D_PLE:D_ROW]
        n2 = _rms(xs, gffn_ref[...]).astype(bf16)
        out = ws[:, 1:2] * expert(n2, w1lo_ref, w3lo_ref, w2lo_ref)
        out = out + ws[:, 2:3] * expert(n2, w1hi_ref, w3hi_ref, w2hi_ref)
        h2 = xs + out
        n3 = _rms(h2, gple_ref[...]).astype(bf16)
        gate = jax.nn.sigmoid(jnp.dot(n3, wpg_ref[...], preferred_element_type=f32))
        ps = xrow[:, D_MODEL:D_MODEL + D_PLE]
        proj = jnp.dot(ps.astype(bf16), wpp_ref[...], preferred_element_type=f32)
        h3 = h2 + gate * proj
        yb[...] = _rms(h3, gfin_ref[...])

    def step(par):
        xb, xo = (xbuf0, xbuf1) if par == 0 else (xbuf1, xbuf0)
        yb, yo = (ybuf0, ybuf1) if par == 0 else (ybuf1, ybuf0)

        @pl.when(i == 0)
        def _():
            def body(k, c):
                for u in range(SUBLANES):
                    gather_row(ci_ref, xb, par, k * SUBLANES + u).start(priority=u % 2)
                return c
            lax.fori_loop(0, n_grp, body, 0)
            yo[...] = jnp.zeros_like(yo)

        wait_tile_in(xb, par)

        @pl.when(i >= 1)
        def _():
            wait_tile_out(yb, par)

        for r in range(TM):
            gather_row(ni_ref, xo, 1 - par, r).start(priority=r % 2)
            scatter_row(pi_ref, yo, 1 - par, r).start(priority=r % 2)
        compute(xb, yb)

        @pl.when((i == n - 1) | (nval_ref[nxt] == 0))
        def _():
            wait_tile_in(xo, 1 - par)
            wait_tile_out(yo, 1 - par)

            def send(r, c):
                scatter_row(ci_ref, yb, par, r).start()
                return c
            lax.fori_loop(0, cnt, send, 0)

            def drain(r, c):
                scatter_row(ci_ref, yb, par, 0).wait()
                return c
            lax.fori_loop(0, cnt, drain, 0)

    @pl.when(cnt > 0)
    def _():
        @pl.when(i % 2 == 0)
        def _():
            step(0)

        @pl.when(i % 2 == 1)
        def _():
            step(1)


def _const_spec(shape):
    nd = len(shape)
    return pl.BlockSpec(shape, lambda *_: (0,) * nd)


def _gate_blocks(w_a, w_x):
    def pair(w):
        z = jnp.zeros((REC_HEAD, REC_HEAD), w.dtype)
        return jnp.stack([jnp.block([[w[2 * g], z], [z, w[2 * g + 1]]]) for g in range(N_LANE_GROUPS)])
    return jnp.concatenate([pair(w_a), pair(w_x)], axis=-1).astype(bf16)


def _mix_trunk(x, p, P):
    bsz, seq, _ = x.shape
    assert bsz % NB == 0 and seq % ST == 0
    n_g, n_t = bsz // NB, seq // ST
    halo_blocks = seq // HALO
    row = lambda v: v.reshape(1, -1)

    x_spec_b = pl.BlockSpec((NB, ST, D_MODEL), lambda g, j: (g, n_t - 1 - j, 0))
    xp_spec = pl.BlockSpec((NB, HALO, D_MODEL),
                           lambda g, j: (g, jnp.maximum((n_t - 1 - j) * (ST // HALO) - 1, 0), 0))
    xn_spec = pl.BlockSpec((NB, HALO, D_MODEL),
                           lambda g, j: (g, jnp.minimum((n_t - j) * (ST // HALO), halo_blocks - 1), 0))
    rec_spec_b = pl.BlockSpec((NB, ST, D_REC), lambda g, j: (g, n_t - 1 - j, 0))
    scan_scratch = [pltpu.VMEM((N_LANE_GROUPS, NB * PITCH, LANES), f32),
                    pltpu.VMEM((N_LANE_GROUPS, NB * PITCH, LANES), f32),
                    pltpu.VMEM((N_LANE_GROUPS, NB * PITCH, LANES), f32),
                    pltpu.VMEM((NB, D_REC), f32)]

    bwd_w = [row(P['g_mix']), P['w_xr'], P['conv_w'], row(P['conv_b']), P['wg_b'],
             row(P['ba_b']), row(P['bx_b']), row(P['lam_b'])]
    xc, hb = pl.pallas_call(
        _bwd_kernel,
        grid=(n_g, n_t),
        in_specs=[x_spec_b, xp_spec, xn_spec] + [_const_spec(w.shape) for w in bwd_w],
        out_specs=[rec_spec_b, rec_spec_b],
        out_shape=[jax.ShapeDtypeStruct((bsz, seq, D_REC), f32)] * 2,
        scratch_shapes=[pltpu.VMEM((NB, ST + 2 * HALO, D_REC), f32)] + scan_scratch,
        compiler_params=pltpu.CompilerParams(dimension_semantics=("arbitrary", "arbitrary"),
                                             vmem_limit_bytes=VMEM_LIMIT),
        name="bwd_scan",
    )(x, x, x, *bwd_w)

    tile_spec = lambda d: pl.BlockSpec((NB, ST, d), lambda g, j: (g, j, 0))
    fwd_w = [row(P['g_mix']), P['w_gr'], P['w_uvt'], P['wg_f'], row(P['ba_f']), row(P['bx_f']), row(P['lam_f']),
             P['ln_g'], P['ln_b'], P['bd'], P['bst'], row(P['g_rec']), P['g_sgu'], P['w_or'], P['w_os'],
             row(P['g_ffn']), P['w_rt'], P['b_rt']]
    hx = pl.pallas_call(
        _fwd_kernel,
        grid=(n_g, n_t),
        in_specs=[tile_spec(D_MODEL), tile_spec(D_REC), tile_spec(D_REC), tile_spec(D_PLE)]
                 + [_const_spec(w.shape) for w in fwd_w],
        out_specs=tile_spec(D_ROW),
        out_shape=jax.ShapeDtypeStruct((bsz, seq, D_ROW), f32),
        scratch_shapes=scan_scratch,
        compiler_params=pltpu.CompilerParams(dimension_semantics=("arbitrary", "arbitrary"),
                                             vmem_limit_bytes=VMEM_LIMIT),
        name="fwd_mix",
    )(x, xc, hb, p, *fwd_w)
    return hx.reshape(bsz * seq, D_ROW)


def _expert_stage(hx, P):
    n_tok = hx.shape[0]
    assert n_tok % TM == 0
    n_tiles = n_tok // TM + N_CLASSES
    assert n_tok // TM > N_CLASSES
    i32 = jnp.int32
    cls = hx[:, D_MODEL + D_PLE].astype(i32)

    class_ids = jnp.arange(N_CLASSES, dtype=i32)
    counts = jnp.sum((cls[:, None] == class_ids[None, :]).astype(i32), axis=0)
    order = jnp.argsort(cls, stable=True).astype(i32)
    dense_start = jnp.cumsum(counts) - counts
    full_c, rem_c = counts // TM, counts % TM
    cum_part = jnp.cumsum((rem_c > 0).astype(i32))
    cum_full = jnp.cumsum(full_c)
    n_part, n_full = cum_part[-1], cum_full[-1]
    tile_ids = jnp.arange(n_tiles, dtype=i32)
    is_part = tile_ids < n_part
    active = tile_ids < n_part + n_full
    first_reaching = lambda cum, k: jnp.minimum(
        jnp.sum((cum[None, :] <= k[:, None]).astype(i32), axis=1), N_CLASSES - 1)
    cls_part = first_reaching(cum_part, tile_ids)
    full_id = jnp.maximum(tile_ids - n_part, 0)
    cls_full = first_reaching(cum_full, full_id)
    tile_cls = jnp.where(is_part, cls_part, cls_full)
    n_valid = jnp.where(active, jnp.where(is_part, rem_c[tile_cls], TM), 0)
    in_class = jnp.where(is_part, full_c[tile_cls], full_id - (cum_full - full_c)[tile_cls]) * TM
    tile_dense = dense_start[tile_cls] + in_class
    rank_end = jnp.cumsum(n_valid)
    rank_start = rank_end - n_valid
    ranks = jnp.arange(n_tok, dtype=i32)
    in_tile = (rank_start[None, :] <= ranks[:, None]) & (ranks[:, None] < rank_end[None, :])
    shift = jnp.sum(jnp.where(in_tile, (tile_dense - rank_start)[None, :], 0), axis=1)
    tok_by_rank = order[ranks + shift]
    look = jnp.minimum(rank_start[:, None] + jnp.arange(TM, dtype=i32)[None, :], n_tok - 1)
    src = tok_by_rank[look].reshape(n_tiles, 1, TM)

    pair_lo = jnp.array([0, 0, 0, 1, 1, 2], jnp.int32)
    pair_hi = jnp.array([1, 2, 3, 2, 3, 3], jnp.int32)
    e_lo = (tile_cls // N_PAIRS) * N_EXP_PER_GROUP + pair_lo[tile_cls % N_PAIRS]
    e_hi = (tile_cls // N_PAIRS) * N_EXP_PER_GROUP + pair_hi[tile_cls % N_PAIRS]

    row = lambda v: v.reshape(1, -1)
    idx_spec = lambda f: pl.BlockSpec((1, 1, TM), f, memory_space=pltpu.SMEM)
    any_spec = pl.BlockSpec(memory_space=pl.ANY)
    lo_spec = lambda s: pl.BlockSpec((1,) + s, lambda i, elo, ehi, nv: (elo[i], 0, 0))
    hi_spec = lambda s: pl.BlockSpec((1,) + s, lambda i, elo, ehi, nv: (ehi[i], 0, 0))
    up_shape, down_shape = (D_MODEL, D_EXPERT), (D_EXPERT, D_MODEL)
    consts = [row(P['g_ple']), P['w_pg'], P['w_pp'], row(P['g_final'])]
    this_tile = lambda i, *_: (i, 0, 0)
    next_tile = lambda i, *_: (jnp.minimum(i + 1, n_tiles - 1), 0, 0)
    prev_tile = lambda i, *_: (jnp.maximum(i - 1, 0), 0, 0)
    tile_buf = lambda d: pltpu.VMEM((TM, d), f32)
    y = pl.pallas_call(
        _expert_kernel,
        grid_spec=pltpu.PrefetchScalarGridSpec(
            num_scalar_prefetch=3,
            grid=(n_tiles,),
            in_specs=[idx_spec(this_tile), idx_spec(next_tile), idx_spec(prev_tile),
                      any_spec, _const_spec((1, D_MODEL)),
                      lo_spec(up_shape), lo_spec(up_shape), lo_spec(down_shape),
                      hi_spec(up_shape), hi_spec(up_shape), hi_spec(down_shape)]
                     + [_const_spec(w.shape) for w in consts],
            out_specs=any_spec,
            scratch_shapes=[tile_buf(D_ROW), tile_buf(D_ROW), tile_buf(D_MODEL), tile_buf(D_MODEL),
                            pltpu.SemaphoreType.DMA((2,)), pltpu.SemaphoreType.DMA((2,))],
        ),
        out_shape=jax.ShapeDtypeStruct((n_tok, D_MODEL), f32),
        compiler_params=pltpu.CompilerParams(dimension_semantics=("arbitrary",), vmem_limit_bytes=VMEM_LIMIT),
        name="experts",
    )(e_lo, e_hi, n_valid.astype(i32), src, src, src, hx, row(P['g_ffn']),
      P['w1'], P['w3'], P['w2'], P['w1'], P['w3'], P['w2'], *consts)
    return y


def _prepare(g_mix, w_in, conv_w, conv_b, lru_w_a, lru_b_a, lru_w_x, lru_b_x, lru_lambda, sgu_ln_g, sgu_ln_b,
             sgu_w_s, sgu_b_s, g_rec_out, g_sgu_out, w_out, g_ffn, w_router_group, b_router_group,
             w_router_expert, b_router_expert, w_exp_gate, w_exp_up, w_exp_down, g_ple, w_ple_gate, w_ple_proj,
             g_final):
    l = 0
    w = w_in[l]
    ws_t = jnp.transpose(sgu_w_s[l], (0, 2, 1))
    z = jnp.zeros((CHUNK, CHUNK), f32)
    bd = jnp.stack([jnp.block([[ws_t[h], z], [z, ws_t[h]]]) for h in range(N_SGU_HEADS)]).astype(bf16)
    w_rt = jnp.concatenate([w_router_group[l].T, w_router_expert[l].T,
                            jnp.zeros((ROUTE_ROWS - N_GROUPS - N_EXPERTS, D_MODEL), f32)], axis=0).astype(bf16)
    b_rt = jnp.concatenate([b_router_group[l], b_router_expert[l],
                            jnp.zeros((ROUTE_ROWS - N_GROUPS - N_EXPERTS,), f32)]).reshape(ROUTE_ROWS, 1)
    return dict(
        g_mix=g_mix[l], w_xr=w[:, :D_REC].astype(bf16), w_gr=w[:, D_REC:2 * D_REC].astype(bf16),
        w_uvt=w[:, 2 * D_REC:].T.astype(bf16),
        conv_w=conv_w[l], conv_b=conv_b[l],
        wg_f=_gate_blocks(lru_w_a[l, 0], lru_w_x[l, 0]), wg_b=_gate_blocks(lru_w_a[l, 1], lru_w_x[l, 1]),
        ba_f=lru_b_a[l, 0], ba_b=lru_b_a[l, 1], bx_f=lru_b_x[l, 0], bx_b=lru_b_x[l, 1],
        lam_f=lru_lambda[l, 0], lam_b=lru_lambda[l, 1],
        ln_g=sgu_ln_g[l].reshape(D_SGU, 1), ln_b=sgu_ln_b[l].reshape(D_SGU, 1),
        bd=bd, bst=jnp.repeat(sgu_b_s[l], SGU_HEAD, axis=0),
        g_rec=g_rec_out[l], g_sgu=g_sgu_out[l].reshape(D_SGU, 1),
        w_or=w_out[l, :D_REC].astype(bf16), w_os=w_out[l, D_REC:].astype(bf16),
        g_ffn=g_ffn[l], w_rt=w_rt, b_rt=b_rt,
        w1=w_exp_gate[l].astype(bf16), w3=w_exp_up[l].astype(bf16), w2=w_exp_down[l].astype(bf16),
        g_ple=g_ple[l], w_pg=w_ple_gate[l].astype(bf16), w_pp=w_ple_proj[l].astype(bf16), g_final=g_final,
    )


def kernel(x_prompt, x_sample, p_prompt, p_sample, g_mix, w_in, conv_w, conv_b, lru_w_a, lru_b_a, lru_w_x, lru_b_x, lru_lambda, sgu_ln_g, sgu_ln_b, sgu_w_s, sgu_b_s, g_rec_out, g_sgu_out, w_out, g_ffn, w_router_group, b_router_group, w_router_expert, b_router_expert, w_exp_gate, w_exp_up, w_exp_down, g_ple, w_ple_gate, w_ple_proj, g_final):
    assert w_in.shape[0] == 1, "single-layer trunk"
    P = _prepare(g_mix, w_in, conv_w, conv_b, lru_w_a, lru_b_a, lru_w_x, lru_b_x, lru_lambda, sgu_ln_g, sgu_ln_b,
                 sgu_w_s, sgu_b_s, g_rec_out, g_sgu_out, w_out, g_ffn, w_router_group, b_router_group,
                 w_router_expert, b_router_expert, w_exp_gate, w_exp_up, w_exp_down, g_ple, w_ple_gate,
                 w_ple_proj, g_final)
    outs = []
    for x, p in ((x_prompt, p_prompt), (x_sample, p_sample)):
        y = _expert_stage(_mix_trunk(x, p[0], P), P)
        outs.append(y.reshape(x.shape))
    return tuple(outs)
```

```python
import functools

import jax
import jax.numpy as jnp
from jax import lax
from jax.experimental import pallas as pl
from jax.experimental.pallas import tpu as pltpu

f32 = jnp.float32
bf16 = jnp.bfloat16

D_MODEL = 1024
D_REC = 512
N_REC_HEADS = 8
REC_HEAD = 64
CONV_W = 4
CONV_LEFT = 2
LRU_C = 8.0
D_SGU = 512
N_SGU_HEADS = 8
SGU_HEAD = 64
CHUNK = 128
N_GROUPS = 4
N_EXP_PER_GROUP = 4
N_EXPERTS = 16
D_EXPERT = 512
D_PLE = 256
EPS = 1e-6

LANES = 128
SUBLANES = 8
NB = SUBLANES
ST = CHUNK
TT = NB * ST
HALO = SUBLANES
PITCH = ST + SUBLANES
N_LANE_GROUPS = D_REC // LANES
MXU_DIM = 256
TM = 512
N_PAIRS = 6
N_CLASSES = N_GROUPS * N_PAIRS
D_ROW = D_MODEL + D_PLE + LANES
ROUTE_ROWS = 32
VMEM_LIMIT = 60 * 1024 * 1024


def _rms(x, g):
    ms = jnp.mean(x * x, axis=-1, keepdims=True)
    return x * lax.rsqrt(ms + EPS) * g


def _softplus(x):
    return jnp.maximum(x, 0.0) + jnp.log1p(jnp.exp(-jnp.abs(x)))


def _gate_ab(xc, wg_ref, ba, bx, lam):
    xcb = xc.astype(bf16)
    sp = _softplus(-lam)
    out = []
    for g in range(N_LANE_GROUPS):
        sl = slice(LANES * g, LANES * (g + 1))
        pre = jnp.dot(xcb[:, sl], wg_ref[g], preferred_element_type=f32)
        r = jax.nn.sigmoid(pre[:, :LANES] + ba[:, sl])
        i = jax.nn.sigmoid(pre[:, LANES:] + bx[:, sl])
        log_a = -LRU_C * r * sp[:, sl]
        a = jnp.exp(log_a)
        z = jnp.tanh(-log_a) * (1.0 + a * a)
        mult = jnp.where(z > 0.0, z * lax.rsqrt(z), 0.0)
        out.append((a, mult * (i * xc[:, sl])))
    return out


def _store_ab(ab, a_ref, b_ref):
    for g, (a, b) in enumerate(ab):
        for s in range(NB):
            a_ref[g, s * PITCH:s * PITCH + ST, :] = a[s * ST:(s + 1) * ST, :]
            b_ref[g, s * PITCH:s * PITCH + ST, :] = b[s * ST:(s + 1) * ST, :]


def _scan(a_ref, b_ref, h_ref, carry_ref, reverse):
    def step(k, hs):
        t = (ST - 1 - k) if reverse else k
        new = []
        for g in range(N_LANE_GROUPS):
            a = a_ref[g, pl.ds(t, NB, stride=PITCH), :]
            b = b_ref[g, pl.ds(t, NB, stride=PITCH), :]
            h = a * hs[g] + b
            h_ref[g, pl.ds(t, NB, stride=PITCH), :] = h
            new.append(h)
        return tuple(new)

    init = tuple(carry_ref[:, LANES * g:LANES * (g + 1)] for g in range(N_LANE_GROUPS))
    hs = lax.fori_loop(0, ST, step, init, unroll=8)
    for g in range(N_LANE_GROUPS):
        carry_ref[:, LANES * g:LANES * (g + 1)] = hs[g]


def _bwd_kernel(x_ref, xp_ref, xn_ref, gmix_ref, wxr_ref, cw_ref, cb_ref, wg_ref, ba_ref, bx_ref, lam_ref,
                xc_ref, hb_ref, ext_ref, a_ref, b_ref, h_ref, carry_ref):
    j = pl.program_id(1)
    n_t = pl.num_programs(1)
    tb = n_t - 1 - j

    @pl.when(j == 0)
    def _():
        carry_ref[...] = jnp.zeros_like(carry_ref)

    gm = gmix_ref[...]

    def proj(xv):
        return jnp.dot(_rms(xv, gm).astype(bf16), wxr_ref[...], preferred_element_type=f32)

    xr = proj(x_ref[...].reshape(TT, D_MODEL))
    ext_ref[:, HALO:HALO + ST, :] = xr.reshape(NB, ST, D_REC)
    hp = proj(xp_ref[...].reshape(NB * HALO, D_MODEL)).reshape(NB, HALO, D_REC)
    ext_ref[:, 0:HALO, :] = jnp.where(tb > 0, hp, 0.0)
    hn = proj(xn_ref[...].reshape(NB * HALO, D_MODEL)).reshape(NB, HALO, D_REC)
    ext_ref[:, HALO + ST:2 * HALO + ST, :] = jnp.where(tb < n_t - 1, hn, 0.0)

    cw = cw_ref[...]
    xc = None
    for k in range(CONV_W):
        off = HALO - CONV_LEFT + k
        term = ext_ref[:, off:off + ST, :] * cw[k:k + 1, :]
        xc = term if xc is None else xc + term
    xc = xc + cb_ref[...]
    xc_ref[...] = xc

    ab = _gate_ab(xc.reshape(TT, D_REC), wg_ref, ba_ref[...], bx_ref[...], lam_ref[...])
    _store_ab(ab, a_ref, b_ref)
    _scan(a_ref, b_ref, h_ref, carry_ref, reverse=True)
    for g in range(N_LANE_GROUPS):
        for s in range(NB):
            hb_ref[s, :, LANES * g:LANES * (g + 1)] = h_ref[g, s * PITCH:s * PITCH + ST, :]


def _route(logits):
    gl = [logits[i:i + 1, :] for i in range(N_GROUPS)]
    gm = jnp.maximum(jnp.maximum(gl[0], gl[1]), jnp.maximum(gl[2], gl[3]))
    sg = sum(jnp.exp(v - gm) for v in gl)
    g_top = 1.0 / sg
    gidx = jnp.where(gl[0] == gm, 0, jnp.where(gl[1] == gm, 1, jnp.where(gl[2] == gm, 2, 3)))
    el = []
    for k in range(N_EXP_PER_GROUP):
        rows = [logits[N_GROUPS + N_EXP_PER_GROUP * g + k:N_GROUPS + N_EXP_PER_GROUP * g + k + 1, :]
                for g in range(N_GROUPS)]
        el.append(jnp.where(gidx == 0, rows[0], jnp.where(gidx == 1, rows[1], jnp.where(gidx == 2, rows[2], rows[3]))))
    em = jnp.maximum(jnp.maximum(el[0], el[1]), jnp.maximum(el[2], el[3]))
    ee = [jnp.exp(v - em) for v in el]
    se = ee[0] + ee[1] + ee[2] + ee[3]
    p = [v / se for v in ee]
    p1 = jnp.maximum(jnp.maximum(p[0], p[1]), jnp.maximum(p[2], p[3]))
    i1 = jnp.where(p[0] == p1, 0, jnp.where(p[1] == p1, 1, jnp.where(p[2] == p1, 2, 3)))
    q = [jnp.where(i1 == k, -1.0, p[k]) for k in range(N_EXP_PER_GROUP)]
    p2 = jnp.maximum(jnp.maximum(q[0], q[1]), jnp.maximum(q[2], q[3]))
    i2 = jnp.where(q[0] == p2, 0, jnp.where(q[1] == p2, 1, jnp.where(q[2] == p2, 2, 3)))
    norm = p1 + p2
    w1 = g_top * (p1 / norm)
    w2 = g_top * (p2 / norm)
    first_is_lo = i1 < i2
    lo = jnp.where(first_is_lo, i1, i2)
    hi = jnp.where(first_is_lo, i2, i1)
    w_lo = jnp.where(first_is_lo, w1, w2)
    w_hi = jnp.where(first_is_lo, w2, w1)
    base = jnp.where(lo == 0, 0, jnp.where(lo == 1, 3, 5))
    cls = gidx * N_PAIRS + base + hi - lo - 1
    return cls.astype(f32), w_lo, w_hi


def _fwd_kernel(x_ref, xc_ref, hb_ref, p_ref, gmix_ref, wgr_ref, wuvt_ref, wg_ref, ba_ref, bx_ref, lam_ref,
                lng_ref, lnb_ref, bd_ref, bst_ref, grec_ref, gsgu_ref, wor_ref, wos_ref, gffn_ref, wrt_ref, brt_ref,
                hx_ref, a_ref, b_ref, h_ref, carry_ref):
    j = pl.program_id(1)

    @pl.when(j == 0)
    def _():
        carry_ref[...] = jnp.zeros_like(carry_ref)

    x2 = x_ref[...].reshape(TT, D_MODEL)
    n = _rms(x2, gmix_ref[...]).astype(bf16)

    ab = _gate_ab(xc_ref[...].reshape(TT, D_REC), wg_ref, ba_ref[...], bx_ref[...], lam_ref[...])
    _store_ab(ab, a_ref, b_ref)
    _scan(a_ref, b_ref, h_ref, carry_ref, reverse=False)
    hf = jnp.concatenate(
        [jnp.concatenate([h_ref[g, s * PITCH:s * PITCH + ST, :] for s in range(NB)], axis=0)
         for g in range(N_LANE_GROUPS)], axis=1)
    gr = jnp.dot(n, wgr_ref[...], preferred_element_type=f32)
    y_rec = (hf + hb_ref[...].reshape(TT, D_REC)) * jax.nn.gelu(gr)
    rec_n = _rms(y_rec, grec_ref[...]).astype(bf16)

    zt = lax.dot_general(wuvt_ref[...], n, (((1,), (1,)), ((), ())), preferred_element_type=f32)
    ut = jax.nn.gelu(zt[:D_SGU, :])
    vt = jax.nn.gelu(zt[D_SGU:, :])
    mu = jnp.mean(vt, axis=0, keepdims=True)
    vc = vt - mu
    var = jnp.mean(vc * vc, axis=0, keepdims=True)
    vn = (vc * lax.rsqrt(var + EPS) * lng_ref[...] + lnb_ref[...]).astype(bf16)
    n_slab = TT // MXU_DIM
    heads = []
    for h in range(N_SGU_HEADS):
        rows = vn[SGU_HEAD * h:SGU_HEAD * (h + 1), :]
        lhs = jnp.concatenate([rows[:, MXU_DIM * c:MXU_DIM * (c + 1)] for c in range(n_slab)], axis=0)
        res = jnp.dot(lhs, bd_ref[h], preferred_element_type=f32)
        heads.append(jnp.concatenate([res[SGU_HEAD * c:SGU_HEAD * (c + 1), :] for c in range(n_slab)], axis=1))
    mixed = jnp.concatenate(heads, axis=0) + jnp.concatenate([bst_ref[...]] * (TT // CHUNK), axis=1)
    yst = ut * mixed
    ms = jnp.mean(yst * yst, axis=0, keepdims=True)
    sgu_n = (yst * lax.rsqrt(ms + EPS) * gsgu_ref[...]).T.astype(bf16)

    delta = (jnp.dot(rec_n, wor_ref[...], preferred_element_type=f32)
             + jnp.dot(sgu_n, wos_ref[...], preferred_element_type=f32))
    h1 = x2 + delta
    hx_ref[:, :, 0:D_MODEL] = h1.reshape(NB, ST, D_MODEL)
    hx_ref[:, :, D_MODEL:D_MODEL + D_PLE] = p_ref[...]

    n2 = _rms(h1, gffn_ref[...]).astype(bf16)
    logits = lax.dot_general(wrt_ref[...], n2, (((1,), (1,)), ((), ())), preferred_element_type=f32) + brt_ref[...]
    cls, w_lo, w_hi = _route(logits)
    rt = jnp.concatenate([cls, w_lo, w_hi, jnp.zeros((LANES - 3, TT), f32)], axis=0).T
    hx_ref[:, :, D_MODEL + D_PLE:D_ROW] = rt.reshape(NB, ST, LANES)


def _expert_kernel(elo_ref, ehi_ref, nval_ref, ci_ref, ni_ref, pi_ref, hx_hbm, gffn_ref,
                   w1lo_ref, w3lo_ref, w2lo_ref, w1hi_ref, w3hi_ref, w2hi_ref,
                   gple_ref, wpg_ref, wpp_ref, gfin_ref, y_hbm, xbuf0, xbuf1, ybuf0, ybuf1, gsem, ssem):
    i = pl.program_id(0)
    n = pl.num_programs(0)
    cnt = nval_ref[i]
    nxt = jnp.minimum(i + 1, n - 1)
    n_grp = TM // SUBLANES

    def gather_row(i_ref, buf, par, r):
        return pltpu.make_async_copy(hx_hbm.at[pl.ds(i_ref[0, 0, r], 1)], buf.at[pl.ds(r, 1)], gsem.at[par])

    def scatter_row(i_ref, buf, par, r):
        return pltpu.make_async_copy(buf.at[pl.ds(r, 1)], y_hbm.at[pl.ds(i_ref[0, 0, r], 1)], ssem.at[par])

    def wait_tile_in(buf, par):
        pltpu.make_async_copy(hx_hbm.at[pl.ds(0, TM)], buf, gsem.at[par]).wait()

    def wait_tile_out(buf, par):
        pltpu.make_async_copy(buf, y_hbm.at[pl.ds(0, TM)], ssem.at[par]).wait()

    def expert(n2, w1_ref, w3_ref, w2_ref):
        a = jnp.dot(n2, w1_ref[0], preferred_element_type=f32)
        b = jnp.dot(n2, w3_ref[0], preferred_element_type=f32)
        hdn = (jax.nn.silu(a) * b).astype(bf16)
        return jnp.dot(hdn, w2_ref[0], preferred_element_type=f32)

    def compute(xb, yb):
        xrow = xb[...]
        xs = xrow[:, 0:D_MODEL]
        ws = xrow[:, D_MODEL + D_PLE:D_ROW]
        n2 = _rms(xs, gffn_ref[...]).astype(bf16)
        out = ws[:, 1:2] * expert(n2, w1lo_ref, w3lo_ref, w2lo_ref)
        out = out + ws[:, 2:3] * expert(n2, w1hi_ref, w3hi_ref, w2hi_ref)
        h2 = xs + out
        n3 = _rms(h2, gple_ref[...]).astype(bf16)
        gate = jax.nn.sigmoid(jnp.dot(n3, wpg_ref[...], preferred_element_type=f32))
        ps = xrow[:, D_MODEL:D_MODEL + D_PLE]
        proj = jnp.dot(ps.astype(bf16), wpp_ref[...], preferred_element_type=f32)
        h3 = h2 + gate * proj
        yb[...] = _rms(h3, gfin_ref[...])

    def step(par):
        xb, xo = (xbuf0, xbuf1) if par == 0 else (xbuf1, xbuf0)
        yb, yo = (ybuf0, ybuf1) if par == 0 else (ybuf1, ybuf0)

        @pl.when(i == 0)
        def _():
            def body(k, c):
                for u in range(SUBLANES):
                    gather_row(ci_ref, xb, par, k * SUBLANES + u).start(priority=u % 2)
                return c
            lax.fori_loop(0, n_grp, body, 0)
            yo[...] = jnp.zeros_like(yo)

        wait_tile_in(xb, par)

        @pl.when(i >= 1)
        def _():
            wait_tile_out(yb, par)

        for r in range(TM):
            gather_row(ni_ref, xo, 1 - par, r).start(priority=r % 2)
        for r in range(TM):
            scatter_row(pi_ref, yo, 1 - par, r).start(priority=r % 2)
        compute(xb, yb)

        @pl.when((i == n - 1) | (nval_ref[nxt] == 0))
        def _():
            wait_tile_in(xo, 1 - par)
            wait_tile_out(yo, 1 - par)

            def send(r, c):
                scatter_row(ci_ref, yb, par, r).start()
                return c
            lax.fori_loop(0, cnt, send, 0)

            def drain(r, c):
                scatter_row(ci_ref, yb, par, 0).wait()
                return c
            lax.fori_loop(0, cnt, drain, 0)

    @pl.when(cnt > 0)
    def _():
        @pl.when(i % 2 == 0)
        def _():
            step(0)

        @pl.when(i % 2 == 1)
        def _():
            step(1)


def _const_spec(shape):
    nd = len(shape)
    return pl.BlockSpec(shape, lambda *_: (0,) * nd)


def _gate_blocks(w_a, w_x):
    def pair(w):
        z = jnp.zeros((REC_HEAD, REC_HEAD), w.dtype)
        return jnp.stack([jnp.block([[w[2 * g], z], [z, w[2 * g + 1]]]) for g in range(N_LANE_GROUPS)])
    return jnp.concatenate([pair(w_a), pair(w_x)], axis=-1).astype(bf16)


def _mix_trunk(x, p, P):
    bsz, seq, _ = x.shape
    assert bsz % NB == 0 and seq % ST == 0
    n_g, n_t = bsz // NB, seq // ST
    halo_blocks = seq // HALO
    row = lambda v: v.reshape(1, -1)

    x_spec_b = pl.BlockSpec((NB, ST, D_MODEL), lambda g, j: (g, n_t - 1 - j, 0))
    xp_spec = pl.BlockSpec((NB, HALO, D_MODEL),
                           lambda g, j: (g, jnp.maximum((n_t - 1 - j) * (ST // HALO) - 1, 0), 0))
    xn_spec = pl.BlockSpec((NB, HALO, D_MODEL),
                           lambda g, j: (g, jnp.minimum((n_t - j) * (ST // HALO), halo_blocks - 1), 0))
    rec_spec_b = pl.BlockSpec((NB, ST, D_REC), lambda g, j: (g, n_t - 1 - j, 0))
    scan_scratch = [pltpu.VMEM((N_LANE_GROUPS, NB * PITCH, LANES), f32),
                    pltpu.VMEM((N_LANE_GROUPS, NB * PITCH, LANES), f32),
                    pltpu.VMEM((N_LANE_GROUPS, NB * PITCH, LANES), f32),
                    pltpu.VMEM((NB, D_REC), f32)]

    bwd_w = [row(P['g_mix']), P['w_xr'], P['conv_w'], row(P['conv_b']), P['wg_b'],
             row(P['ba_b']), row(P['bx_b']), row(P['lam_b'])]
    xc, hb = pl.pallas_call(
        _bwd_kernel,
        grid=(n_g, n_t),
        in_specs=[x_spec_b, xp_spec, xn_spec] + [_const_spec(w.shape) for w in bwd_w],
        out_specs=[rec_spec_b, rec_spec_b],
        out_shape=[jax.ShapeDtypeStruct((bsz, seq, D_REC), f32)] * 2,
        scratch_shapes=[pltpu.VMEM((NB, ST + 2 * HALO, D_REC), f32)] + scan_scratch,
        compiler_params=pltpu.CompilerParams(dimension_semantics=("arbitrary", "arbitrary"),
                                             vmem_limit_bytes=VMEM_LIMIT),
        name="bwd_scan",
    )(x, x, x, *bwd_w)

    tile_spec = lambda d: pl.BlockSpec((NB, ST, d), lambda g, j: (g, j, 0))
    fwd_w = [row(P['g_mix']), P['w_gr'], P['w_uvt'], P['wg_f'], row(P['ba_f']), row(P['bx_f']), row(P['lam_f']),
             P['ln_g'], P['ln_b'], P['bd'], P['bst'], row(P['g_rec']), P['g_sgu'], P['w_or'], P['w_os'],
             row(P['g_ffn']), P['w_rt'], P['b_rt']]
    hx = pl.pallas_call(
        _fwd_kernel,
        grid=(n_g, n_t),
        in_specs=[tile_spec(D_MODEL), tile_spec(D_REC), tile_spec(D_REC), tile_spec(D_PLE)]
                 + [_const_spec(w.shape) for w in fwd_w],
        out_specs=tile_spec(D_ROW),
        out_shape=jax.ShapeDtypeStruct((bsz, seq, D_ROW), f32),
        scratch_shapes=scan_scratch,
        compiler_params=pltpu.CompilerParams(dimension_semantics=("arbitrary", "arbitrary"),
                                             vmem_limit_bytes=VMEM_LIMIT),
        name="fwd_mix",
    )(x, xc, hb, p, *fwd_w)
    return hx.reshape(bsz * seq, D_ROW)


def _expert_stage(hx, P):
    n_tok = hx.shape[0]
    assert n_tok % TM == 0
    n_tiles = n_tok // TM + N_CLASSES
    assert n_tok // TM > N_CLASSES
    i32 = jnp.int32
    cls = hx[:, D_MODEL + D_PLE].astype(i32)

    class_ids = jnp.arange(N_CLASSES, dtype=i32)
    counts = jnp.sum((cls[:, None] == class_ids[None, :]).astype(i32), axis=0)
    order = jnp.argsort(cls, stable=True).astype(i32)
    dense_start = jnp.cumsum(counts) - counts
    full_c, rem_c = counts // TM, counts % TM
    cum_part = jnp.cumsum((rem_c > 0).astype(i32))
    cum_full = jnp.cumsum(full_c)
    n_part, n_full = cum_part[-1], cum_full[-1]
    tile_ids = jnp.arange(n_tiles, dtype=i32)
    is_part = tile_ids < n_part
    active = tile_ids < n_part + n_full
    first_reaching = lambda cum, k: jnp.minimum(
        jnp.sum((cum[None, :] <= k[:, None]).astype(i32), axis=1), N_CLASSES - 1)
    cls_part = first_reaching(cum_part, tile_ids)
    full_id = jnp.maximum(tile_ids - n_part, 0)
    cls_full = first_reaching(cum_full, full_id)
    tile_cls = jnp.where(is_part, cls_part, cls_full)
    n_valid = jnp.where(active, jnp.where(is_part, rem_c[tile_cls], TM), 0)
    in_class = jnp.where(is_part, full_c[tile_cls], full_id - (cum_full - full_c)[tile_cls]) * TM
    tile_dense = dense_start[tile_cls] + in_class
    rank_end = jnp.cumsum(n_valid)
    rank_start = rank_end - n_valid
    rows = jnp.arange(TM, dtype=i32)[None, :]
    pos = tile_dense[:, None] + rows
    ahead = jnp.minimum(rank_start[:N_CLASSES, None] + rows, n_tok - 1)
    holds = (rank_start[None, None, :] <= ahead[:, :, None]) & (ahead[:, :, None] < rank_end[None, None, :])
    ahead_pos = ahead + jnp.sum(jnp.where(holds, (tile_dense - rank_start)[None, None, :], 0), axis=2)
    pos = pos.at[:N_CLASSES].set(jnp.where(rows < n_valid[:N_CLASSES, None], pos[:N_CLASSES], ahead_pos))
    src = order[jnp.clip(pos, 0, n_tok - 1)].reshape(n_tiles, 1, TM)

    pair_lo = jnp.array([0, 0, 0, 1, 1, 2], jnp.int32)
    pair_hi = jnp.array([1, 2, 3, 2, 3, 3], jnp.int32)
    e_lo = (tile_cls // N_PAIRS) * N_EXP_PER_GROUP + pair_lo[tile_cls % N_PAIRS]
    e_hi = (tile_cls // N_PAIRS) * N_EXP_PER_GROUP + pair_hi[tile_cls % N_PAIRS]

    row = lambda v: v.reshape(1, -1)
    idx_spec = lambda f: pl.BlockSpec((1, 1, TM), f, memory_space=pltpu.SMEM)
    any_spec = pl.BlockSpec(memory_space=pl.ANY)
    lo_spec = lambda s: pl.BlockSpec((1,) + s, lambda i, elo, ehi, nv: (elo[i], 0, 0))
    hi_spec = lambda s: pl.BlockSpec((1,) + s, lambda i, elo, ehi, nv: (ehi[i], 0, 0))
    up_shape, down_shape = (D_MODEL, D_EXPERT), (D_EXPERT, D_MODEL)
    consts = [row(P['g_ple']), P['w_pg'], P['w_pp'], row(P['g_final'])]
    this_tile = lambda i, *_: (i, 0, 0)
    next_tile = lambda i, *_: (jnp.minimum(i + 1, n_tiles - 1), 0, 0)
    prev_tile = lambda i, *_: (jnp.maximum(i - 1, 0), 0, 0)
    tile_buf = lambda d: pltpu.VMEM((TM, d), f32)
    y = pl.pallas_call(
        _expert_kernel,
        grid_spec=pltpu.PrefetchScalarGridSpec(
            num_scalar_prefetch=3,
            grid=(n_tiles,),
            in_specs=[idx_spec(this_tile), idx_spec(next_tile), idx_spec(prev_tile),
                      any_spec, _const_spec((1, D_MODEL)),
                      lo_spec(up_shape), lo_spec(up_shape), lo_spec(down_shape),
                      hi_spec(up_shape), hi_spec(up_shape), hi_spec(down_shape)]
                     + [_const_spec(w.shape) for w in consts],
            out_specs=any_spec,
            scratch_shapes=[tile_buf(D_ROW), tile_buf(D_ROW), tile_buf(D_MODEL), tile_buf(D_MODEL),
                            pltpu.SemaphoreType.DMA((2,)), pltpu.SemaphoreType.DMA((2,))],
        ),
        out_shape=jax.ShapeDtypeStruct((n_tok, D_MODEL), f32),
        compiler_params=pltpu.CompilerParams(dimension_semantics=("arbitrary",), vmem_limit_bytes=VMEM_LIMIT),
        name="experts",
    )(e_lo, e_hi, n_valid.astype(i32), src, src, src, hx, row(P['g_ffn']),
      P['w1'], P['w3'], P['w2'], P['w1'], P['w3'], P['w2'], *consts)
    return y


def _prepare(g_mix, w_in, conv_w, conv_b, lru_w_a, lru_b_a, lru_w_x, lru_b_x, lru_lambda, sgu_ln_g, sgu_ln_b,
             sgu_w_s, sgu_b_s, g_rec_out, g_sgu_out, w_out, g_ffn, w_router_group, b_router_group,
             w_router_expert, b_router_expert, w_exp_gate, w_exp_up, w_exp_down, g_ple, w_ple_gate, w_ple_proj,
             g_final):
    l = 0
    w = w_in[l]
    ws_t = jnp.transpose(sgu_w_s[l], (0, 2, 1))
    z = jnp.zeros((CHUNK, CHUNK), f32)
    bd = jnp.stack([jnp.block([[ws_t[h], z], [z, ws_t[h]]]) for h in range(N_SGU_HEADS)]).astype(bf16)
    w_rt = jnp.concatenate([w_router_group[l].T, w_router_expert[l].T,
                            jnp.zeros((ROUTE_ROWS - N_GROUPS - N_EXPERTS, D_MODEL), f32)], axis=0).astype(bf16)
    b_rt = jnp.concatenate([b_router_group[l], b_router_expert[l],
                            jnp.zeros((ROUTE_ROWS - N_GROUPS - N_EXPERTS,), f32)]).reshape(ROUTE_ROWS, 1)
    return dict(
        g_mix=g_mix[l], w_xr=w[:, :D_REC].astype(bf16), w_gr=w[:, D_REC:2 * D_REC].astype(bf16),
        w_uvt=w[:, 2 * D_REC:].T.astype(bf16),
        conv_w=conv_w[l], conv_b=conv_b[l],
        wg_f=_gate_blocks(lru_w_a[l, 0], lru_w_x[l, 0]), wg_b=_gate_blocks(lru_w_a[l, 1], lru_w_x[l, 1]),
        ba_f=lru_b_a[l, 0], ba_b=lru_b_a[l, 1], bx_f=lru_b_x[l, 0], bx_b=lru_b_x[l, 1],
        lam_f=lru_lambda[l, 0], lam_b=lru_lambda[l, 1],
        ln_g=sgu_ln_g[l].reshape(D_SGU, 1), ln_b=sgu_ln_b[l].reshape(D_SGU, 1),
        bd=bd, bst=jnp.repeat(sgu_b_s[l], SGU_HEAD, axis=0),
        g_rec=g_rec_out[l], g_sgu=g_sgu_out[l].reshape(D_SGU, 1),
        w_or=w_out[l, :D_REC].astype(bf16), w_os=w_out[l, D_REC:].astype(bf16),
        g_ffn=g_ffn[l], w_rt=w_rt, b_rt=b_rt,
        w1=w_exp_gate[l].astype(bf16), w3=w_exp_up[l].astype(bf16), w2=w_exp_down[l].astype(bf16),
        g_ple=g_ple[l], w_pg=w_ple_gate[l].astype(bf16), w_pp=w_ple_proj[l].astype(bf16), g_final=g_final,
    )


def kernel(x_prompt, x_sample, p_prompt, p_sample, g_mix, w_in, conv_w, conv_b, lru_w_a, lru_b_a, lru_w_x, lru_b_x, lru_lambda, sgu_ln_g, sgu_ln_b, sgu_w_s, sgu_b_s, g_rec_out, g_sgu_out, w_out, g_ffn, w_router_group, b_router_group, w_router_expert, b_router_expert, w_exp_gate, w_exp_up, w_exp_down, g_ple, w_ple_gate, w_ple_proj, g_final):
    assert w_in.shape[0] == 1, "single-layer trunk"
    P = _prepare(g_mix, w_in, conv_w, conv_b, lru_w_a, lru_b_a, lru_w_x, lru_b_x, lru_lambda, sgu_ln_g, sgu_ln_b,
                 sgu_w_s, sgu_b_s, g_rec_out, g_sgu_out, w_out, g_ffn, w_router_group, b_router_group,
                 w_router_expert, b_router_expert, w_exp_gate, w_exp_up, w_exp_down, g_ple, w_ple_gate,
                 w_ple_proj, g_final)
    outs = []
    for x, p in ((x_prompt, p_prompt), (x_sample, p_sample)):
        y = _expert_stage(_mix_trunk(x, p[0], P), P)
        outs.append(y.reshape(x.shape))
    return tuple(outs)
```

```python
import functools

import jax
import jax.numpy as jnp
from jax import lax
from jax.experimental import pallas as pl
from jax.experimental.pallas import tpu as pltpu

f32 = jnp.float32
bf16 = jnp.bfloat16

D_MODEL = 1024
D_REC = 512
N_REC_HEADS = 8
REC_HEAD = 64
CONV_W = 4
CONV_LEFT = 2
LRU_C = 8.0
D_SGU = 512
N_SGU_HEADS = 8
SGU_HEAD = 64
CHUNK = 128
N_GROUPS = 4
N_EXP_PER_GROUP = 4
N_EXPERTS = 16
D_EXPERT = 512
D_PLE = 256
EPS = 1e-6

LANES = 128
SUBLANES = 8
NB = SUBLANES
ST = CHUNK
TT = NB * ST
HALO = SUBLANES
PITCH = ST + SUBLANES
N_LANE_GROUPS = D_REC // LANES
MXU_DIM = 256
TM = 512
N_PAIRS = 6
N_CLASSES = N_GROUPS * N_PAIRS
D_ROW = D_MODEL + D_PLE + LANES
ROUTE_ROWS = 32
VMEM_LIMIT = 60 * 1024 * 1024


def _rms(x, g):
    ms = jnp.mean(x * x, axis=-1, keepdims=True)
    return x * lax.rsqrt(ms + EPS) * g


def _softplus(x):
    return jnp.maximum(x, 0.0) + jnp.log1p(jnp.exp(-jnp.abs(x)))


def _gate_ab(xc, wg_ref, ba, bx, lam):
    xcb = xc.astype(bf16)
    sp = _softplus(-lam)
    out = []
    for g in range(N_LANE_GROUPS):
        sl = slice(LANES * g, LANES * (g + 1))
        pre = jnp.dot(xcb[:, sl], wg_ref[g], preferred_element_type=f32)
        r = jax.nn.sigmoid(pre[:, :LANES] + ba[:, sl])
        i = jax.nn.sigmoid(pre[:, LANES:] + bx[:, sl])
        log_a = -LRU_C * r * sp[:, sl]
        a = jnp.exp(log_a)
        z = jnp.tanh(-log_a) * (1.0 + a * a)
        mult = jnp.where(z > 0.0, z * lax.rsqrt(z), 0.0)
        out.append((a, mult * (i * xc[:, sl])))
    return out


def _store_ab(ab, a_ref, b_ref):
    for g, (a, b) in enumerate(ab):
        for s in range(NB):
            a_ref[g, s * PITCH:s * PITCH + ST, :] = a[s * ST:(s + 1) * ST, :]
            b_ref[g, s * PITCH:s * PITCH + ST, :] = b[s * ST:(s + 1) * ST, :]


def _scan(a_ref, b_ref, h_ref, carry_ref, reverse):
    def step(k, hs):
        t = (ST - 1 - k) if reverse else k
        new = []
        for g in range(N_LANE_GROUPS):
            a = a_ref[g, pl.ds(t, NB, stride=PITCH), :]
            b = b_ref[g, pl.ds(t, NB, stride=PITCH), :]
            h = a * hs[g] + b
            h_ref[g, pl.ds(t, NB, stride=PITCH), :] = h
            new.append(h)
        return tuple(new)

    init = tuple(carry_ref[:, LANES * g:LANES * (g + 1)] for g in range(N_LANE_GROUPS))
    hs = lax.fori_loop(0, ST, step, init, unroll=8)
    for g in range(N_LANE_GROUPS):
        carry_ref[:, LANES * g:LANES * (g + 1)] = hs[g]


def _bwd_kernel(x_ref, xp_ref, xn_ref, gmix_ref, wxr_ref, cw_ref, cb_ref, wg_ref, ba_ref, bx_ref, lam_ref,
                xc_ref, hb_ref, ext_ref, a_ref, b_ref, h_ref, carry_ref):
    j = pl.program_id(1)
    n_t = pl.num_programs(1)
    tb = n_t - 1 - j

    @pl.when(j == 0)
    def _():
        carry_ref[...] = jnp.zeros_like(carry_ref)

    gm = gmix_ref[...]

    def proj(xv):
        return jnp.dot(_rms(xv, gm).astype(bf16), wxr_ref[...], preferred_element_type=f32)

    xr = proj(x_ref[...].reshape(TT, D_MODEL))
    ext_ref[:, HALO:HALO + ST, :] = xr.reshape(NB, ST, D_REC)
    hp = proj(xp_ref[...].reshape(NB * HALO, D_MODEL)).reshape(NB, HALO, D_REC)
    ext_ref[:, 0:HALO, :] = jnp.where(tb > 0, hp, 0.0)
    hn = proj(xn_ref[...].reshape(NB * HALO, D_MODEL)).reshape(NB, HALO, D_REC)
    ext_ref[:, HALO + ST:2 * HALO + ST, :] = jnp.where(tb < n_t - 1, hn, 0.0)

    cw = cw_ref[...]
    xc = None
    for k in range(CONV_W):
        off = HALO - CONV_LEFT + k
        term = ext_ref[:, off:off + ST, :] * cw[k:k + 1, :]
        xc = term if xc is None else xc + term
    xc = xc + cb_ref[...]
    xc_ref[...] = xc

    ab = _gate_ab(xc.reshape(TT, D_REC), wg_ref, ba_ref[...], bx_ref[...], lam_ref[...])
    _store_ab(ab, a_ref, b_ref)
    _scan(a_ref, b_ref, h_ref, carry_ref, reverse=True)
    for g in range(N_LANE_GROUPS):
        for s in range(NB):
            hb_ref[s, :, LANES * g:LANES * (g + 1)] = h_ref[g, s * PITCH:s * PITCH + ST, :]


def _route(logits):
    gl = [logits[i:i + 1, :] for i in range(N_GROUPS)]
    gm = jnp.maximum(jnp.maximum(gl[0], gl[1]), jnp.maximum(gl[2], gl[3]))
    sg = sum(jnp.exp(v - gm) for v in gl)
    g_top = 1.0 / sg
    gidx = jnp.where(gl[0] == gm, 0, jnp.where(gl[1] == gm, 1, jnp.where(gl[2] == gm, 2, 3)))
    el = []
    for k in range(N_EXP_PER_GROUP):
        rows = [logits[N_GROUPS + N_EXP_PER_GROUP * g + k:N_GROUPS + N_EXP_PER_GROUP * g + k + 1, :]
                for g in range(N_GROUPS)]
        el.append(jnp.where(gidx == 0, rows[0], jnp.where(gidx == 1, rows[1], jnp.where(gidx == 2, rows[2], rows[3]))))
    em = jnp.maximum(jnp.maximum(el[0], el[1]), jnp.maximum(el[2], el[3]))
    ee = [jnp.exp(v - em) for v in el]
    se = ee[0] + ee[1] + ee[2] + ee[3]
    p = [v / se for v in ee]
    p1 = jnp.maximum(jnp.maximum(p[0], p[1]), jnp.maximum(p[2], p[3]))
    i1 = jnp.where(p[0] == p1, 0, jnp.where(p[1] == p1, 1, jnp.where(p[2] == p1, 2, 3)))
    q = [jnp.where(i1 == k, -1.0, p[k]) for k in range(N_EXP_PER_GROUP)]
    p2 = jnp.maximum(jnp.maximum(q[0], q[1]), jnp.maximum(q[2], q[3]))
    i2 = jnp.where(q[0] == p2, 0, jnp.where(q[1] == p2, 1, jnp.where(q[2] == p2, 2, 3)))
    norm = p1 + p2
    w1 = g_top * (p1 / norm)
    w2 = g_top * (p2 / norm)
    first_is_lo = i1 < i2
    lo = jnp.where(first_is_lo, i1, i2)
    hi = jnp.where(first_is_lo, i2, i1)
    w_lo = jnp.where(first_is_lo, w1, w2)
    w_hi = jnp.where(first_is_lo, w2, w1)
    base = jnp.where(lo == 0, 0, jnp.where(lo == 1, 3, 5))
    cls = gidx * N_PAIRS + base + hi - lo - 1
    return cls.astype(f32), w_lo, w_hi


def _fwd_kernel(x_ref, xc_ref, hb_ref, p_ref, gmix_ref, wgr_ref, wuvt_ref, wg_ref, ba_ref, bx_ref, lam_ref,
                lng_ref, lnb_ref, bd_ref, bst_ref, grec_ref, gsgu_ref, wor_ref, wos_ref, gffn_ref, wrt_ref, brt_ref,
                hx_ref, cls_ref, a_ref, b_ref, h_ref, carry_ref):
    j = pl.program_id(1)

    @pl.when(j == 0)
    def _():
        carry_ref[...] = jnp.zeros_like(carry_ref)

    x2 = x_ref[...].reshape(TT, D_MODEL)
    n = _rms(x2, gmix_ref[...]).astype(bf16)

    ab = _gate_ab(xc_ref[...].reshape(TT, D_REC), wg_ref, ba_ref[...], bx_ref[...], lam_ref[...])
    _store_ab(ab, a_ref, b_ref)
    _scan(a_ref, b_ref, h_ref, carry_ref, reverse=False)
    hf = jnp.concatenate(
        [jnp.concatenate([h_ref[g, s * PITCH:s * PITCH + ST, :] for s in range(NB)], axis=0)
         for g in range(N_LANE_GROUPS)], axis=1)
    gr = jnp.dot(n, wgr_ref[...], preferred_element_type=f32)
    y_rec = (hf + hb_ref[...].reshape(TT, D_REC)) * jax.nn.gelu(gr)
    rec_n = _rms(y_rec, grec_ref[...]).astype(bf16)

    zt = lax.dot_general(wuvt_ref[...], n, (((1,), (1,)), ((), ())), preferred_element_type=f32)
    ut = jax.nn.gelu(zt[:D_SGU, :])
    vt = jax.nn.gelu(zt[D_SGU:, :])
    mu = jnp.mean(vt, axis=0, keepdims=True)
    vc = vt - mu
    var = jnp.mean(vc * vc, axis=0, keepdims=True)
    vn = (vc * lax.rsqrt(var + EPS) * lng_ref[...] + lnb_ref[...]).astype(bf16)
    n_slab = TT // MXU_DIM
    heads = []
    for h in range(N_SGU_HEADS):
        rows = vn[SGU_HEAD * h:SGU_HEAD * (h + 1), :]
        lhs = jnp.concatenate([rows[:, MXU_DIM * c:MXU_DIM * (c + 1)] for c in range(n_slab)], axis=0)
        res = jnp.dot(lhs, bd_ref[h], preferred_element_type=f32)
        heads.append(jnp.concatenate([res[SGU_HEAD * c:SGU_HEAD * (c + 1), :] for c in range(n_slab)], axis=1))
    mixed = jnp.concatenate(heads, axis=0) + jnp.concatenate([bst_ref[...]] * (TT // CHUNK), axis=1)
    yst = ut * mixed
    ms = jnp.mean(yst * yst, axis=0, keepdims=True)
    sgu_n = (yst * lax.rsqrt(ms + EPS) * gsgu_ref[...]).T.astype(bf16)

    delta = (jnp.dot(rec_n, wor_ref[...], preferred_element_type=f32)
             + jnp.dot(sgu_n, wos_ref[...], preferred_element_type=f32))
    h1 = x2 + delta
    hx_ref[:, :, 0:D_MODEL] = h1.reshape(NB, ST, D_MODEL)
    hx_ref[:, :, D_MODEL:D_MODEL + D_PLE] = p_ref[...]

    n2 = _rms(h1, gffn_ref[...]).astype(bf16)
    logits = lax.dot_general(wrt_ref[...], n2, (((1,), (1,)), ((), ())), preferred_element_type=f32) + brt_ref[...]
    cls, w_lo, w_hi = _route(logits)
    rt = jnp.concatenate([cls, w_lo, w_hi, jnp.zeros((LANES - 3, TT), f32)], axis=0).T
    hx_ref[:, :, D_MODEL + D_PLE:D_ROW] = rt.reshape(NB, ST, LANES)
    cls_ref[0] = jnp.concatenate([cls, jnp.zeros((SUBLANES - 1, TT), f32)], axis=0)


def _expert_kernel(elo_ref, ehi_ref, nval_ref, ci_ref, ni_ref, pi_ref, hx_hbm, gffn_ref,
                   w1lo_ref, w3lo_ref, w2lo_ref, w1hi_ref, w3hi_ref, w2hi_ref,
                   gple_ref, wpg_ref, wpp_ref, gfin_ref, y_hbm, xbuf0, xbuf1, ybuf0, ybuf1, gsem, ssem):
    i = pl.program_id(0)
    n = pl.num_programs(0)
    cnt = nval_ref[i]
    nxt = jnp.minimum(i + 1, n - 1)
    n_grp = TM // SUBLANES

    def gather_row(i_ref, buf, par, r):
        return pltpu.make_async_copy(hx_hbm.at[pl.ds(i_ref[0, 0, r], 1)], buf.at[pl.ds(r, 1)], gsem.at[par])

    def scatter_row(i_ref, buf, par, r):
        return pltpu.make_async_copy(buf.at[pl.ds(r, 1)], y_hbm.at[pl.ds(i_ref[0, 0, r], 1)], ssem.at[par])

    def wait_tile_in(buf, par):
        pltpu.make_async_copy(hx_hbm.at[pl.ds(0, TM)], buf, gsem.at[par]).wait()

    def wait_tile_out(buf, par):
        pltpu.make_async_copy(buf, y_hbm.at[pl.ds(0, TM)], ssem.at[par]).wait()

    def expert(n2, w1_ref, w3_ref, w2_ref):
        a = jnp.dot(n2, w1_ref[0], preferred_element_type=f32)
        b = jnp.dot(n2, w3_ref[0], preferred_element_type=f32)
        hdn = (jax.nn.silu(a) * b).astype(bf16)
        return jnp.dot(hdn, w2_ref[0], preferred_element_type=f32)

    def compute(xb, yb):
        xrow = xb[...]
        xs = xrow[:, 0:D_MODEL]
        ws = xrow[:, D_MODEL + D_PLE:D_ROW]
        n2 = _rms(xs, gffn_ref[...]).astype(bf16)
        out = ws[:, 1:2] * expert(n2, w1lo_ref, w3lo_ref, w2lo_ref)
        out = out + ws[:, 2:3] * expert(n2, w1hi_ref, w3hi_ref, w2hi_ref)
        h2 = xs + out
        n3 = _rms(h2, gple_ref[...]).astype(bf16)
        gate = jax.nn.sigmoid(jnp.dot(n3, wpg_ref[...], preferred_element_type=f32))
        ps = xrow[:, D_MODEL:D_MODEL + D_PLE]
        proj = jnp.dot(ps.astype(bf16), wpp_ref[...], preferred_element_type=f32)
        h3 = h2 + gate * proj
        yb[...] = _rms(h3, gfin_ref[...])

    def step(par):
        xb, xo = (xbuf0, xbuf1) if par == 0 else (xbuf1, xbuf0)
        yb, yo = (ybuf0, ybuf1) if par == 0 else (ybuf1, ybuf0)

        @pl.when(i == 0)
        def _():
            def body(k, c):
                for u in range(SUBLANES):
                    gather_row(ci_ref, xb, par, k * SUBLANES + u).start(priority=u % 2)
                return c
            lax.fori_loop(0, n_grp, body, 0)
            yo[...] = jnp.zeros_like(yo)

        wait_tile_in(xb, par)

        @pl.when(i >= 1)
        def _():
            wait_tile_out(yb, par)

        for r in range(TM):
            gather_row(ni_ref, xo, 1 - par, r).start(priority=r % 2)
        for r in range(TM):
            scatter_row(pi_ref, yo, 1 - par, r).start(priority=r % 2)
        compute(xb, yb)

        @pl.when((i == n - 1) | (nval_ref[nxt] == 0))
        def _():
            wait_tile_in(xo, 1 - par)
            wait_tile_out(yo, 1 - par)

            def send(r, c):
                scatter_row(ci_ref, yb, par, r).start()
                return c
            lax.fori_loop(0, cnt, send, 0)

            def drain(r, c):
                scatter_row(ci_ref, yb, par, 0).wait()
                return c
            lax.fori_loop(0, cnt, drain, 0)

    @pl.when(cnt > 0)
    def _():
        @pl.when(i % 2 == 0)
        def _():
            step(0)

        @pl.when(i % 2 == 1)
        def _():
            step(1)


def _const_spec(shape):
    nd = len(shape)
    return pl.BlockSpec(shape, lambda *_: (0,) * nd)


def _gate_blocks(w_a, w_x):
    def pair(w):
        z = jnp.zeros((REC_HEAD, REC_HEAD), w.dtype)
        return jnp.stack([jnp.block([[w[2 * g], z], [z, w[2 * g + 1]]]) for g in range(N_LANE_GROUPS)])
    return jnp.concatenate([pair(w_a), pair(w_x)], axis=-1).astype(bf16)


def _mix_trunk(x, p, P):
    bsz, seq, _ = x.shape
    assert bsz % NB == 0 and seq % ST == 0
    n_g, n_t = bsz // NB, seq // ST
    halo_blocks = seq // HALO
    row = lambda v: v.reshape(1, -1)

    x_spec_b = pl.BlockSpec((NB, ST, D_MODEL), lambda g, j: (g, n_t - 1 - j, 0))
    xp_spec = pl.BlockSpec((NB, HALO, D_MODEL),
                           lambda g, j: (g, jnp.maximum((n_t - 1 - j) * (ST // HALO) - 1, 0), 0))
    xn_spec = pl.BlockSpec((NB, HALO, D_MODEL),
                           lambda g, j: (g, jnp.minimum((n_t - j) * (ST // HALO), halo_blocks - 1), 0))
    rec_spec_b = pl.BlockSpec((NB, ST, D_REC), lambda g, j: (g, n_t - 1 - j, 0))
    scan_scratch = [pltpu.VMEM((N_LANE_GROUPS, NB * PITCH, LANES), f32),
                    pltpu.VMEM((N_LANE_GROUPS, NB * PITCH, LANES), f32),
                    pltpu.VMEM((N_LANE_GROUPS, NB * PITCH, LANES), f32),
                    pltpu.VMEM((NB, D_REC), f32)]

    bwd_w = [row(P['g_mix']), P['w_xr'], P['conv_w'], row(P['conv_b']), P['wg_b'],
             row(P['ba_b']), row(P['bx_b']), row(P['lam_b'])]
    xc, hb = pl.pallas_call(
        _bwd_kernel,
        grid=(n_g, n_t),
        in_specs=[x_spec_b, xp_spec, xn_spec] + [_const_spec(w.shape) for w in bwd_w],
        out_specs=[rec_spec_b, rec_spec_b],
        out_shape=[jax.ShapeDtypeStruct((bsz, seq, D_REC), f32)] * 2,
        scratch_shapes=[pltpu.VMEM((NB, ST + 2 * HALO, D_REC), f32)] + scan_scratch,
        compiler_params=pltpu.CompilerParams(dimension_semantics=("arbitrary", "arbitrary"),
                                             vmem_limit_bytes=VMEM_LIMIT),
        name="bwd_scan",
    )(x, x, x, *bwd_w)

    tile_spec = lambda d: pl.BlockSpec((NB, ST, d), lambda g, j: (g, j, 0))
    fwd_w = [row(P['g_mix']), P['w_gr'], P['w_uvt'], P['wg_f'], row(P['ba_f']), row(P['bx_f']), row(P['lam_f']),
             P['ln_g'], P['ln_b'], P['bd'], P['bst'], row(P['g_rec']), P['g_sgu'], P['w_or'], P['w_os'],
             row(P['g_ffn']), P['w_rt'], P['b_rt']]
    hx, cls_t = pl.pallas_call(
        _fwd_kernel,
        grid=(n_g, n_t),
        in_specs=[tile_spec(D_MODEL), tile_spec(D_REC), tile_spec(D_REC), tile_spec(D_PLE)]
                 + [_const_spec(w.shape) for w in fwd_w],
        out_specs=[tile_spec(D_ROW), pl.BlockSpec((1, SUBLANES, TT), lambda g, j: (g * n_t + j, 0, 0))],
        out_shape=[jax.ShapeDtypeStruct((bsz, seq, D_ROW), f32),
                   jax.ShapeDtypeStruct((n_g * n_t, SUBLANES, TT), f32)],
        scratch_shapes=scan_scratch,
        compiler_params=pltpu.CompilerParams(dimension_semantics=("arbitrary", "arbitrary"),
                                             vmem_limit_bytes=VMEM_LIMIT),
        name="fwd_mix",
    )(x, xc, hb, p, *fwd_w)
    cls = cls_t[:, 0, :].reshape(n_g, n_t, NB, ST).transpose(0, 2, 1, 3).reshape(bsz * seq)
    return hx.reshape(bsz * seq, D_ROW), cls.astype(jnp.int32)


def _expert_stage(hx, cls, P):
    n_tok = hx.shape[0]
    assert n_tok % TM == 0
    n_tiles = n_tok // TM + N_CLASSES
    assert n_tok // TM > N_CLASSES
    i32 = jnp.int32

    class_ids = jnp.arange(N_CLASSES, dtype=i32)
    counts = jnp.sum((cls[:, None] == class_ids[None, :]).astype(i32), axis=0)
    order = jnp.argsort(cls, stable=True).astype(i32)
    dense_start = jnp.cumsum(counts) - counts
    full_c, rem_c = counts // TM, counts % TM
    cum_part = jnp.cumsum((rem_c > 0).astype(i32))
    cum_full = jnp.cumsum(full_c)
    n_part, n_full = cum_part[-1], cum_full[-1]
    tile_ids = jnp.arange(n_tiles, dtype=i32)
    is_part = tile_ids < n_part
    active = tile_ids < n_part + n_full
    first_reaching = lambda cum, k: jnp.minimum(
        jnp.sum((cum[None, :] <= k[:, None]).astype(i32), axis=1), N_CLASSES - 1)
    cls_part = first_reaching(cum_part, tile_ids)
    full_id = jnp.maximum(tile_ids - n_part, 0)
    cls_full = first_reaching(cum_full, full_id)
    tile_cls = jnp.where(is_part, cls_part, cls_full)
    n_valid = jnp.where(active, jnp.where(is_part, rem_c[tile_cls], TM), 0)
    in_class = jnp.where(is_part, full_c[tile_cls], full_id - (cum_full - full_c)[tile_cls]) * TM
    tile_dense = dense_start[tile_cls] + in_class
    rank_end = jnp.cumsum(n_valid)
    rank_start = rank_end - n_valid
    rows = jnp.arange(TM, dtype=i32)[None, :]
    pos = tile_dense[:, None] + rows
    ahead = jnp.minimum(rank_start[:N_CLASSES, None] + rows, n_tok - 1)
    holds = (rank_start[None, None, :] <= ahead[:, :, None]) & (ahead[:, :, None] < rank_end[None, None, :])
    ahead_pos = ahead + jnp.sum(jnp.where(holds, (tile_dense - rank_start)[None, None, :], 0), axis=2)
    pos = pos.at[:N_CLASSES].set(jnp.where(rows < n_valid[:N_CLASSES, None], pos[:N_CLASSES], ahead_pos))
    src = order[jnp.clip(pos, 0, n_tok - 1)].reshape(n_tiles, 1, TM)

    pair_lo = jnp.array([0, 0, 0, 1, 1, 2], jnp.int32)
    pair_hi = jnp.array([1, 2, 3, 2, 3, 3], jnp.int32)
    e_lo = (tile_cls // N_PAIRS) * N_EXP_PER_GROUP + pair_lo[tile_cls % N_PAIRS]
    e_hi = (tile_cls // N_PAIRS) * N_EXP_PER_GROUP + pair_hi[tile_cls % N_PAIRS]

    row = lambda v: v.reshape(1, -1)
    idx_spec = lambda f: pl.BlockSpec((1, 1, TM), f, memory_space=pltpu.SMEM)
    any_spec = pl.BlockSpec(memory_space=pl.ANY)
    lo_spec = lambda s: pl.BlockSpec((1,) + s, lambda i, elo, ehi, nv: (elo[i], 0, 0))
    hi_spec = lambda s: pl.BlockSpec((1,) + s, lambda i, elo, ehi, nv: (ehi[i], 0, 0))
    up_shape, down_shape = (D_MODEL, D_EXPERT), (D_EXPERT, D_MODEL)
    consts = [row(P['g_ple']), P['w_pg'], P['w_pp'], row(P['g_final'])]
    this_tile = lambda i, *_: (i, 0, 0)
    next_tile = lambda i, *_: (jnp.minimum(i + 1, n_tiles - 1), 0, 0)
    prev_tile = lambda i, *_: (jnp.maximum(i - 1, 0), 0, 0)
    tile_buf = lambda d: pltpu.VMEM((TM, d), f32)
    y = pl.pallas_call(
        _expert_kernel,
        grid_spec=pltpu.PrefetchScalarGridSpec(
            num_scalar_prefetch=3,
            grid=(n_tiles,),
            in_specs=[idx_spec(this_tile), idx_spec(next_tile), idx_spec(prev_tile),
                      any_spec, _const_spec((1, D_MODEL)),
                      lo_spec(up_shape), lo_spec(up_shape), lo_spec(down_shape),
                      hi_spec(up_shape), hi_spec(up_shape), hi_spec(down_shape)]
                     + [_const_spec(w.shape) for w in consts],
            out_specs=any_spec,
            scratch_shapes=[tile_buf(D_ROW), tile_buf(D_ROW), tile_buf(D_MODEL), tile_buf(D_MODEL),
                            pltpu.SemaphoreType.DMA((2,)), pltpu.SemaphoreType.DMA((2,))],
        ),
        out_shape=jax.ShapeDtypeStruct((n_tok, D_MODEL), f32),
        compiler_params=pltpu.CompilerParams(dimension_semantics=("arbitrary",), vmem_limit_bytes=VMEM_LIMIT),
        name="experts",
    )(e_lo, e_hi, n_valid.astype(i32), src, src, src, hx, row(P['g_ffn']),
      P['w1'], P['w3'], P['w2'], P['w1'], P['w3'], P['w2'], *consts)
    return y


def _prepare(g_mix, w_in, conv_w, conv_b, lru_w_a, lru_b_a, lru_w_x, lru_b_x, lru_lambda, sgu_ln_g, sgu_ln_b,
             sgu_w_s, sgu_b_s, g_rec_out, g_sgu_out, w_out, g_ffn, w_router_group, b_router_group,
             w_router_expert, b_router_expert, w_exp_gate, w_exp_up, w_exp_down, g_ple, w_ple_gate, w_ple_proj,
             g_final):
    l = 0
    w = w_in[l]
    ws_t = jnp.transpose(sgu_w_s[l], (0, 2, 1))
    z = jnp.zeros((CHUNK, CHUNK), f32)
    bd = jnp.stack([jnp.block([[ws_t[h], z], [z, ws_t[h]]]) for h in range(N_SGU_HEADS)]).astype(bf16)
    w_rt = jnp.concatenate([w_router_group[l].T, w_router_expert[l].T,
                            jnp.zeros((ROUTE_ROWS - N_GROUPS - N_EXPERTS, D_MODEL), f32)], axis=0).astype(bf16)
    b_rt = jnp.concatenate([b_router_group[l], b_router_expert[l],
                            jnp.zeros((ROUTE_ROWS - N_GROUPS - N_EXPERTS,), f32)]).reshape(ROUTE_ROWS, 1)
    return dict(
        g_mix=g_mix[l], w_xr=w[:, :D_REC].astype(bf16), w_gr=w[:, D_REC:2 * D_REC].astype(bf16),
        w_uvt=w[:, 2 * D_REC:].T.astype(bf16),
        conv_w=conv_w[l], conv_b=conv_b[l],
        wg_f=_gate_blocks(lru_w_a[l, 0], lru_w_x[l, 0]), wg_b=_gate_blocks(lru_w_a[l, 1], lru_w_x[l, 1]),
        ba_f=lru_b_a[l, 0], ba_b=lru_b_a[l, 1], bx_f=lru_b_x[l, 0], bx_b=lru_b_x[l, 1],
        lam_f=lru_lambda[l, 0], lam_b=lru_lambda[l, 1],
        ln_g=sgu_ln_g[l].reshape(D_SGU, 1), ln_b=sgu_ln_b[l].reshape(D_SGU, 1),
        bd=bd, bst=jnp.repeat(sgu_b_s[l], SGU_HEAD, axis=0),
        g_rec=g_rec_out[l], g_sgu=g_sgu_out[l].reshape(D_SGU, 1),
        w_or=w_out[l, :D_REC].astype(bf16), w_os=w_out[l, D_REC:].astype(bf16),
        g_ffn=g_ffn[l], w_rt=w_rt, b_rt=b_rt,
        w1=w_exp_gate[l].astype(bf16), w3=w_exp_up[l].astype(bf16), w2=w_exp_down[l].astype(bf16),
        g_ple=g_ple[l], w_pg=w_ple_gate[l].astype(bf16), w_pp=w_ple_proj[l].astype(bf16), g_final=g_final,
    )


def kernel(x_prompt, x_sample, p_prompt, p_sample, g_mix, w_in, conv_w, conv_b, lru_w_a, lru_b_a, lru_w_x, lru_b_x, lru_lambda, sgu_ln_g, sgu_ln_b, sgu_w_s, sgu_b_s, g_rec_out, g_sgu_out, w_out, g_ffn, w_router_group, b_router_group, w_router_expert, b_router_expert, w_exp_gate, w_exp_up, w_exp_down, g_ple, w_ple_gate, w_ple_proj, g_final):
    assert w_in.shape[0] == 1, "single-layer trunk"
    P = _prepare(g_mix, w_in, conv_w, conv_b, lru_w_a, lru_b_a, lru_w_x, lru_b_x, lru_lambda, sgu_ln_g, sgu_ln_b,
                 sgu_w_s, sgu_b_s, g_rec_out, g_sgu_out, w_out, g_ffn, w_router_group, b_router_group,
                 w_router_expert, b_router_expert, w_exp_gate, w_exp_up, w_exp_down, g_ple, w_ple_gate,
                 w_ple_proj, g_final)
    outs = []
    for x, p in ((x_prompt, p_prompt), (x_sample, p_sample)):
        y = _expert_stage(*_mix_trunk(x, p[0], P), P)
        outs.append(y.reshape(x.shape))
    return tuple(outs)
```

```python
import functools

import jax
import jax.numpy as jnp
from jax import lax
from jax.experimental import pallas as pl
from jax.experimental.pallas import tpu as pltpu

f32 = jnp.float32
bf16 = jnp.bfloat16

D_MODEL = 1024
D_REC = 512
N_REC_HEADS = 8
REC_HEAD = 64
CONV_W = 4
CONV_LEFT = 2
LRU_C = 8.0
D_SGU = 512
N_SGU_HEADS = 8
SGU_HEAD = 64
CHUNK = 128
N_GROUPS = 4
N_EXP_PER_GROUP = 4
N_EXPERTS = 16
D_EXPERT = 512
D_PLE = 256
EPS = 1e-6

LANES = 128
SUBLANES = 8
NB = SUBLANES
ST = CHUNK
TT = NB * ST
HALO = SUBLANES
PITCH = ST + SUBLANES
N_LANE_GROUPS = D_REC // LANES
MXU_DIM = 256
TM = 512
N_PAIRS = 6
N_CLASSES = N_GROUPS * N_PAIRS
D_ROW = D_MODEL + D_PLE + LANES
ROUTE_ROWS = 32
VMEM_LIMIT = 60 * 1024 * 1024


def _rms(x, g):
    ms = jnp.mean(x * x, axis=-1, keepdims=True)
    return x * lax.rsqrt(ms + EPS) * g


def _softplus(x):
    return jnp.maximum(x, 0.0) + jnp.log1p(jnp.exp(-jnp.abs(x)))


def _gate_ab(xc, wg_ref, ba, bx, lam):
    xcb = xc.astype(bf16)
    sp = _softplus(-lam)
    out = []
    for g in range(N_LANE_GROUPS):
        sl = slice(LANES * g, LANES * (g + 1))
        pre = jnp.dot(xcb[:, sl], wg_ref[g], preferred_element_type=f32)
        r = jax.nn.sigmoid(pre[:, :LANES] + ba[:, sl])
        i = jax.nn.sigmoid(pre[:, LANES:] + bx[:, sl])
        log_a = -LRU_C * r * sp[:, sl]
        a = jnp.exp(log_a)
        z = jnp.tanh(-log_a) * (1.0 + a * a)
        mult = jnp.where(z > 0.0, z * lax.rsqrt(z), 0.0)
        out.append((a, mult * (i * xc[:, sl])))
    return out


def _store_ab(ab, a_ref, b_ref):
    for g, (a, b) in enumerate(ab):
        for s in range(NB):
            a_ref[g, s * PITCH:s * PITCH + ST, :] = a[s * ST:(s + 1) * ST, :]
            b_ref[g, s * PITCH:s * PITCH + ST, :] = b[s * ST:(s + 1) * ST, :]


def _scan(a_ref, b_ref, h_ref, carry_ref, reverse):
    hs = [carry_ref[:, LANES * g:LANES * (g + 1)] for g in range(N_LANE_GROUPS)]
    for k in range(ST):
        t = (ST - 1 - k) if reverse else k
        for g in range(N_LANE_GROUPS):
            a = a_ref[g, pl.ds(t, NB, stride=PITCH), :]
            b = b_ref[g, pl.ds(t, NB, stride=PITCH), :]
            hs[g] = a * hs[g] + b
            h_ref[g, pl.ds(t, NB, stride=PITCH), :] = hs[g]
    for g in range(N_LANE_GROUPS):
        carry_ref[:, LANES * g:LANES * (g + 1)] = hs[g]


def _bwd_kernel(x_ref, xp_ref, xn_ref, gmix_ref, wxr_ref, cw_ref, cb_ref, wg_ref, ba_ref, bx_ref, lam_ref,
                xc_ref, hb_ref, ext_ref, a_ref, b_ref, h_ref, carry_ref):
    j = pl.program_id(1)
    n_t = pl.num_programs(1)
    tb = n_t - 1 - j

    @pl.when(j == 0)
    def _():
        carry_ref[...] = jnp.zeros_like(carry_ref)

    gm = gmix_ref[...]

    def proj(xv):
        return jnp.dot(_rms(xv, gm).astype(bf16), wxr_ref[...], preferred_element_type=f32)

    xr = proj(x_ref[...].reshape(TT, D_MODEL))
    ext_ref[:, HALO:HALO + ST, :] = xr.reshape(NB, ST, D_REC)
    hp = proj(xp_ref[...].reshape(NB * HALO, D_MODEL)).reshape(NB, HALO, D_REC)
    ext_ref[:, 0:HALO, :] = jnp.where(tb > 0, hp, 0.0)
    hn = proj(xn_ref[...].reshape(NB * HALO, D_MODEL)).reshape(NB, HALO, D_REC)
    ext_ref[:, HALO + ST:2 * HALO + ST, :] = jnp.where(tb < n_t - 1, hn, 0.0)

    cw = cw_ref[...]
    xc = None
    for k in range(CONV_W):
        off = HALO - CONV_LEFT + k
        term = ext_ref[:, off:off + ST, :] * cw[k:k + 1, :]
        xc = term if xc is None else xc + term
    xc = xc + cb_ref[...]
    xc_ref[...] = xc

    ab = _gate_ab(xc.reshape(TT, D_REC), wg_ref, ba_ref[...], bx_ref[...], lam_ref[...])
    _store_ab(ab, a_ref, b_ref)
    _scan(a_ref, b_ref, h_ref, carry_ref, reverse=True)
    for g in range(N_LANE_GROUPS):
        for s in range(NB):
            hb_ref[s, :, LANES * g:LANES * (g + 1)] = h_ref[g, s * PITCH:s * PITCH + ST, :]


def _route(logits):
    gl = [logits[i:i + 1, :] for i in range(N_GROUPS)]
    gm = jnp.maximum(jnp.maximum(gl[0], gl[1]), jnp.maximum(gl[2], gl[3]))
    sg = sum(jnp.exp(v - gm) for v in gl)
    g_top = 1.0 / sg
    gidx = jnp.where(gl[0] == gm, 0, jnp.where(gl[1] == gm, 1, jnp.where(gl[2] == gm, 2, 3)))
    el = []
    for k in range(N_EXP_PER_GROUP):
        rows = [logits[N_GROUPS + N_EXP_PER_GROUP * g + k:N_GROUPS + N_EXP_PER_GROUP * g + k + 1, :]
                for g in range(N_GROUPS)]
        el.append(jnp.where(gidx == 0, rows[0], jnp.where(gidx == 1, rows[1], jnp.where(gidx == 2, rows[2], rows[3]))))
    em = jnp.maximum(jnp.maximum(el[0], el[1]), jnp.maximum(el[2], el[3]))
    ee = [jnp.exp(v - em) for v in el]
    se = ee[0] + ee[1] + ee[2] + ee[3]
    p = [v / se for v in ee]
    p1 = jnp.maximum(jnp.maximum(p[0], p[1]), jnp.maximum(p[2], p[3]))
    i1 = jnp.where(p[0] == p1, 0, jnp.where(p[1] == p1, 1, jnp.where(p[2] == p1, 2, 3)))
    q = [jnp.where(i1 == k, -1.0, p[k]) for k in range(N_EXP_PER_GROUP)]
    p2 = jnp.maximum(jnp.maximum(q[0], q[1]), jnp.maximum(q[2], q[3]))
    i2 = jnp.where(q[0] == p2, 0, jnp.where(q[1] == p2, 1, jnp.where(q[2] == p2, 2, 3)))
    norm = p1 + p2
    w1 = g_top * (p1 / norm)
    w2 = g_top * (p2 / norm)
    first_is_lo = i1 < i2
    lo = jnp.where(first_is_lo, i1, i2)
    hi = jnp.where(first_is_lo, i2, i1)
    w_lo = jnp.where(first_is_lo, w1, w2)
    w_hi = jnp.where(first_is_lo, w2, w1)
    base = jnp.where(lo == 0, 0, jnp.where(lo == 1, 3, 5))
    cls = gidx * N_PAIRS + base + hi - lo - 1
    return cls.astype(f32), w_lo, w_hi


def _fwd_kernel(x_ref, xc_ref, hb_ref, p_ref, gmix_ref, wgr_ref, wuvt_ref, wg_ref, ba_ref, bx_ref, lam_ref,
                lng_ref, lnb_ref, bd_ref, bst_ref, grec_ref, gsgu_ref, wor_ref, wos_ref, gffn_ref, wrt_ref, brt_ref,
                hx_ref, cls_ref, a_ref, b_ref, h_ref, carry_ref):
    j = pl.program_id(1)

    @pl.when(j == 0)
    def _():
        carry_ref[...] = jnp.zeros_like(carry_ref)

    x2 = x_ref[...].reshape(TT, D_MODEL)
    n = _rms(x2, gmix_ref[...]).astype(bf16)

    ab = _gate_ab(xc_ref[...].reshape(TT, D_REC), wg_ref, ba_ref[...], bx_ref[...], lam_ref[...])
    _store_ab(ab, a_ref, b_ref)
    _scan(a_ref, b_ref, h_ref, carry_ref, reverse=False)
    hf = jnp.concatenate(
        [jnp.concatenate([h_ref[g, s * PITCH:s * PITCH + ST, :] for s in range(NB)], axis=0)
         for g in range(N_LANE_GROUPS)], axis=1)
    gr = jnp.dot(n, wgr_ref[...], preferred_element_type=f32)
    y_rec = (hf + hb_ref[...].reshape(TT, D_REC)) * jax.nn.gelu(gr)
    rec_n = _rms(y_rec, grec_ref[...]).astype(bf16)

    zt = lax.dot_general(wuvt_ref[...], n, (((1,), (1,)), ((), ())), preferred_element_type=f32)
    ut = jax.nn.gelu(zt[:D_SGU, :])
    vt = jax.nn.gelu(zt[D_SGU:, :])
    mu = jnp.mean(vt, axis=0, keepdims=True)
    vc = vt - mu
    var = jnp.mean(vc * vc, axis=0, keepdims=True)
    vn = (vc * lax.rsqrt(var + EPS) * lng_ref[...] + lnb_ref[...]).astype(bf16)
    n_slab = TT // MXU_DIM
    heads = []
    for h in range(N_SGU_HEADS):
        rows = vn[SGU_HEAD * h:SGU_HEAD * (h + 1), :]
        lhs = jnp.concatenate([rows[:, MXU_DIM * c:MXU_DIM * (c + 1)] for c in range(n_slab)], axis=0)
        res = jnp.dot(lhs, bd_ref[h], preferred_element_type=f32)
        heads.append(jnp.concatenate([res[SGU_HEAD * c:SGU_HEAD * (c + 1), :] for c in range(n_slab)], axis=1))
    mixed = jnp.concatenate(heads, axis=0) + jnp.concatenate([bst_ref[...]] * (TT // CHUNK), axis=1)
    yst = ut * mixed
    ms = jnp.mean(yst * yst, axis=0, keepdims=True)
    sgu_n = (yst * lax.rsqrt(ms + EPS) * gsgu_ref[...]).T.astype(bf16)

    delta = (jnp.dot(rec_n, wor_ref[...], preferred_element_type=f32)
             + jnp.dot(sgu_n, wos_ref[...], preferred_element_type=f32))
    h1 = x2 + delta
    hx_ref[:, :, 0:D_MODEL] = h1.reshape(NB, ST, D_MODEL)
    hx_ref[:, :, D_MODEL:D_MODEL + D_PLE] = p_ref[...]

    n2 = _rms(h1, gffn_ref[...]).astype(bf16)
    logits = lax.dot_general(wrt_ref[...], n2, (((1,), (1,)), ((), ())), preferred_element_type=f32) + brt_ref[...]
    cls, w_lo, w_hi = _route(logits)
    rt = jnp.concatenate([cls, w_lo, w_hi, jnp.zeros((LANES - 3, TT), f32)], axis=0).T
    hx_ref[:, :, D_MODEL + D_PLE:D_ROW] = rt.reshape(NB, ST, LANES)
    cls_ref[0] = jnp.concatenate([cls, jnp.zeros((SUBLANES - 1, TT), f32)], axis=0)


def _expert_kernel(elo_ref, ehi_ref, nval_ref, ci_ref, ni_ref, pi_ref, hx_hbm, gffn_ref,
                   w1lo_ref, w3lo_ref, w2lo_ref, w1hi_ref, w3hi_ref, w2hi_ref,
                   gple_ref, wpg_ref, wpp_ref, gfin_ref, y_hbm, xbuf0, xbuf1, ybuf0, ybuf1, gsem, ssem):
    i = pl.program_id(0)
    n = pl.num_programs(0)
    cnt = nval_ref[i]
    nxt = jnp.minimum(i + 1, n - 1)
    n_grp = TM // SUBLANES

    def gather_row(i_ref, buf, par, r):
        return pltpu.make_async_copy(hx_hbm.at[pl.ds(i_ref[0, 0, r], 1)], buf.at[pl.ds(r, 1)], gsem.at[par])

    def scatter_row(i_ref, buf, par, r):
        return pltpu.make_async_copy(buf.at[pl.ds(r, 1)], y_hbm.at[pl.ds(i_ref[0, 0, r], 1)], ssem.at[par])

    def wait_tile_in(buf, par):
        pltpu.make_async_copy(hx_hbm.at[pl.ds(0, TM)], buf, gsem.at[par]).wait()

    def wait_tile_out(buf, par):
        pltpu.make_async_copy(buf, y_hbm.at[pl.ds(0, TM)], ssem.at[par]).wait()

    def expert(n2, w1_ref, w3_ref, w2_ref):
        a = jnp.dot(n2, w1_ref[0], preferred_element_type=f32)
        b = jnp.dot(n2, w3_ref[0], preferred_element_type=f32)
        hdn = (jax.nn.silu(a) * b).astype(bf16)
        return jnp.dot(hdn, w2_ref[0], preferred_element_type=f32)

    def compute(xb, yb):
        xrow = xb[...]
        xs = xrow[:, 0:D_MODEL]
        ws = xrow[:, D_MODEL + D_PLE:D_ROW]
        n2 = _rms(xs, gffn_ref[...]).astype(bf16)
        out = ws[:, 1:2] * expert(n2, w1lo_ref, w3lo_ref, w2lo_ref)
        out = out + ws[:, 2:3] * expert(n2, w1hi_ref, w3hi_ref, w2hi_ref)
        h2 = xs + out
        n3 = _rms(h2, gple_ref[...]).astype(bf16)
        gate = jax.nn.sigmoid(jnp.dot(n3, wpg_ref[...], preferred_element_type=f32))
        ps = xrow[:, D_MODEL:D_MODEL + D_PLE]
        proj = jnp.dot(ps.astype(bf16), wpp_ref[...], preferred_element_type=f32)
        h3 = h2 + gate * proj
        yb[...] = _rms(h3, gfin_ref[...])

    def step(par):
        xb, xo = (xbuf0, xbuf1) if par == 0 else (xbuf1, xbuf0)
        yb, yo = (ybuf0, ybuf1) if par == 0 else (ybuf1, ybuf0)

        @pl.when(i == 0)
        def _():
            def body(k, c):
                for u in range(SUBLANES):
                    gather_row(ci_ref, xb, par, k * SUBLANES + u).start(priority=u % 2)
                return c
            lax.fori_loop(0, n_grp, body, 0)
            yo[...] = jnp.zeros_like(yo)

        wait_tile_in(xb, par)

        @pl.when(i >= 1)
        def _():
            wait_tile_out(yb, par)

        for r in range(TM):
            gather_row(ni_ref, xo, 1 - par, r).start(priority=r % 2)
        for r in range(TM):
            scatter_row(pi_ref, yo, 1 - par, r).start(priority=r % 2)
        compute(xb, yb)

        @pl.when((i == n - 1) | (nval_ref[nxt] == 0))
        def _():
            wait_tile_in(xo, 1 - par)
            wait_tile_out(yo, 1 - par)

            def send(r, c):
                scatter_row(ci_ref, yb, par, r).start()
                return c
            lax.fori_loop(0, cnt, send, 0)

            def drain(r, c):
                scatter_row(ci_ref, yb, par, 0).wait()
                return c
            lax.fori_loop(0, cnt, drain, 0)

    @pl.when(cnt > 0)
    def _():
        @pl.when(i % 2 == 0)
        def _():
            step(0)

        @pl.when(i % 2 == 1)
        def _():
            step(1)


def _const_spec(shape):
    nd = len(shape)
    return pl.BlockSpec(shape, lambda *_: (0,) * nd)


def _gate_blocks(w_a, w_x):
    def pair(w):
        z = jnp.zeros((REC_HEAD, REC_HEAD), w.dtype)
        return jnp.stack([jnp.block([[w[2 * g], z], [z, w[2 * g + 1]]]) for g in range(N_LANE_GROUPS)])
    return jnp.concatenate([pair(w_a), pair(w_x)], axis=-1).astype(bf16)


def _mix_trunk(x, p, P):
    bsz, seq, _ = x.shape
    assert bsz % NB == 0 and seq % ST == 0
    n_g, n_t = bsz // NB, seq // ST
    halo_blocks = seq // HALO
    row = lambda v: v.reshape(1, -1)

    x_spec_b = pl.BlockSpec((NB, ST, D_MODEL), lambda g, j: (g, n_t - 1 - j, 0))
    xp_spec = pl.BlockSpec((NB, HALO, D_MODEL),
                           lambda g, j: (g, jnp.maximum((n_t - 1 - j) * (ST // HALO) - 1, 0), 0))
    xn_spec = pl.BlockSpec((NB, HALO, D_MODEL),
                           lambda g, j: (g, jnp.minimum((n_t - j) * (ST // HALO), halo_blocks - 1), 0))
    rec_spec_b = pl.BlockSpec((NB, ST, D_REC), lambda g, j: (g, n_t - 1 - j, 0))
    scan_scratch = [pltpu.VMEM((N_LANE_GROUPS, NB * PITCH, LANES), f32),
                    pltpu.VMEM((N_LANE_GROUPS, NB * PITCH, LANES), f32),
                    pltpu.VMEM((N_LANE_GROUPS, NB * PITCH, LANES), f32),
                    pltpu.VMEM((NB, D_REC), f32)]

    bwd_w = [row(P['g_mix']), P['w_xr'], P['conv_w'], row(P['conv_b']), P['wg_b'],
             row(P['ba_b']), row(P['bx_b']), row(P['lam_b'])]
    xc, hb = pl.pallas_call(
        _bwd_kernel,
        grid=(n_g, n_t),
        in_specs=[x_spec_b, xp_spec, xn_spec] + [_const_spec(w.shape) for w in bwd_w],
        out_specs=[rec_spec_b, rec_spec_b],
        out_shape=[jax.ShapeDtypeStruct((bsz, seq, D_REC), f32)] * 2,
        scratch_shapes=[pltpu.VMEM((NB, ST + 2 * HALO, D_REC), f32)] + scan_scratch,
        compiler_params=pltpu.CompilerParams(dimension_semantics=("arbitrary", "arbitrary"),
                                             vmem_limit_bytes=VMEM_LIMIT),
        name="bwd_scan",
    )(x, x, x, *bwd_w)

    tile_spec = lambda d: pl.BlockSpec((NB, ST, d), lambda g, j: (g, j, 0))
    fwd_w = [row(P['g_mix']), P['w_gr'], P['w_uvt'], P['wg_f'], row(P['ba_f']), row(P['bx_f']), row(P['lam_f']),
             P['ln_g'], P['ln_b'], P['bd'], P['bst'], row(P['g_rec']), P['g_sgu'], P['w_or'], P['w_os'],
             row(P['g_ffn']), P['w_rt'], P['b_rt']]
    hx, cls_t = pl.pallas_call(
        _fwd_kernel,
        grid=(n_g, n_t),
        in_specs=[tile_spec(D_MODEL), tile_spec(D_REC), tile_spec(D_REC), tile_spec(D_PLE)]
                 + [_const_spec(w.shape) for w in fwd_w],
        out_specs=[tile_spec(D_ROW), pl.BlockSpec((1, SUBLANES, TT), lambda g, j: (g * n_t + j, 0, 0))],
        out_shape=[jax.ShapeDtypeStruct((bsz, seq, D_ROW), f32),
                   jax.ShapeDtypeStruct((n_g * n_t, SUBLANES, TT), f32)],
        scratch_shapes=scan_scratch,
        compiler_params=pltpu.CompilerParams(dimension_semantics=("arbitrary", "arbitrary"),
                                             vmem_limit_bytes=VMEM_LIMIT),
        name="fwd_mix",
    )(x, xc, hb, p, *fwd_w)
    cls = cls_t[:, 0, :].reshape(n_g, n_t, NB, ST).transpose(0, 2, 1, 3).reshape(bsz * seq)
    return hx.reshape(bsz * seq, D_ROW), cls.astype(jnp.int32)


def _expert_stage(hx, cls, P):
    n_tok = hx.shape[0]
    assert n_tok % TM == 0
    n_tiles = n_tok // TM + N_CLASSES
    assert n_tok // TM > N_CLASSES
    i32 = jnp.int32

    class_ids = jnp.arange(N_CLASSES, dtype=i32)
    counts = jnp.sum((cls[:, None] == class_ids[None, :]).astype(i32), axis=0)
    order = jnp.argsort(cls, stable=True).astype(i32)
    dense_start = jnp.cumsum(counts) - counts
    full_c, rem_c = counts // TM, counts % TM
    cum_part = jnp.cumsum((rem_c > 0).astype(i32))
    cum_full = jnp.cumsum(full_c)
    n_part, n_full = cum_part[-1], cum_full[-1]
    tile_ids = jnp.arange(n_tiles, dtype=i32)
    is_part = tile_ids < n_part
    active = tile_ids < n_part + n_full
    first_reaching = lambda cum, k: jnp.minimum(
        jnp.sum((cum[None, :] <= k[:, None]).astype(i32), axis=1), N_CLASSES - 1)
    cls_part = first_reaching(cum_part, tile_ids)
    full_id = jnp.maximum(tile_ids - n_part, 0)
    cls_full = first_reaching(cum_full, full_id)
    tile_cls = jnp.where(is_part, cls_part, cls_full)
    n_valid = jnp.where(active, jnp.where(is_part, rem_c[tile_cls], TM), 0)
    in_class = jnp.where(is_part, full_c[tile_cls], full_id - (cum_full - full_c)[tile_cls]) * TM
    tile_dense = dense_start[tile_cls] + in_class
    rank_end = jnp.cumsum(n_valid)
    rank_start = rank_end - n_valid
    rows = jnp.arange(TM, dtype=i32)[None, :]
    pos = tile_dense[:, None] + rows
    ahead = jnp.minimum(rank_start[:N_CLASSES, None] + rows, n_tok - 1)
    holds = (rank_start[None, None, :] <= ahead[:, :, None]) & (ahead[:, :, None] < rank_end[None, None, :])
    ahead_pos = ahead + jnp.sum(jnp.where(holds, (tile_dense - rank_start)[None, None, :], 0), axis=2)
    pos = pos.at[:N_CLASSES].set(jnp.where(rows < n_valid[:N_CLASSES, None], pos[:N_CLASSES], ahead_pos))
    src = order[jnp.clip(pos, 0, n_tok - 1)].reshape(n_tiles, 1, TM)

    pair_lo = jnp.array([0, 0, 0, 1, 1, 2], jnp.int32)
    pair_hi = jnp.array([1, 2, 3, 2, 3, 3], jnp.int32)
    e_lo = (tile_cls // N_PAIRS) * N_EXP_PER_GROUP + pair_lo[tile_cls % N_PAIRS]
    e_hi = (tile_cls // N_PAIRS) * N_EXP_PER_GROUP + pair_hi[tile_cls % N_PAIRS]

    row = lambda v: v.reshape(1, -1)
    idx_spec = lambda f: pl.BlockSpec((1, 1, TM), f, memory_space=pltpu.SMEM)
    any_spec = pl.BlockSpec(memory_space=pl.ANY)
    lo_spec = lambda s: pl.BlockSpec((1,) + s, lambda i, elo, ehi, nv: (elo[i], 0, 0))
    hi_spec = lambda s: pl.BlockSpec((1,) + s, lambda i, elo, ehi, nv: (ehi[i], 0, 0))
    up_shape, down_shape = (D_MODEL, D_EXPERT), (D_EXPERT, D_MODEL)
    consts = [row(P['g_ple']), P['w_pg'], P['w_pp'], row(P['g_final'])]
    this_tile = lambda i, *_: (i, 0, 0)
    next_tile = lambda i, *_: (jnp.minimum(i + 1, n_tiles - 1), 0, 0)
    prev_tile = lambda i, *_: (jnp.maximum(i - 1, 0), 0, 0)
    tile_buf = lambda d: pltpu.VMEM((TM, d), f32)
    y = pl.pallas_call(
        _expert_kernel,
        grid_spec=pltpu.PrefetchScalarGridSpec(
            num_scalar_prefetch=3,
            grid=(n_tiles,),
            in_specs=[idx_spec(this_tile), idx_spec(next_tile), idx_spec(prev_tile),
                      any_spec, _const_spec((1, D_MODEL)),
                      lo_spec(up_shape), lo_spec(up_shape), lo_spec(down_shape),
                      hi_spec(up_shape), hi_spec(up_shape), hi_spec(down_shape)]
                     + [_const_spec(w.shape) for w in consts],
            out_specs=any_spec,
            scratch_shapes=[tile_buf(D_ROW), tile_buf(D_ROW), tile_buf(D_MODEL), tile_buf(D_MODEL),
                            pltpu.SemaphoreType.DMA((2,)), pltpu.SemaphoreType.DMA((2,))],
        ),
        out_shape=jax.ShapeDtypeStruct((n_tok, D_MODEL), f32),
        compiler_params=pltpu.CompilerParams(dimension_semantics=("arbitrary",), vmem_limit_bytes=VMEM_LIMIT),
        name="experts",
    )(e_lo, e_hi, n_valid.astype(i32), src, src, src, hx, row(P['g_ffn']),
      P['w1'], P['w3'], P['w2'], P['w1'], P['w3'], P['w2'], *consts)
    return y


def _prepare(g_mix, w_in, conv_w, conv_b, lru_w_a, lru_b_a, lru_w_x, lru_b_x, lru_lambda, sgu_ln_g, sgu_ln_b,
             sgu_w_s, sgu_b_s, g_rec_out, g_sgu_out, w_out, g_ffn, w_router_group, b_router_group,
             w_router_expert, b_router_expert, w_exp_gate, w_exp_up, w_exp_down, g_ple, w_ple_gate, w_ple_proj,
             g_final):
    l = 0
    w = w_in[l]
    ws_t = jnp.transpose(sgu_w_s[l], (0, 2, 1))
    z = jnp.zeros((CHUNK, CHUNK), f32)
    bd = jnp.stack([jnp.block([[ws_t[h], z], [z, ws_t[h]]]) for h in range(N_SGU_HEADS)]).astype(bf16)
    w_rt = jnp.concatenate([w_router_group[l].T, w_router_expert[l].T,
                            jnp.zeros((ROUTE_ROWS - N_GROUPS - N_EXPERTS, D_MODEL), f32)], axis=0).astype(bf16)
    b_rt = jnp.concatenate([b_router_group[l], b_router_expert[l],
                            jnp.zeros((ROUTE_ROWS - N_GROUPS - N_EXPERTS,), f32)]).reshape(ROUTE_ROWS, 1)
    return dict(
        g_mix=g_mix[l], w_xr=w[:, :D_REC].astype(bf16), w_gr=w[:, D_REC:2 * D_REC].astype(bf16),
        w_uvt=w[:, 2 * D_REC:].T.astype(bf16),
        conv_w=conv_w[l], conv_b=conv_b[l],
        wg_f=_gate_blocks(lru_w_a[l, 0], lru_w_x[l, 0]), wg_b=_gate_blocks(lru_w_a[l, 1], lru_w_x[l, 1]),
        ba_f=lru_b_a[l, 0], ba_b=lru_b_a[l, 1], bx_f=lru_b_x[l, 0], bx_b=lru_b_x[l, 1],
        lam_f=lru_lambda[l, 0], lam_b=lru_lambda[l, 1],
        ln_g=sgu_ln_g[l].reshape(D_SGU, 1), ln_b=sgu_ln_b[l].reshape(D_SGU, 1),
        bd=bd, bst=jnp.repeat(sgu_b_s[l], SGU_HEAD, axis=0),
        g_rec=g_rec_out[l], g_sgu=g_sgu_out[l].reshape(D_SGU, 1),
        w_or=w_out[l, :D_REC].astype(bf16), w_os=w_out[l, D_REC:].astype(bf16),
        g_ffn=g_ffn[l], w_rt=w_rt, b_rt=b_rt,
        w1=w_exp_gate[l].astype(bf16), w3=w_exp_up[l].astype(bf16), w2=w_exp_down[l].astype(bf16),
        g_ple=g_ple[l], w_pg=w_ple_gate[l].astype(bf16), w_pp=w_ple_proj[l].astype(bf16), g_final=g_final,
    )


def kernel(x_prompt, x_sample, p_prompt, p_sample, g_mix, w_in, conv_w, conv_b, lru_w_a, lru_b_a, lru_w_x, lru_b_x, lru_lambda, sgu_ln_g, sgu_ln_b, sgu_w_s, sgu_b_s, g_rec_out, g_sgu_out, w_out, g_ffn, w_router_group, b_router_group, w_router_expert, b_router_expert, w_exp_gate, w_exp_up, w_exp_down, g_ple, w_ple_gate, w_ple_proj, g_final):
    assert w_in.shape[0] == 1, "single-layer trunk"
    P = _prepare(g_mix, w_in, conv_w, conv_b, lru_w_a, lru_b_a, lru_w_x, lru_b_x, lru_lambda, sgu_ln_g, sgu_ln_b,
                 sgu_w_s, sgu_b_s, g_rec_out, g_sgu_out, w_out, g_ffn, w_router_group, b_router_group,
                 w_router_expert, b_router_expert, w_exp_gate, w_exp_up, w_exp_down, g_ple, w_ple_gate,
                 w_ple_proj, g_final)
    outs = []
    for x, p in ((x_prompt, p_prompt), (x_sample, p_sample)):
        y = _expert_stage(*_mix_trunk(x, p[0], P), P)
        outs.append(y.reshape(x.shape))
    return tuple(outs)
```

```python
import functools

import jax
import jax.numpy as jnp
from jax import lax
from jax.experimental import pallas as pl
from jax.experimental.pallas import tpu as pltpu

f32 = jnp.float32
bf16 = jnp.bfloat16

D_MODEL = 1024
D_REC = 512
N_REC_HEADS = 8
REC_HEAD = 64
CONV_W = 4
CONV_LEFT = 2
LRU_C = 8.0
D_SGU = 512
N_SGU_HEADS = 8
SGU_HEAD = 64
CHUNK = 128
N_GROUPS = 4
N_EXP_PER_GROUP = 4
N_EXPERTS = 16
D_EXPERT = 512
D_PLE = 256
EPS = 1e-6

LANES = 128
SUBLANES = 8
NB = SUBLANES
ST = CHUNK
TT = NB * ST
HALO = SUBLANES
PITCH = ST + SUBLANES // 2
N_LANE_GROUPS = D_REC // LANES
MXU_DIM = 256
TM = 512
N_PAIRS = 6
N_CLASSES = N_GROUPS * N_PAIRS
D_ROW = D_MODEL + D_PLE + LANES
ROUTE_ROWS = 32
VMEM_LIMIT = 60 * 1024 * 1024


def _rms(x, g):
    ms = jnp.mean(x * x, axis=-1, keepdims=True)
    return x * lax.rsqrt(ms + EPS) * g


def _softplus(x):
    return jnp.maximum(x, 0.0) + jnp.log1p(jnp.exp(-jnp.abs(x)))


def _gate_ab(xc, wg_ref, ba, bx, lam):
    xcb = xc.astype(bf16)
    sp = _softplus(-lam)
    out = []
    for g in range(N_LANE_GROUPS):
        sl = slice(LANES * g, LANES * (g + 1))
        pre = jnp.dot(xcb[:, sl], wg_ref[g], preferred_element_type=f32)
        r = jax.nn.sigmoid(pre[:, :LANES] + ba[:, sl])
        i = jax.nn.sigmoid(pre[:, LANES:] + bx[:, sl])
        log_a = -LRU_C * r * sp[:, sl]
        a = jnp.exp(log_a)
        z = jnp.tanh(-log_a) * (1.0 + a * a)
        mult = jnp.where(z > 0.0, z * lax.rsqrt(z), 0.0)
        out.append((a, mult * (i * xc[:, sl])))
    return out


def _store_ab(ab, a_ref, b_ref):
    for g, (a, b) in enumerate(ab):
        for s in range(NB):
            a_ref[g, s * PITCH:s * PITCH + ST, :] = a[s * ST:(s + 1) * ST, :]
            b_ref[g, s * PITCH:s * PITCH + ST, :] = b[s * ST:(s + 1) * ST, :]


def _scan(a_ref, b_ref, h_ref, carry_ref, reverse):
    hs = [carry_ref[:, LANES * g:LANES * (g + 1)] for g in range(N_LANE_GROUPS)]
    for k in range(ST):
        t = (ST - 1 - k) if reverse else k
        for g in range(N_LANE_GROUPS):
            a = a_ref[g, pl.ds(t, NB, stride=PITCH), :]
            b = b_ref[g, pl.ds(t, NB, stride=PITCH), :]
            hs[g] = a * hs[g] + b
            h_ref[g, pl.ds(t, NB, stride=PITCH), :] = hs[g]
    for g in range(N_LANE_GROUPS):
        carry_ref[:, LANES * g:LANES * (g + 1)] = hs[g]


def _bwd_kernel(x_ref, xp_ref, xn_ref, gmix_ref, wxr_ref, cw_ref, cb_ref, wg_ref, ba_ref, bx_ref, lam_ref,
                xc_ref, hb_ref, ext_ref, a_ref, b_ref, h_ref, carry_ref):
    j = pl.program_id(1)
    n_t = pl.num_programs(1)
    tb = n_t - 1 - j

    @pl.when(j == 0)
    def _():
        carry_ref[...] = jnp.zeros_like(carry_ref)

    gm = gmix_ref[...]

    def proj(xv):
        return jnp.dot(_rms(xv, gm).astype(bf16), wxr_ref[...], preferred_element_type=f32)

    xr = proj(x_ref[...].reshape(TT, D_MODEL))
    ext_ref[:, HALO:HALO + ST, :] = xr.reshape(NB, ST, D_REC)
    hp = proj(xp_ref[...].reshape(NB * HALO, D_MODEL)).reshape(NB, HALO, D_REC)
    ext_ref[:, 0:HALO, :] = jnp.where(tb > 0, hp, 0.0)
    hn = proj(xn_ref[...].reshape(NB * HALO, D_MODEL)).reshape(NB, HALO, D_REC)
    ext_ref[:, HALO + ST:2 * HALO + ST, :] = jnp.where(tb < n_t - 1, hn, 0.0)

    cw = cw_ref[...]
    xc = None
    for k in range(CONV_W):
        off = HALO - CONV_LEFT + k
        term = ext_ref[:, off:off + ST, :] * cw[k:k + 1, :]
        xc = term if xc is None else xc + term
    xc = xc + cb_ref[...]
    xc_ref[...] = xc

    ab = _gate_ab(xc.reshape(TT, D_REC), wg_ref, ba_ref[...], bx_ref[...], lam_ref[...])
    _store_ab(ab, a_ref, b_ref)
    _scan(a_ref, b_ref, h_ref, carry_ref, reverse=True)
    for g in range(N_LANE_GROUPS):
        for s in range(NB):
            hb_ref[s, :, LANES * g:LANES * (g + 1)] = h_ref[g, s * PITCH:s * PITCH + ST, :]


def _route(logits):
    gl = [logits[i:i + 1, :] for i in range(N_GROUPS)]
    gm = jnp.maximum(jnp.maximum(gl[0], gl[1]), jnp.maximum(gl[2], gl[3]))
    sg = sum(jnp.exp(v - gm) for v in gl)
    g_top = 1.0 / sg
    gidx = jnp.where(gl[0] == gm, 0, jnp.where(gl[1] == gm, 1, jnp.where(gl[2] == gm, 2, 3)))
    el = []
    for k in range(N_EXP_PER_GROUP):
        rows = [logits[N_GROUPS + N_EXP_PER_GROUP * g + k:N_GROUPS + N_EXP_PER_GROUP * g + k + 1, :]
                for g in range(N_GROUPS)]
        el.append(jnp.where(gidx == 0, rows[0], jnp.where(gidx == 1, rows[1], jnp.where(gidx == 2, rows[2], rows[3]))))
    em = jnp.maximum(jnp.maximum(el[0], el[1]), jnp.maximum(el[2], el[3]))
    ee = [jnp.exp(v - em) for v in el]
    se = ee[0] + ee[1] + ee[2] + ee[3]
    p = [v / se for v in ee]
    p1 = jnp.maximum(jnp.maximum(p[0], p[1]), jnp.maximum(p[2], p[3]))
    i1 = jnp.where(p[0] == p1, 0, jnp.where(p[1] == p1, 1, jnp.where(p[2] == p1, 2, 3)))
    q = [jnp.where(i1 == k, -1.0, p[k]) for k in range(N_EXP_PER_GROUP)]
    p2 = jnp.maximum(jnp.maximum(q[0], q[1]), jnp.maximum(q[2], q[3]))
    i2 = jnp.where(q[0] == p2, 0, jnp.where(q[1] == p2, 1, jnp.where(q[2] == p2, 2, 3)))
    norm = p1 + p2
    w1 = g_top * (p1 / norm)
    w2 = g_top * (p2 / norm)
    first_is_lo = i1 < i2
    lo = jnp.where(first_is_lo, i1, i2)
    hi = jnp.where(first_is_lo, i2, i1)
    w_lo = jnp.where(first_is_lo, w1, w2)
    w_hi = jnp.where(first_is_lo, w2, w1)
    base = jnp.where(lo == 0, 0, jnp.where(lo == 1, 3, 5))
    cls = gidx * N_PAIRS + base + hi - lo - 1
    return cls.astype(f32), w_lo, w_hi


def _fwd_kernel(x_ref, xc_ref, hb_ref, p_ref, gmix_ref, wgr_ref, wuvt_ref, wg_ref, ba_ref, bx_ref, lam_ref,
                lng_ref, lnb_ref, bd_ref, bst_ref, grec_ref, gsgu_ref, wor_ref, wos_ref, gffn_ref, wrt_ref, brt_ref,
                hx_ref, cls_ref, a_ref, b_ref, h_ref, carry_ref):
    j = pl.program_id(1)

    @pl.when(j == 0)
    def _():
        carry_ref[...] = jnp.zeros_like(carry_ref)

    x2 = x_ref[...].reshape(TT, D_MODEL)
    n = _rms(x2, gmix_ref[...]).astype(bf16)

    ab = _gate_ab(xc_ref[...].reshape(TT, D_REC), wg_ref, ba_ref[...], bx_ref[...], lam_ref[...])
    _store_ab(ab, a_ref, b_ref)
    _scan(a_ref, b_ref, h_ref, carry_ref, reverse=False)
    hf = jnp.concatenate(
        [jnp.concatenate([h_ref[g, s * PITCH:s * PITCH + ST, :] for s in range(NB)], axis=0)
         for g in range(N_LANE_GROUPS)], axis=1)
    gr = jnp.dot(n, wgr_ref[...], preferred_element_type=f32)
    y_rec = (hf + hb_ref[...].reshape(TT, D_REC)) * jax.nn.gelu(gr)
    rec_n = _rms(y_rec, grec_ref[...]).astype(bf16)

    zt = lax.dot_general(wuvt_ref[...], n, (((1,), (1,)), ((), ())), preferred_element_type=f32)
    ut = jax.nn.gelu(zt[:D_SGU, :])
    vt = jax.nn.gelu(zt[D_SGU:, :])
    mu = jnp.mean(vt, axis=0, keepdims=True)
    vc = vt - mu
    var = jnp.mean(vc * vc, axis=0, keepdims=True)
    vn = (vc * lax.rsqrt(var + EPS) * lng_ref[...] + lnb_ref[...]).astype(bf16)
    n_slab = TT // MXU_DIM
    heads = []
    for h in range(N_SGU_HEADS):
        rows = vn[SGU_HEAD * h:SGU_HEAD * (h + 1), :]
        lhs = jnp.concatenate([rows[:, MXU_DIM * c:MXU_DIM * (c + 1)] for c in range(n_slab)], axis=0)
        res = jnp.dot(lhs, bd_ref[h], preferred_element_type=f32)
        heads.append(jnp.concatenate([res[SGU_HEAD * c:SGU_HEAD * (c + 1), :] for c in range(n_slab)], axis=1))
    mixed = jnp.concatenate(heads, axis=0) + jnp.concatenate([bst_ref[...]] * (TT // CHUNK), axis=1)
    yst = ut * mixed
    ms = jnp.mean(yst * yst, axis=0, keepdims=True)
    sgu_n = (yst * lax.rsqrt(ms + EPS) * gsgu_ref[...]).T.astype(bf16)

    delta = (jnp.dot(rec_n, wor_ref[...], preferred_element_type=f32)
             + jnp.dot(sgu_n, wos_ref[...], preferred_element_type=f32))
    h1 = x2 + delta
    hx_ref[:, :, 0:D_MODEL] = h1.reshape(NB, ST, D_MODEL)
    hx_ref[:, :, D_MODEL:D_MODEL + D_PLE] = p_ref[...]

    n2 = _rms(h1, gffn_ref[...]).astype(bf16)
    logits = lax.dot_general(wrt_ref[...], n2, (((1,), (1,)), ((), ())), preferred_element_type=f32) + brt_ref[...]
    cls, w_lo, w_hi = _route(logits)
    rt = jnp.concatenate([cls, w_lo, w_hi, jnp.zeros((LANES - 3, TT), f32)], axis=0).T
    hx_ref[:, :, D_MODEL + D_PLE:D_ROW] = rt.reshape(NB, ST, LANES)
    cls_ref[0] = jnp.concatenate([cls, jnp.zeros((SUBLANES - 1, TT), f32)], axis=0)


def _expert_kernel(elo_ref, ehi_ref, nval_ref, ci_ref, ni_ref, pi_ref, hx_hbm, gffn_ref,
                   w1lo_ref, w3lo_ref, w2lo_ref, w1hi_ref, w3hi_ref, w2hi_ref,
                   gple_ref, wpg_ref, wpp_ref, gfin_ref, y_hbm, xbuf0, xbuf1, ybuf0, ybuf1, gsem, ssem):
    i = pl.program_id(0)
    n = pl.num_programs(0)
    cnt = nval_ref[i]
    nxt = jnp.minimum(i + 1, n - 1)
    n_grp = TM // SUBLANES

    def gather_row(i_ref, buf, par, r):
        return pltpu.make_async_copy(hx_hbm.at[pl.ds(i_ref[0, 0, r], 1)], buf.at[pl.ds(r, 1)], gsem.at[par])

    def scatter_row(i_ref, buf, par, r):
        return pltpu.make_async_copy(buf.at[pl.ds(r, 1)], y_hbm.at[pl.ds(i_ref[0, 0, r], 1)], ssem.at[par])

    def wait_tile_in(buf, par):
        pltpu.make_async_copy(hx_hbm.at[pl.ds(0, TM)], buf, gsem.at[par]).wait()

    def wait_tile_out(buf, par):
        pltpu.make_async_copy(buf, y_hbm.at[pl.ds(0, TM)], ssem.at[par]).wait()

    def expert(n2, w1_ref, w3_ref, w2_ref):
        a = jnp.dot(n2, w1_ref[0], preferred_element_type=f32)
        b = jnp.dot(n2, w3_ref[0], preferred_element_type=f32)
        hdn = (jax.nn.silu(a) * b).astype(bf16)
        return jnp.dot(hdn, w2_ref[0], preferred_element_type=f32)

    def compute(xb, yb):
        xrow = xb[...]
        xs = xrow[:, 0:D_MODEL]
        ws = xrow[:, D_MODEL + D_PLE:D_ROW]
        n2 = _rms(xs, gffn_ref[...]).astype(bf16)
        out = ws[:, 1:2] * expert(n2, w1lo_ref, w3lo_ref, w2lo_ref)
        out = out + ws[:, 2:3] * expert(n2, w1hi_ref, w3hi_ref, w2hi_ref)
        h2 = xs + out
        n3 = _rms(h2, gple_ref[...]).astype(bf16)
        gate = jax.nn.sigmoid(jnp.dot(n3, wpg_ref[...], preferred_element_type=f32))
        ps = xrow[:, D_MODEL:D_MODEL + D_PLE]
        proj = jnp.dot(ps.astype(bf16), wpp_ref[...], preferred_element_type=f32)
        h3 = h2 + gate * proj
        yb[...] = _rms(h3, gfin_ref[...])

    def step(par):
        xb, xo = (xbuf0, xbuf1) if par == 0 else (xbuf1, xbuf0)
        yb, yo = (ybuf0, ybuf1) if par == 0 else (ybuf1, ybuf0)

        @pl.when(i == 0)
        def _():
            def body(k, c):
                for u in range(SUBLANES):
                    gather_row(ci_ref, xb, par, k * SUBLANES + u).start(priority=u % 2)
                return c
            lax.fori_loop(0, n_grp, body, 0)
            yo[...] = jnp.zeros_like(yo)

        wait_tile_in(xb, par)

        @pl.when(i >= 1)
        def _():
            wait_tile_out(yb, par)

        for r in range(TM):
            gather_row(ni_ref, xo, 1 - par, r).start(priority=r % 2)
        for r in range(TM):
            scatter_row(pi_ref, yo, 1 - par, r).start(priority=r % 2)
        compute(xb, yb)

        @pl.when((i == n - 1) | (nval_ref[nxt] == 0))
        def _():
            wait_tile_in(xo, 1 - par)
            wait_tile_out(yo, 1 - par)

            def send(r, c):
                scatter_row(ci_ref, yb, par, r).start()
                return c
            lax.fori_loop(0, cnt, send, 0)

            def drain(r, c):
                scatter_row(ci_ref, yb, par, 0).wait()
                return c
            lax.fori_loop(0, cnt, drain, 0)

    @pl.when(cnt > 0)
    def _():
        @pl.when(i % 2 == 0)
        def _():
            step(0)

        @pl.when(i % 2 == 1)
        def _():
            step(1)


def _const_spec(shape):
    nd = len(shape)
    return pl.BlockSpec(shape, lambda *_: (0,) * nd)


def _gate_blocks(w_a, w_x):
    def pair(w):
        z = jnp.zeros((REC_HEAD, REC_HEAD), w.dtype)
        return jnp.stack([jnp.block([[w[2 * g], z], [z, w[2 * g + 1]]]) for g in range(N_LANE_GROUPS)])
    return jnp.concatenate([pair(w_a), pair(w_x)], axis=-1).astype(bf16)


def _mix_trunk(x, p, P):
    bsz, seq, _ = x.shape
    assert bsz % NB == 0 and seq % ST == 0
    n_g, n_t = bsz // NB, seq // ST
    halo_blocks = seq // HALO
    row = lambda v: v.reshape(1, -1)

    x_spec_b = pl.BlockSpec((NB, ST, D_MODEL), lambda g, j: (g, n_t - 1 - j, 0))
    xp_spec = pl.BlockSpec((NB, HALO, D_MODEL),
                           lambda g, j: (g, jnp.maximum((n_t - 1 - j) * (ST // HALO) - 1, 0), 0))
    xn_spec = pl.BlockSpec((NB, HALO, D_MODEL),
                           lambda g, j: (g, jnp.minimum((n_t - j) * (ST // HALO), halo_blocks - 1), 0))
    rec_spec_b = pl.BlockSpec((NB, ST, D_REC), lambda g, j: (g, n_t - 1 - j, 0))
    scan_scratch = [pltpu.VMEM((N_LANE_GROUPS, NB * PITCH, LANES), f32),
                    pltpu.VMEM((N_LANE_GROUPS, NB * PITCH, LANES), f32),
                    pltpu.VMEM((N_LANE_GROUPS, NB * PITCH, LANES), f32),
                    pltpu.VMEM((NB, D_REC), f32)]

    bwd_w = [row(P['g_mix']), P['w_xr'], P['conv_w'], row(P['conv_b']), P['wg_b'],
             row(P['ba_b']), row(P['bx_b']), row(P['lam_b'])]
    xc, hb = pl.pallas_call(
        _bwd_kernel,
        grid=(n_g, n_t),
        in_specs=[x_spec_b, xp_spec, xn_spec] + [_const_spec(w.shape) for w in bwd_w],
        out_specs=[rec_spec_b, rec_spec_b],
        out_shape=[jax.ShapeDtypeStruct((bsz, seq, D_REC), f32)] * 2,
        scratch_shapes=[pltpu.VMEM((NB, ST + 2 * HALO, D_REC), f32)] + scan_scratch,
        compiler_params=pltpu.CompilerParams(dimension_semantics=("arbitrary", "arbitrary"),
                                             vmem_limit_bytes=VMEM_LIMIT),
        name="bwd_scan",
    )(x, x, x, *bwd_w)

    tile_spec = lambda d: pl.BlockSpec((NB, ST, d), lambda g, j: (g, j, 0))
    fwd_w = [row(P['g_mix']), P['w_gr'], P['w_uvt'], P['wg_f'], row(P['ba_f']), row(P['bx_f']), row(P['lam_f']),
             P['ln_g'], P['ln_b'], P['bd'], P['bst'], row(P['g_rec']), P['g_sgu'], P['w_or'], P['w_os'],
             row(P['g_ffn']), P['w_rt'], P['b_rt']]
    hx, cls_t = pl.pallas_call(
        _fwd_kernel,
        grid=(n_g, n_t),
        in_specs=[tile_spec(D_MODEL), tile_spec(D_REC), tile_spec(D_REC), tile_spec(D_PLE)]
                 + [_const_spec(w.shape) for w in fwd_w],
        out_specs=[tile_spec(D_ROW), pl.BlockSpec((1, SUBLANES, TT), lambda g, j: (g * n_t + j, 0, 0))],
        out_shape=[jax.ShapeDtypeStruct((bsz, seq, D_ROW), f32),
                   jax.ShapeDtypeStruct((n_g * n_t, SUBLANES, TT), f32)],
        scratch_shapes=scan_scratch,
        compiler_params=pltpu.CompilerParams(dimension_semantics=("arbitrary", "arbitrary"),
                                             vmem_limit_bytes=VMEM_LIMIT),
        name="fwd_mix",
    )(x, xc, hb, p, *fwd_w)
    cls = cls_t[:, 0, :].reshape(n_g, n_t, NB, ST).transpose(0, 2, 1, 3).reshape(bsz * seq)
    return hx.reshape(bsz * seq, D_ROW), cls.astype(jnp.int32)


def _expert_stage(hx, cls, P):
    n_tok = hx.shape[0]
    assert n_tok % TM == 0
    n_tiles = n_tok // TM + N_CLASSES
    assert n_tok // TM > N_CLASSES
    i32 = jnp.int32

    class_ids = jnp.arange(N_CLASSES, dtype=i32)
    counts = jnp.sum((cls[:, None] == class_ids[None, :]).astype(i32), axis=0)
    order = jnp.argsort(cls, stable=True).astype(i32)
    dense_start = jnp.cumsum(counts) - counts
    full_c, rem_c = counts // TM, counts % TM
    cum_part = jnp.cumsum((rem_c > 0).astype(i32))
    cum_full = jnp.cumsum(full_c)
    n_part, n_full = cum_part[-1], cum_full[-1]
    tile_ids = jnp.arange(n_tiles, dtype=i32)
    is_part = tile_ids < n_part
    active = tile_ids < n_part + n_full
    first_reaching = lambda cum, k: jnp.minimum(
        jnp.sum((cum[None, :] <= k[:, None]).astype(i32), axis=1), N_CLASSES - 1)
    cls_part = first_reaching(cum_part, tile_ids)
    full_id = jnp.maximum(tile_ids - n_part, 0)
    cls_full = first_reaching(cum_full, full_id)
    tile_cls = jnp.where(is_part, cls_part, cls_full)
    n_valid = jnp.where(active, jnp.where(is_part, rem_c[tile_cls], TM), 0)
    in_class = jnp.where(is_part, full_c[tile_cls], full_id - (cum_full - full_c)[tile_cls]) * TM
    tile_dense = dense_start[tile_cls] + in_class
    rank_end = jnp.cumsum(n_valid)
    rank_start = rank_end - n_valid
    rows = jnp.arange(TM, dtype=i32)[None, :]
    pos = tile_dense[:, None] + rows
    ahead = jnp.minimum(rank_start[:N_CLASSES, None] + rows, n_tok - 1)
    holds = (rank_start[None, None, :] <= ahead[:, :, None]) & (ahead[:, :, None] < rank_end[None, None, :])
    ahead_pos = ahead + jnp.sum(jnp.where(holds, (tile_dense - rank_start)[None, None, :], 0), axis=2)
    pos = pos.at[:N_CLASSES].set(jnp.where(rows < n_valid[:N_CLASSES, None], pos[:N_CLASSES], ahead_pos))
    src = order[jnp.clip(pos, 0, n_tok - 1)].reshape(n_tiles, 1, TM)

    pair_lo = jnp.array([0, 0, 0, 1, 1, 2], jnp.int32)
    pair_hi = jnp.array([1, 2, 3, 2, 3, 3], jnp.int32)
    e_lo = (tile_cls // N_PAIRS) * N_EXP_PER_GROUP + pair_lo[tile_cls % N_PAIRS]
    e_hi = (tile_cls // N_PAIRS) * N_EXP_PER_GROUP + pair_hi[tile_cls % N_PAIRS]

    row = lambda v: v.reshape(1, -1)
    idx_spec = lambda f: pl.BlockSpec((1, 1, TM), f, memory_space=pltpu.SMEM)
    any_spec = pl.BlockSpec(memory_space=pl.ANY)
    lo_spec = lambda s: pl.BlockSpec((1,) + s, lambda i, elo, ehi, nv: (elo[i], 0, 0))
    hi_spec = lambda s: pl.BlockSpec((1,) + s, lambda i, elo, ehi, nv: (ehi[i], 0, 0))
    up_shape, down_shape = (D_MODEL, D_EXPERT), (D_EXPERT, D_MODEL)
    consts = [row(P['g_ple']), P['w_pg'], P['w_pp'], row(P['g_final'])]
    this_tile = lambda i, *_: (i, 0, 0)
    next_tile = lambda i, *_: (jnp.minimum(i + 1, n_tiles - 1), 0, 0)
    prev_tile = lambda i, *_: (jnp.maximum(i - 1, 0), 0, 0)
    tile_buf = lambda d: pltpu.VMEM((TM, d), f32)
    y = pl.pallas_call(
        _expert_kernel,
        grid_spec=pltpu.PrefetchScalarGridSpec(
            num_scalar_prefetch=3,
            grid=(n_tiles,),
            in_specs=[idx_spec(this_tile), idx_spec(next_tile), idx_spec(prev_tile),
                      any_spec, _const_spec((1, D_MODEL)),
                      lo_spec(up_shape), lo_spec(up_shape), lo_spec(down_shape),
                      hi_spec(up_shape), hi_spec(up_shape), hi_spec(down_shape)]
                     + [_const_spec(w.shape) for w in consts],
            out_specs=any_spec,
            scratch_shapes=[tile_buf(D_ROW), tile_buf(D_ROW), tile_buf(D_MODEL), tile_buf(D_MODEL),
                            pltpu.SemaphoreType.DMA((2,)), pltpu.SemaphoreType.DMA((2,))],
        ),
        out_shape=jax.ShapeDtypeStruct((n_tok, D_MODEL), f32),
        compiler_params=pltpu.CompilerParams(dimension_semantics=("arbitrary",), vmem_limit_bytes=VMEM_LIMIT),
        name="experts",
    )(e_lo, e_hi, n_valid.astype(i32), src, src, src, hx, row(P['g_ffn']),
      P['w1'], P['w3'], P['w2'], P['w1'], P['w3'], P['w2'], *consts)
    return y


def _prepare(g_mix, w_in, conv_w, conv_b, lru_w_a, lru_b_a, lru_w_x, lru_b_x, lru_lambda, sgu_ln_g, sgu_ln_b,
             sgu_w_s, sgu_b_s, g_rec_out, g_sgu_out, w_out, g_ffn, w_router_group, b_router_group,
             w_router_expert, b_router_expert, w_exp_gate, w_exp_up, w_exp_down, g_ple, w_ple_gate, w_ple_proj,
             g_final):
    l = 0
    w = w_in[l]
    ws_t = jnp.transpose(sgu_w_s[l], (0, 2, 1))
    z = jnp.zeros((CHUNK, CHUNK), f32)
    bd = jnp.stack([jnp.block([[ws_t[h], z], [z, ws_t[h]]]) for h in range(N_SGU_HEADS)]).astype(bf16)
    w_rt = jnp.concatenate([w_router_group[l].T, w_router_expert[l].T,
                            jnp.zeros((ROUTE_ROWS - N_GROUPS - N_EXPERTS, D_MODEL), f32)], axis=0).astype(bf16)
    b_rt = jnp.concatenate([b_router_group[l], b_router_expert[l],
                            jnp.zeros((ROUTE_ROWS - N_GROUPS - N_EXPERTS,), f32)]).reshape(ROUTE_ROWS, 1)
    return dict(
        g_mix=g_mix[l], w_xr=w[:, :D_REC].astype(bf16), w_gr=w[:, D_REC:2 * D_REC].astype(bf16),
        w_uvt=w[:, 2 * D_REC:].T.astype(bf16),
        conv_w=conv_w[l], conv_b=conv_b[l],
        wg_f=_gate_blocks(lru_w_a[l, 0], lru_w_x[l, 0]), wg_b=_gate_blocks(lru_w_a[l, 1], lru_w_x[l, 1]),
        ba_f=lru_b_a[l, 0], ba_b=lru_b_a[l, 1], bx_f=lru_b_x[l, 0], bx_b=lru_b_x[l, 1],
        lam_f=lru_lambda[l, 0], lam_b=lru_lambda[l, 1],
        ln_g=sgu_ln_g[l].reshape(D_SGU, 1), ln_b=sgu_ln_b[l].reshape(D_SGU, 1),
        bd=bd, bst=jnp.repeat(sgu_b_s[l], SGU_HEAD, axis=0),
        g_rec=g_rec_out[l], g_sgu=g_sgu_out[l].reshape(D_SGU, 1),
        w_or=w_out[l, :D_REC].astype(bf16), w_os=w_out[l, D_REC:].astype(bf16),
        g_ffn=g_ffn[l], w_rt=w_rt, b_rt=b_rt,
        w1=w_exp_gate[l].astype(bf16), w3=w_exp_up[l].astype(bf16), w2=w_exp_down[l].astype(bf16),
        g_ple=g_ple[l], w_pg=w_ple_gate[l].astype(bf16), w_pp=w_ple_proj[l].astype(bf16), g_final=g_final,
    )


def kernel(x_prompt, x_sample, p_prompt, p_sample, g_mix, w_in, conv_w, conv_b, lru_w_a, lru_b_a, lru_w_x, lru_b_x, lru_lambda, sgu_ln_g, sgu_ln_b, sgu_w_s, sgu_b_s, g_rec_out, g_sgu_out, w_out, g_ffn, w_router_group, b_router_group, w_router_expert, b_router_expert, w_exp_gate, w_exp_up, w_exp_down, g_ple, w_ple_gate, w_ple_proj, g_final):
    assert w_in.shape[0] == 1, "single-layer trunk"
    P = _prepare(g_mix, w_in, conv_w, conv_b, lru_w_a, lru_b_a, lru_w_x, lru_b_x, lru_lambda, sgu_ln_g, sgu_ln_b,
                 sgu_w_s, sgu_b_s, g_rec_out, g_sgu_out, w_out, g_ffn, w_router_group, b_router_group,
                 w_router_expert, b_router_expert, w_exp_gate, w_exp_up, w_exp_down, g_ple, w_ple_gate,
                 w_ple_proj, g_final)
    outs = []
    for x, p in ((x_prompt, p_prompt), (x_sample, p_sample)):
        y = _expert_stage(*_mix_trunk(x, p[0], P), P)
        outs.append(y.reshape(x.shape))
    return tuple(outs)
```

```python
import functools

import jax
import jax.numpy as jnp
from jax import lax
from jax.experimental import pallas as pl
from jax.experimental.pallas import tpu as pltpu

f32 = jnp.float32
bf16 = jnp.bfloat16

D_MODEL = 1024
D_REC = 512
N_REC_HEADS = 8
REC_HEAD = 64
CONV_W = 4
CONV_LEFT = 2
LRU_C = 8.0
D_SGU = 512
N_SGU_HEADS = 8
SGU_HEAD = 64
CHUNK = 128
N_GROUPS = 4
N_EXP_PER_GROUP = 4
N_EXPERTS = 16
D_EXPERT = 512
D_PLE = 256
EPS = 1e-6

LANES = 128
SUBLANES = 8
NB = SUBLANES
ST = CHUNK
TT = NB * ST
HALO = SUBLANES
PITCH = ST + SUBLANES // 2
N_LANE_GROUPS = D_REC // LANES
MXU_DIM = 256
TM = 512
N_PAIRS = 6
N_CLASSES = N_GROUPS * N_PAIRS
D_ROW = D_MODEL + D_PLE + LANES
ROUTE_ROWS = 32
VMEM_LIMIT = 60 * 1024 * 1024


def _rms(x, g):
    ms = jnp.mean(x * x, axis=-1, keepdims=True)
    return x * lax.rsqrt(ms + EPS) * g


def _softplus(x):
    return jnp.maximum(x, 0.0) + jnp.log1p(jnp.exp(-jnp.abs(x)))


def _gate_ab(xc, wg_ref, ba, bx, lam):
    xcb = xc.astype(bf16)
    sp = _softplus(-lam)
    out = []
    for g in range(N_LANE_GROUPS):
        sl = slice(LANES * g, LANES * (g + 1))
        pre = jnp.dot(xcb[:, sl], wg_ref[g], preferred_element_type=f32)
        r = jax.nn.sigmoid(pre[:, :LANES] + ba[:, sl])
        i = jax.nn.sigmoid(pre[:, LANES:] + bx[:, sl])
        log_a = -LRU_C * r * sp[:, sl]
        a = jnp.exp(log_a)
        z = jnp.tanh(-log_a) * (1.0 + a * a)
        mult = jnp.where(z > 0.0, z * lax.rsqrt(z), 0.0)
        out.append((a, mult * (i * xc[:, sl])))
    return out


def _store_ab(ab, a_ref, b_ref):
    for g, (a, b) in enumerate(ab):
        for s in range(NB):
            a_ref[g, s * PITCH:s * PITCH + ST, :] = a[s * ST:(s + 1) * ST, :]
            b_ref[g, s * PITCH:s * PITCH + ST, :] = b[s * ST:(s + 1) * ST, :]


def _scan(a_ref, b_ref, h_ref, carry_ref, reverse):
    hs = [carry_ref[:, LANES * g:LANES * (g + 1)] for g in range(N_LANE_GROUPS)]
    for k in range(ST):
        t = (ST - 1 - k) if reverse else k
        for g in range(N_LANE_GROUPS):
            a = a_ref[g, pl.ds(t, NB, stride=PITCH), :]
            b = b_ref[g, pl.ds(t, NB, stride=PITCH), :]
            hs[g] = a * hs[g] + b
            h_ref[g, pl.ds(t, NB, stride=PITCH), :] = hs[g]
    for g in range(N_LANE_GROUPS):
        carry_ref[:, LANES * g:LANES * (g + 1)] = hs[g]


def _bwd_kernel(x_ref, xp_ref, xn_ref, gmix_ref, wxr_ref, cw_ref, cb_ref, wg_ref, ba_ref, bx_ref, lam_ref,
                xc_ref, hb_ref, a_ref, b_ref, h_ref, carry_ref):
    j = pl.program_id(1)
    n_t = pl.num_programs(1)
    tb = n_t - 1 - j

    @pl.when(j == 0)
    def _():
        carry_ref[...] = jnp.zeros_like(carry_ref)

    gm = gmix_ref[...]

    def proj(xv):
        return jnp.dot(_rms(xv, gm).astype(bf16), wxr_ref[...], preferred_element_type=f32)

    xr = proj(x_ref[...].reshape(TT, D_MODEL)).reshape(NB, ST, D_REC)
    hp = proj(xp_ref[...].reshape(NB * HALO, D_MODEL)).reshape(NB, HALO, D_REC)
    hn = proj(xn_ref[...].reshape(NB * HALO, D_MODEL)).reshape(NB, HALO, D_REC)
    ext = jnp.concatenate([jnp.where(tb > 0, hp, 0.0), xr, jnp.where(tb < n_t - 1, hn, 0.0)], axis=1)

    cw = cw_ref[...]
    xc = None
    for k in range(CONV_W):
        off = HALO - CONV_LEFT + k
        term = ext[:, off:off + ST, :] * cw[k:k + 1, :]
        xc = term if xc is None else xc + term
    xc = xc + cb_ref[...]
    xc_ref[...] = xc

    ab = _gate_ab(xc.reshape(TT, D_REC), wg_ref, ba_ref[...], bx_ref[...], lam_ref[...])
    _store_ab(ab, a_ref, b_ref)
    _scan(a_ref, b_ref, h_ref, carry_ref, reverse=True)
    for g in range(N_LANE_GROUPS):
        for s in range(NB):
            hb_ref[s, :, LANES * g:LANES * (g + 1)] = h_ref[g, s * PITCH:s * PITCH + ST, :]


def _route(logits):
    gl = [logits[i:i + 1, :] for i in range(N_GROUPS)]
    gm = jnp.maximum(jnp.maximum(gl[0], gl[1]), jnp.maximum(gl[2], gl[3]))
    sg = sum(jnp.exp(v - gm) for v in gl)
    g_top = 1.0 / sg
    gidx = jnp.where(gl[0] == gm, 0, jnp.where(gl[1] == gm, 1, jnp.where(gl[2] == gm, 2, 3)))
    el = []
    for k in range(N_EXP_PER_GROUP):
        rows = [logits[N_GROUPS + N_EXP_PER_GROUP * g + k:N_GROUPS + N_EXP_PER_GROUP * g + k + 1, :]
                for g in range(N_GROUPS)]
        el.append(jnp.where(gidx == 0, rows[0], jnp.where(gidx == 1, rows[1], jnp.where(gidx == 2, rows[2], rows[3]))))
    em = jnp.maximum(jnp.maximum(el[0], el[1]), jnp.maximum(el[2], el[3]))
    ee = [jnp.exp(v - em) for v in el]
    se = ee[0] + ee[1] + ee[2] + ee[3]
    p = [v / se for v in ee]
    p1 = jnp.maximum(jnp.maximum(p[0], p[1]), jnp.maximum(p[2], p[3]))
    i1 = jnp.where(p[0] == p1, 0, jnp.where(p[1] == p1, 1, jnp.where(p[2] == p1, 2, 3)))
    q = [jnp.where(i1 == k, -1.0, p[k]) for k in range(N_EXP_PER_GROUP)]
    p2 = jnp.maximum(jnp.maximum(q[0], q[1]), jnp.maximum(q[2], q[3]))
    i2 = jnp.where(q[0] == p2, 0, jnp.where(q[1] == p2, 1, jnp.where(q[2] == p2, 2, 3)))
    norm = p1 + p2
    w1 = g_top * (p1 / norm)
    w2 = g_top * (p2 / norm)
    first_is_lo = i1 < i2
    lo = jnp.where(first_is_lo, i1, i2)
    hi = jnp.where(first_is_lo, i2, i1)
    w_lo = jnp.where(first_is_lo, w1, w2)
    w_hi = jnp.where(first_is_lo, w2, w1)
    base = jnp.where(lo == 0, 0, jnp.where(lo == 1, 3, 5))
    cls = gidx * N_PAIRS + base + hi - lo - 1
    return cls.astype(f32), w_lo, w_hi


def _fwd_kernel(x_ref, xc_ref, hb_ref, p_ref, gmix_ref, wgr_ref, wuvt_ref, wg_ref, ba_ref, bx_ref, lam_ref,
                lng_ref, lnb_ref, bd_ref, bst_ref, grec_ref, gsgu_ref, wor_ref, wos_ref, gffn_ref, wrt_ref, brt_ref,
                hx_ref, cls_ref, a_ref, b_ref, h_ref, carry_ref):
    j = pl.program_id(1)

    @pl.when(j == 0)
    def _():
        carry_ref[...] = jnp.zeros_like(carry_ref)

    x2 = x_ref[...].reshape(TT, D_MODEL)
    n = _rms(x2, gmix_ref[...]).astype(bf16)

    ab = _gate_ab(xc_ref[...].reshape(TT, D_REC), wg_ref, ba_ref[...], bx_ref[...], lam_ref[...])
    _store_ab(ab, a_ref, b_ref)
    _scan(a_ref, b_ref, h_ref, carry_ref, reverse=False)
    hf = jnp.concatenate(
        [jnp.concatenate([h_ref[g, s * PITCH:s * PITCH + ST, :] for s in range(NB)], axis=0)
         for g in range(N_LANE_GROUPS)], axis=1)
    gr = jnp.dot(n, wgr_ref[...], preferred_element_type=f32)
    y_rec = (hf + hb_ref[...].reshape(TT, D_REC)) * jax.nn.gelu(gr)
    rec_n = _rms(y_rec, grec_ref[...]).astype(bf16)

    zt = lax.dot_general(wuvt_ref[...], n, (((1,), (1,)), ((), ())), preferred_element_type=f32)
    ut = jax.nn.gelu(zt[:D_SGU, :])
    vt = jax.nn.gelu(zt[D_SGU:, :])
    mu = jnp.mean(vt, axis=0, keepdims=True)
    vc = vt - mu
    var = jnp.mean(vc * vc, axis=0, keepdims=True)
    vn = (vc * lax.rsqrt(var + EPS) * lng_ref[...] + lnb_ref[...]).astype(bf16)
    n_slab = TT // MXU_DIM
    heads = []
    for h in range(N_SGU_HEADS):
        rows = vn[SGU_HEAD * h:SGU_HEAD * (h + 1), :]
        lhs = jnp.concatenate([rows[:, MXU_DIM * c:MXU_DIM * (c + 1)] for c in range(n_slab)], axis=0)
        res = jnp.dot(lhs, bd_ref[h], preferred_element_type=f32)
        heads.append(jnp.concatenate([res[SGU_HEAD * c:SGU_HEAD * (c + 1), :] for c in range(n_slab)], axis=1))
    mixed = jnp.concatenate(heads, axis=0) + jnp.concatenate([bst_ref[...]] * (TT // CHUNK), axis=1)
    yst = ut * mixed
    ms = jnp.mean(yst * yst, axis=0, keepdims=True)
    sgu_n = (yst * lax.rsqrt(ms + EPS) * gsgu_ref[...]).T.astype(bf16)

    delta = (jnp.dot(rec_n, wor_ref[...], preferred_element_type=f32)
             + jnp.dot(sgu_n, wos_ref[...], preferred_element_type=f32))
    h1 = x2 + delta
    hx_ref[:, :, 0:D_MODEL] = h1.reshape(NB, ST, D_MODEL)
    hx_ref[:, :, D_MODEL:D_MODEL + D_PLE] = p_ref[...]

    n2 = _rms(h1, gffn_ref[...]).astype(bf16)
    logits = lax.dot_general(wrt_ref[...], n2, (((1,), (1,)), ((), ())), preferred_element_type=f32) + brt_ref[...]
    cls, w_lo, w_hi = _route(logits)
    rt = jnp.concatenate([cls, w_lo, w_hi, jnp.zeros((LANES - 3, TT), f32)], axis=0).T
    hx_ref[:, :, D_MODEL + D_PLE:D_ROW] = rt.reshape(NB, ST, LANES)
    cls_ref[0] = jnp.concatenate([cls, jnp.zeros((SUBLANES - 1, TT), f32)], axis=0)


def _expert_kernel(elo_ref, ehi_ref, nval_ref, ci_ref, ni_ref, pi_ref, hx_hbm, gffn_ref,
                   w1lo_ref, w3lo_ref, w2lo_ref, w1hi_ref, w3hi_ref, w2hi_ref,
                   gple_ref, wpg_ref, wpp_ref, gfin_ref, y_hbm, xbuf0, xbuf1, ybuf0, ybuf1, gsem, ssem):
    i = pl.program_id(0)
    n = pl.num_programs(0)
    cnt = nval_ref[i]
    nxt = jnp.minimum(i + 1, n - 1)
    n_grp = TM // SUBLANES

    def gather_row(i_ref, buf, par, r):
        return pltpu.make_async_copy(hx_hbm.at[pl.ds(i_ref[0, 0, r], 1)], buf.at[pl.ds(r, 1)], gsem.at[par])

    def scatter_row(i_ref, buf, par, r):
        return pltpu.make_async_copy(buf.at[pl.ds(r, 1)], y_hbm.at[pl.ds(i_ref[0, 0, r], 1)], ssem.at[par])

    def wait_tile_in(buf, par):
        pltpu.make_async_copy(hx_hbm.at[pl.ds(0, TM)], buf, gsem.at[par]).wait()

    def wait_tile_out(buf, par):
        pltpu.make_async_copy(buf, y_hbm.at[pl.ds(0, TM)], ssem.at[par]).wait()

    def expert(n2, w1_ref, w3_ref, w2_ref):
        a = jnp.dot(n2, w1_ref[0], preferred_element_type=f32)
        b = jnp.dot(n2, w3_ref[0], preferred_element_type=f32)
        hdn = (jax.nn.silu(a) * b).astype(bf16)
        return jnp.dot(hdn, w2_ref[0], preferred_element_type=f32)

    def compute(xb, yb):
        xrow = xb[...]
        xs = xrow[:, 0:D_MODEL]
        ws = xrow[:, D_MODEL + D_PLE:D_ROW]
        n2 = _rms(xs, gffn_ref[...]).astype(bf16)
        out = ws[:, 1:2] * expert(n2, w1lo_ref, w3lo_ref, w2lo_ref)
        out = out + ws[:, 2:3] * expert(n2, w1hi_ref, w3hi_ref, w2hi_ref)
        h2 = xs + out
        n3 = _rms(h2, gple_ref[...]).astype(bf16)
        gate = jax.nn.sigmoid(jnp.dot(n3, wpg_ref[...], preferred_element_type=f32))
        ps = xrow[:, D_MODEL:D_MODEL + D_PLE]
        proj = jnp.dot(ps.astype(bf16), wpp_ref[...], preferred_element_type=f32)
        h3 = h2 + gate * proj
        yb[...] = _rms(h3, gfin_ref[...])

    def step(par):
        xb, xo = (xbuf0, xbuf1) if par == 0 else (xbuf1, xbuf0)
        yb, yo = (ybuf0, ybuf1) if par == 0 else (ybuf1, ybuf0)

        @pl.when(i == 0)
        def _():
            def body(k, c):
                for u in range(SUBLANES):
                    gather_row(ci_ref, xb, par, k * SUBLANES + u).start(priority=u % 2)
                return c
            lax.fori_loop(0, n_grp, body, 0)
            yo[...] = jnp.zeros_like(yo)

        wait_tile_in(xb, par)

        @pl.when(i >= 1)
        def _():
            wait_tile_out(yb, par)

        for r in range(TM):
            gather_row(ni_ref, xo, 1 - par, r).start(priority=r % 2)
        for r in range(TM):
            scatter_row(pi_ref, yo, 1 - par, r).start(priority=r % 2)
        compute(xb, yb)

        @pl.when((i == n - 1) | (nval_ref[nxt] == 0))
        def _():
            wait_tile_in(xo, 1 - par)
            wait_tile_out(yo, 1 - par)

            def send(r, c):
                scatter_row(ci_ref, yb, par, r).start()
                return c
            lax.fori_loop(0, cnt, send, 0)

            def drain(r, c):
                scatter_row(ci_ref, yb, par, 0).wait()
                return c
            lax.fori_loop(0, cnt, drain, 0)

    @pl.when(cnt > 0)
    def _():
        @pl.when(i % 2 == 0)
        def _():
            step(0)

        @pl.when(i % 2 == 1)
        def _():
            step(1)


def _const_spec(shape):
    nd = len(shape)
    return pl.BlockSpec(shape, lambda *_: (0,) * nd)


def _gate_blocks(w_a, w_x):
    def pair(w):
        z = jnp.zeros((REC_HEAD, REC_HEAD), w.dtype)
        return jnp.stack([jnp.block([[w[2 * g], z], [z, w[2 * g + 1]]]) for g in range(N_LANE_GROUPS)])
    return jnp.concatenate([pair(w_a), pair(w_x)], axis=-1).astype(bf16)


def _mix_trunk(x, p, P):
    bsz, seq, _ = x.shape
    assert bsz % NB == 0 and seq % ST == 0
    n_g, n_t = bsz // NB, seq // ST
    halo_blocks = seq // HALO
    row = lambda v: v.reshape(1, -1)

    x_spec_b = pl.BlockSpec((NB, ST, D_MODEL), lambda g, j: (g, n_t - 1 - j, 0))
    xp_spec = pl.BlockSpec((NB, HALO, D_MODEL),
                           lambda g, j: (g, jnp.maximum((n_t - 1 - j) * (ST // HALO) - 1, 0), 0))
    xn_spec = pl.BlockSpec((NB, HALO, D_MODEL),
                           lambda g, j: (g, jnp.minimum((n_t - j) * (ST // HALO), halo_blocks - 1), 0))
    rec_spec_b = pl.BlockSpec((NB, ST, D_REC), lambda g, j: (g, n_t - 1 - j, 0))
    scan_scratch = [pltpu.VMEM((N_LANE_GROUPS, NB * PITCH, LANES), f32),
                    pltpu.VMEM((N_LANE_GROUPS, NB * PITCH, LANES), f32),
                    pltpu.VMEM((N_LANE_GROUPS, NB * PITCH, LANES), f32),
                    pltpu.VMEM((NB, D_REC), f32)]

    bwd_w = [row(P['g_mix']), P['w_xr'], P['conv_w'], row(P['conv_b']), P['wg_b'],
             row(P['ba_b']), row(P['bx_b']), row(P['lam_b'])]
    xc, hb = pl.pallas_call(
        _bwd_kernel,
        grid=(n_g, n_t),
        in_specs=[x_spec_b, xp_spec, xn_spec] + [_const_spec(w.shape) for w in bwd_w],
        out_specs=[rec_spec_b, rec_spec_b],
        out_shape=[jax.ShapeDtypeStruct((bsz, seq, D_REC), f32)] * 2,
        scratch_shapes=scan_scratch,
        compiler_params=pltpu.CompilerParams(dimension_semantics=("arbitrary", "arbitrary"),
                                             vmem_limit_bytes=VMEM_LIMIT),
        name="bwd_scan",
    )(x, x, x, *bwd_w)

    tile_spec = lambda d: pl.BlockSpec((NB, ST, d), lambda g, j: (g, j, 0))
    fwd_w = [row(P['g_mix']), P['w_gr'], P['w_uvt'], P['wg_f'], row(P['ba_f']), row(P['bx_f']), row(P['lam_f']),
             P['ln_g'], P['ln_b'], P['bd'], P['bst'], row(P['g_rec']), P['g_sgu'], P['w_or'], P['w_os'],
             row(P['g_ffn']), P['w_rt'], P['b_rt']]
    hx, cls_t = pl.pallas_call(
        _fwd_kernel,
        grid=(n_g, n_t),
        in_specs=[tile_spec(D_MODEL), tile_spec(D_REC), tile_spec(D_REC), tile_spec(D_PLE)]
                 + [_const_spec(w.shape) for w in fwd_w],
        out_specs=[tile_spec(D_ROW), pl.BlockSpec((1, SUBLANES, TT), lambda g, j: (g * n_t + j, 0, 0))],
        out_shape=[jax.ShapeDtypeStruct((bsz, seq, D_ROW), f32),
                   jax.ShapeDtypeStruct((n_g * n_t, SUBLANES, TT), f32)],
        scratch_shapes=scan_scratch,
        compiler_params=pltpu.CompilerParams(dimension_semantics=("arbitrary", "arbitrary"),
                                             vmem_limit_bytes=VMEM_LIMIT),
        name="fwd_mix",
    )(x, xc, hb, p, *fwd_w)
    cls = cls_t[:, 0, :].reshape(n_g, n_t, NB, ST).transpose(0, 2, 1, 3).reshape(bsz * seq)
    return hx.reshape(bsz * seq, D_ROW), cls.astype(jnp.int32)


def _expert_stage(hx, cls, P):
    n_tok = hx.shape[0]
    assert n_tok % TM == 0
    n_tiles = n_tok // TM + N_CLASSES
    assert n_tok // TM > N_CLASSES
    i32 = jnp.int32

    class_ids = jnp.arange(N_CLASSES, dtype=i32)
    counts = jnp.sum((cls[:, None] == class_ids[None, :]).astype(i32), axis=0)
    order = jnp.argsort(cls, stable=True).astype(i32)
    dense_start = jnp.cumsum(counts) - counts
    full_c, rem_c = counts // TM, counts % TM
    cum_part = jnp.cumsum((rem_c > 0).astype(i32))
    cum_full = jnp.cumsum(full_c)
    n_part, n_full = cum_part[-1], cum_full[-1]
    tile_ids = jnp.arange(n_tiles, dtype=i32)
    is_part = tile_ids < n_part
    active = tile_ids < n_part + n_full
    first_reaching = lambda cum, k: jnp.minimum(
        jnp.sum((cum[None, :] <= k[:, None]).astype(i32), axis=1), N_CLASSES - 1)
    cls_part = first_reaching(cum_part, tile_ids)
    full_id = jnp.maximum(tile_ids - n_part, 0)
    cls_full = first_reaching(cum_full, full_id)
    tile_cls = jnp.where(is_part, cls_part, cls_full)
    n_valid = jnp.where(active, jnp.where(is_part, rem_c[tile_cls], TM), 0)
    in_class = jnp.where(is_part, full_c[tile_cls], full_id - (cum_full - full_c)[tile_cls]) * TM
    tile_dense = dense_start[tile_cls] + in_class
    rank_end = jnp.cumsum(n_valid)
    rank_start = rank_end - n_valid
    rows = jnp.arange(TM, dtype=i32)[None, :]
    pos = tile_dense[:, None] + rows
    ahead = jnp.minimum(rank_start[:N_CLASSES, None] + rows, n_tok - 1)
    holds = (rank_start[None, None, :] <= ahead[:, :, None]) & (ahead[:, :, None] < rank_end[None, None, :])
    ahead_pos = ahead + jnp.sum(jnp.where(holds, (tile_dense - rank_start)[None, None, :], 0), axis=2)
    pos = pos.at[:N_CLASSES].set(jnp.where(rows < n_valid[:N_CLASSES, None], pos[:N_CLASSES], ahead_pos))
    src = order[jnp.clip(pos, 0, n_tok - 1)].reshape(n_tiles, 1, TM)

    pair_lo = jnp.array([0, 0, 0, 1, 1, 2], jnp.int32)
    pair_hi = jnp.array([1, 2, 3, 2, 3, 3], jnp.int32)
    e_lo = (tile_cls // N_PAIRS) * N_EXP_PER_GROUP + pair_lo[tile_cls % N_PAIRS]
    e_hi = (tile_cls // N_PAIRS) * N_EXP_PER_GROUP + pair_hi[tile_cls % N_PAIRS]

    row = lambda v: v.reshape(1, -1)
    idx_spec = lambda f: pl.BlockSpec((1, 1, TM), f, memory_space=pltpu.SMEM)
    any_spec = pl.BlockSpec(memory_space=pl.ANY)
    lo_spec = lambda s: pl.BlockSpec((1,) + s, lambda i, elo, ehi, nv: (elo[i], 0, 0))
    hi_spec = lambda s: pl.BlockSpec((1,) + s, lambda i, elo, ehi, nv: (ehi[i], 0, 0))
    up_shape, down_shape = (D_MODEL, D_EXPERT), (D_EXPERT, D_MODEL)
    consts = [row(P['g_ple']), P['w_pg'], P['w_pp'], row(P['g_final'])]
    this_tile = lambda i, *_: (i, 0, 0)
    next_tile = lambda i, *_: (jnp.minimum(i + 1, n_tiles - 1), 0, 0)
    prev_tile = lambda i, *_: (jnp.maximum(i - 1, 0), 0, 0)
    tile_buf = lambda d: pltpu.VMEM((TM, d), f32)
    y = pl.pallas_call(
        _expert_kernel,
        grid_spec=pltpu.PrefetchScalarGridSpec(
            num_scalar_prefetch=3,
            grid=(n_tiles,),
            in_specs=[idx_spec(this_tile), idx_spec(next_tile), idx_spec(prev_tile),
                      any_spec, _const_spec((1, D_MODEL)),
                      lo_spec(up_shape), lo_spec(up_shape), lo_spec(down_shape),
                      hi_spec(up_shape), hi_spec(up_shape), hi_spec(down_shape)]
                     + [_const_spec(w.shape) for w in consts],
            out_specs=any_spec,
            scratch_shapes=[tile_buf(D_ROW), tile_buf(D_ROW), tile_buf(D_MODEL), tile_buf(D_MODEL),
                            pltpu.SemaphoreType.DMA((2,)), pltpu.SemaphoreType.DMA((2,))],
        ),
        out_shape=jax.ShapeDtypeStruct((n_tok, D_MODEL), f32),
        compiler_params=pltpu.CompilerParams(dimension_semantics=("arbitrary",), vmem_limit_bytes=VMEM_LIMIT),
        name="experts",
    )(e_lo, e_hi, n_valid.astype(i32), src, src, src, hx, row(P['g_ffn']),
      P['w1'], P['w3'], P['w2'], P['w1'], P['w3'], P['w2'], *consts)
    return y


def _prepare(g_mix, w_in, conv_w, conv_b, lru_w_a, lru_b_a, lru_w_x, lru_b_x, lru_lambda, sgu_ln_g, sgu_ln_b,
             sgu_w_s, sgu_b_s, g_rec_out, g_sgu_out, w_out, g_ffn, w_router_group, b_router_group,
             w_router_expert, b_router_expert, w_exp_gate, w_exp_up, w_exp_down, g_ple, w_ple_gate, w_ple_proj,
             g_final):
    l = 0
    w = w_in[l]
    ws_t = jnp.transpose(sgu_w_s[l], (0, 2, 1))
    z = jnp.zeros((CHUNK, CHUNK), f32)
    bd = jnp.stack([jnp.block([[ws_t[h], z], [z, ws_t[h]]]) for h in range(N_SGU_HEADS)]).astype(bf16)
    w_rt = jnp.concatenate([w_router_group[l].T, w_router_expert[l].T,
                            jnp.zeros((ROUTE_ROWS - N_GROUPS - N_EXPERTS, D_MODEL), f32)], axis=0).astype(bf16)
    b_rt = jnp.concatenate([b_router_group[l], b_router_expert[l],
                            jnp.zeros((ROUTE_ROWS - N_GROUPS - N_EXPERTS,), f32)]).reshape(ROUTE_ROWS, 1)
    return dict(
        g_mix=g_mix[l], w_xr=w[:, :D_REC].astype(bf16), w_gr=w[:, D_REC:2 * D_REC].astype(bf16),
        w_uvt=w[:, 2 * D_REC:].T.astype(bf16),
        conv_w=conv_w[l], conv_b=conv_b[l],
        wg_f=_gate_blocks(lru_w_a[l, 0], lru_w_x[l, 0]), wg_b=_gate_blocks(lru_w_a[l, 1], lru_w_x[l, 1]),
        ba_f=lru_b_a[l, 0], ba_b=lru_b_a[l, 1], bx_f=lru_b_x[l, 0], bx_b=lru_b_x[l, 1],
        lam_f=lru_lambda[l, 0], lam_b=lru_lambda[l, 1],
        ln_g=sgu_ln_g[l].reshape(D_SGU, 1), ln_b=sgu_ln_b[l].reshape(D_SGU, 1),
        bd=bd, bst=jnp.repeat(sgu_b_s[l], SGU_HEAD, axis=0),
        g_rec=g_rec_out[l], g_sgu=g_sgu_out[l].reshape(D_SGU, 1),
        w_or=w_out[l, :D_REC].astype(bf16), w_os=w_out[l, D_REC:].astype(bf16),
        g_ffn=g_ffn[l], w_rt=w_rt, b_rt=b_rt,
        w1=w_exp_gate[l].astype(bf16), w3=w_exp_up[l].astype(bf16), w2=w_exp_down[l].astype(bf16),
        g_ple=g_ple[l], w_pg=w_ple_gate[l].astype(bf16), w_pp=w_ple_proj[l].astype(bf16), g_final=g_final,
    )


def kernel(x_prompt, x_sample, p_prompt, p_sample, g_mix, w_in, conv_w, conv_b, lru_w_a, lru_b_a, lru_w_x, lru_b_x, lru_lambda, sgu_ln_g, sgu_ln_b, sgu_w_s, sgu_b_s, g_rec_out, g_sgu_out, w_out, g_ffn, w_router_group, b_router_group, w_router_expert, b_router_expert, w_exp_gate, w_exp_up, w_exp_down, g_ple, w_ple_gate, w_ple_proj, g_final):
    assert w_in.shape[0] == 1, "single-layer trunk"
    P = _prepare(g_mix, w_in, conv_w, conv_b, lru_w_a, lru_b_a, lru_w_x, lru_b_x, lru_lambda, sgu_ln_g, sgu_ln_b,
                 sgu_w_s, sgu_b_s, g_rec_out, g_sgu_out, w_out, g_ffn, w_router_group, b_router_group,
                 w_router_expert, b_router_expert, w_exp_gate, w_exp_up, w_exp_down, g_ple, w_ple_gate,
                 w_ple_proj, g_final)
    outs = []
    for x, p in ((x_prompt, p_prompt), (x_sample, p_sample)):
        y = _expert_stage(*_mix_trunk(x, p[0], P), P)
        outs.append(y.reshape(x.shape))
    return tuple(outs)
```
